```python
import jax
import jax.numpy as jnp
from jax import lax
import numpy as np

D_MODEL = 1024
BATCH = 8
SEQ = 4096
DEPTH = 2

CTX_LEN = 256
GRID_W = 64
HEAD_DIM = 64
NA_WIDTH = D_MODEL // 2
NA_HEADS = NA_WIDTH // HEAD_DIM
NA_WIN_ROWS = 8
NA_WIN_COLS = 16
NA_QCB = 16
NA_KCB = 32
RW_WIDTH = D_MODEL // 4
RW_HEADS = RW_WIDTH // HEAD_DIM
RW_DECAY_RANK = 64
RW_A_RANK = 64
RW_GATE_RANK = 128
RW_LORA = RW_DECAY_RANK + RW_A_RANK + RW_GATE_RANK
RW_IN = 3 * RW_WIDTH + 2 * RW_LORA
RW_GN_EPS = 64e-5
CV_WIDTH = D_MODEL // 4
CV_CONV_LEN = 31
MIX_WIDTH = NA_WIDTH + RW_WIDTH + CV_WIDTH
IN_WIDTH = 3 * NA_WIDTH + RW_IN + 2 * CV_WIDTH
D_FF = 2816
N_EXPERTS = 8
TOP_K = 2
D_FF_EXPERT = 3584
MOE_BLOCK = 128
N_DENSE_LAYERS = (DEPTH + 1) // 2
N_MOE_LAYERS = DEPTH // 2
RMS_EPS = 1e-6
LN_EPS = 1e-5

kernel_name = 'hybrid_natten_rwkv7_conformer_moe_dit'


def _rmsnorm(x, g):
    xf = x.astype(jnp.float32)
    y = xf * lax.rsqrt(jnp.mean(xf * xf, axis=-1, keepdims=True) + RMS_EPS)
    return (y * g.astype(jnp.float32)).astype(x.dtype)


def _layernorm(x, g, b):
    xf = x.astype(jnp.float32)
    mu = jnp.mean(xf, axis=-1, keepdims=True)
    var = jnp.mean(jnp.square(xf - mu), axis=-1, keepdims=True)
    y = (xf - mu) * lax.rsqrt(var + LN_EPS)
    return (y * g.astype(jnp.float32) + b.astype(jnp.float32)).astype(x.dtype)


def _modulate(h, shift, scale):
    return h * (1 + scale) + shift


def _swiglu(h, w_gate, w_up, w_down):
    return (jax.nn.silu(h @ w_gate) * (h @ w_up)) @ w_down


def _moe_swiglu(h, router, w_gate, w_up, w_down):
    shp = h.shape
    d = shp[-1]
    xt = h.reshape(-1, d)
    n_tok = xt.shape[0]
    n_asg = n_tok * TOP_K
    logits = (xt @ router).astype(jnp.float32)
    top_logit, top_e = lax.top_k(logits, TOP_K)
    gates = jax.nn.softmax(top_logit, axis=-1)
    flat_e = top_e.reshape(-1)
    flat_tok = jnp.arange(n_asg, dtype=jnp.int32) // TOP_K
    order = jnp.argsort(flat_e)
    s_e = flat_e[order]
    s_tok = flat_tok[order]
    s_gate = gates.reshape(-1)[order]
    counts = jnp.bincount(flat_e, length=N_EXPERTS)
    padded = (counts + MOE_BLOCK - 1) // MOE_BLOCK * MOE_BLOCK
    pad_end = jnp.cumsum(padded)
    pad_start = pad_end - padded
    start = jnp.cumsum(counts) - counts
    dest = pad_start[s_e] + jnp.arange(n_asg, dtype=jnp.int32) - start[s_e]
    n_blocks = (n_asg + MOE_BLOCK - 1) // MOE_BLOCK + N_EXPERTS
    slot_tok = jnp.full((n_blocks * MOE_BLOCK,), n_tok, jnp.int32).at[dest].set(s_tok)
    block_e = jnp.minimum(jnp.searchsorted(pad_end, jnp.arange(n_blocks) * MOE_BLOCK, side='right'), N_EXPERTS - 1)
    x_pad = jnp.concatenate([xt, jnp.zeros((1, d), xt.dtype)], axis=0)
    xb = x_pad[slot_tok].reshape(n_blocks, MOE_BLOCK, d)

    def expert_block(args):
        x_blk, e = args
        return _swiglu(x_blk, w_gate[e], w_up[e], w_down[e])

    yb = lax.map(expert_block, (xb, block_e)).reshape(-1, d)
    y = jnp.zeros_like(xt).at[s_tok].add(yb[dest] * s_gate[:, None].astype(xt.dtype))
    return y.reshape(shp)


def _na_column_layout():
    n_cb = GRID_W // NA_QCB
    q_cols = np.arange(GRID_W).reshape(n_cb, NA_QCB)
    blk_start = np.clip(np.arange(n_cb) * NA_QCB - NA_WIN_COLS // 2, 0, GRID_W - NA_KCB)
    key_cols = blk_start[:, None] + np.arange(NA_KCB)[None, :]
    win_start = np.clip(q_cols - NA_WIN_COLS // 2, 0, GRID_W - NA_WIN_COLS)
    kcol = key_cols[:, None, :]
    ws = win_start[:, :, None]
    mask = (kcol >= ws) & (kcol < ws + NA_WIN_COLS)
    rel_idx = np.clip(kcol - q_cols[:, :, None] + NA_WIN_COLS - 1, 0, 2 * NA_WIN_COLS - 2)
    return key_cols, mask, rel_idx


def _neighbourhood_attention(q, k, v, kc, vc, rpb):
    B, N, H, dh = q.shape
    rows = N // GRID_W
    kr = min(NA_WIN_ROWS, rows)
    n_cb = GRID_W // NA_QCB
    key_cols, col_mask, rel_idx = _na_column_layout()
    qg = (q * dh ** -0.5).reshape(B, rows, GRID_W, H, dh).transpose(1, 0, 3, 2, 4)
    kg = k.reshape(B, rows, GRID_W, H, dh).transpose(0, 3, 1, 2, 4)
    vg = v.reshape(B, rows, GRID_W, H, dh).transpose(0, 3, 1, 2, 4)
    kct = kc.transpose(0, 2, 1, 3)
    vct = vc.transpose(0, 2, 1, 3)
    rpb_cols = rpb[:, :, rel_idx]
    n_win = kr * NA_KCB

    def one_row(args):
        i, qi = args
        rs = jnp.clip(i - kr // 2, 0, rows - kr)
        ks = lax.dynamic_slice_in_dim(kg, rs, kr, axis=2)[:, :, :, key_cols]
        vs = lax.dynamic_slice_in_dim(vg, rs, kr, axis=2)[:, :, :, key_cols]
        qb = qi.reshape(B, H, n_cb, NA_QCB, dh)
        s_win = jnp.einsum('bhcqd,bhrckd->bhcqrk', qb, ks).astype(jnp.float32)
        bias = lax.dynamic_slice_in_dim(rpb_cols, rs - i + NA_WIN_ROWS - 1, kr, axis=1)
        bias = bias.transpose(0, 2, 3, 1, 4).astype(jnp.float32)
        s_win = jnp.where(col_mask[:, :, None, :], s_win + bias, -jnp.inf)
        s_ctx = jnp.einsum('bhcqd,bhld->bhcql', qb, kct).astype(jnp.float32)
        s = jnp.concatenate([s_win.reshape(B, H, n_cb, NA_QCB, n_win), s_ctx], axis=-1)
        p = jax.nn.softmax(s, axis=-1).astype(v.dtype)
        p_win = p[..., :n_win].reshape(B, H, n_cb, NA_QCB, kr, NA_KCB)
        o = (jnp.einsum('bhcqrk,bhrckd->bhcqd', p_win, vs)
             + jnp.einsum('bhcql,bhld->bhcqd', p[..., n_win:], vct))
        return o.reshape(B, H, GRID_W, dh)

    out = lax.map(one_row, (jnp.arange(rows), qg))
    return out.transpose(1, 0, 3, 2, 4).reshape(B, N, H * dh)


def _context_attention(q, k, v):
    B, L, H, dh = q.shape
    s = jnp.einsum('blhd,bmhd->bhlm', q, k).astype(jnp.float32) * (dh ** -0.5)
    p = jax.nn.softmax(s, axis=-1).astype(v.dtype)
    return jnp.einsum('bhlm,bmhd->blhd', p, v).reshape(B, L, H * dh)


def _token_shift(u, mu_prev, mu_next):
    zero = jnp.zeros_like(u[:, :1])
    prev = jnp.concatenate([zero, u[:, :-1]], axis=1)
    nxt = jnp.concatenate([u[:, 1:], zero], axis=1)
    return u + mu_prev * (prev - u) + mu_next * (nxt - u)


def _head_norm(y, g, b):
    mu = jnp.mean(y, axis=-1, keepdims=True)
    var = jnp.mean(jnp.square(y - mu), axis=-1, keepdims=True)
    yn = (y - mu) * lax.rsqrt(var + RW_GN_EPS)
    return yn.reshape(y.shape[0], y.shape[1], -1) * g + b


def _rwkv7_scan(r, w, k, v, z, b):
    G, H, dh = r.shape[1:]

    def step(S, inp):
        r_t, w_t, k_t, v_t, z_t, b_t = inp
        s_z = jnp.einsum('ghvk,ghk->ghv', S, z_t)
        S = S * w_t[:, :, None, :] + s_z[..., None] * b_t[:, :, None, :] + v_t[..., None] * k_t[:, :, None, :]
        return S, jnp.einsum('ghvk,ghk->ghv', S, r_t)

    _, y = lax.scan(step, jnp.zeros((G, H, dh, dh), jnp.float32), (r, w, k, v, z, b))
    return y


def _bi_rwkv7(ul, uc, mu_prev, mu_next, w0, w2, a0, a2, g2, k_k, k_a, r_k, gn_g, gn_b, need_ctx_out):
    B, N, _ = ul.shape
    L = uc.shape[1]
    T = L + N
    u = jnp.concatenate([_token_shift(uc, mu_prev, mu_next), _token_shift(ul, mu_prev, mu_next)],
                        axis=1).astype(jnp.float32)
    heads = lambda t: t.reshape(B, T, RW_HEADS, HEAD_DIM)
    r = u[..., :RW_WIDTH]
    k = u[..., RW_WIDTH:2 * RW_WIDTH]
    v = u[..., 2 * RW_WIDTH:3 * RW_WIDTH]
    kk = heads(k * k_k)
    kk = kk * lax.rsqrt(jnp.sum(kk * kk, axis=-1, keepdims=True) + 1e-12)
    rh, vh = heads(r), heads(v)
    per_dir = []
    for d in range(2):
        lo = 3 * RW_WIDTH + d * RW_LORA
        wd = u[..., lo:lo + RW_DECAY_RANK]
        ad = u[..., lo + RW_DECAY_RANK:lo + RW_DECAY_RANK + RW_A_RANK]
        gd = u[..., lo + RW_DECAY_RANK + RW_A_RANK:lo + RW_LORA]
        log_w = -jax.nn.softplus(-(w0[d] + jnp.tanh(wd) @ w2[d])) - 0.5
        decay = jnp.exp(-jnp.exp(log_w))
        a = jax.nn.sigmoid(a0[d] + ad @ a2[d])
        g = jax.nn.sigmoid(gd) @ g2[d]
        kd = heads(k * (1.0 + (a - 1.0) * k_a))
        per_dir.append((heads(decay), kd, kk * heads(a), g))
    perm = np.concatenate([np.arange(L)[::-1], L + np.arange(N)[::-1]])

    def stack(t_f, t_b):
        return jnp.concatenate([t_f, t_b[:, perm]], axis=0).swapaxes(0, 1)

    (w_f, k_f, b_f, _), (w_b, k_b, b_b, _) = per_dir
    y = _rwkv7_scan(stack(rh, rh), stack(w_f, w_b), stack(k_f, k_b), stack(vh, vh),
                    stack(-kk, -kk), stack(b_f, b_b)).swapaxes(0, 1)
    t0 = 0 if need_ctx_out else L
    y_dirs = (y[:B, t0:], y[B:][:, perm][:, t0:])
    outs = []
    for (_, kd, _, g), yd in zip(per_dir, y_dirs):
        bonus = (jnp.sum(rh * kd * r_k, axis=-1, keepdims=True) * vh)[:, t0:].reshape(B, T - t0, RW_WIDTH)
        outs.append((_head_norm(yd, gn_g, gn_b) + bonus) * g[:, t0:])
    out = (outs[0] + outs[1]).astype(ul.dtype)
    out_l = out[:, L - t0:]
    out_c = out[:, :L] if need_ctx_out else None
    return out_l, out_c


def _conformer_conv(u, dw_w, dw_b, ln_g, ln_b):
    h = u[..., :CV_WIDTH] * jax.nn.sigmoid(u[..., CV_WIDTH:])
    h = lax.conv_general_dilated(h, dw_w[:, None, :], (1,), [(CV_CONV_LEN // 2, CV_CONV_LEN // 2)],
                                 dimension_numbers=('NWC', 'WIO', 'NWC'),
                                 feature_group_count=CV_WIDTH) + dw_b
    return jax.nn.silu(_layernorm(h, ln_g, ln_b))


def _mixers(hl, hc, w_in, w_out, na_rpb, rw_mu_prev, rw_mu_next, rw_w0, rw_w2, rw_a0, rw_a2, rw_g2,
            rw_k_k, rw_k_a, rw_r_k, rw_gn_g, rw_gn_b, cv_dw_w, cv_dw_b, cv_ln_g, cv_ln_b, need_ctx_out):
    o_rw = 3 * NA_WIDTH
    o_cv = o_rw + RW_IN
    heads = lambda t: t.reshape(t.shape[0], t.shape[1], NA_HEADS, HEAD_DIM)
    pl = hl @ w_in
    c_off = 0 if need_ctx_out else NA_WIDTH
    pc = hc @ (w_in if need_ctx_out else w_in[:, NA_WIDTH:o_cv])
    kc = heads(pc[..., NA_WIDTH - c_off:2 * NA_WIDTH - c_off])
    vc = heads(pc[..., 2 * NA_WIDTH - c_off:o_rw - c_off])
    na_l = _neighbourhood_attention(heads(pl[..., :NA_WIDTH]), heads(pl[..., NA_WIDTH:2 * NA_WIDTH]),
                                    heads(pl[..., 2 * NA_WIDTH:o_rw]), kc, vc, na_rpb)
    rw_l, rw_c = _bi_rwkv7(pl[..., o_rw:o_cv], pc[..., o_rw - c_off:o_cv - c_off], rw_mu_prev, rw_mu_next,
                           rw_w0, rw_w2, rw_a0, rw_a2, rw_g2, rw_k_k, rw_k_a, rw_r_k, rw_gn_g, rw_gn_b,
                           need_ctx_out)
    cv_l = _conformer_conv(pl[..., o_cv:], cv_dw_w, cv_dw_b, cv_ln_g, cv_ln_b)
    out_l = jnp.concatenate([na_l, rw_l, cv_l], axis=-1) @ w_out
    if not need_ctx_out:
        return out_l, None
    na_c = _context_attention(heads(pc[..., :NA_WIDTH]), kc, vc)
    cv_c = _conformer_conv(pc[..., o_cv:], cv_dw_w, cv_dw_b, cv_ln_g, cv_ln_b)
    out_c = jnp.concatenate([na_c, rw_c, cv_c], axis=-1) @ w_out
    return out_l, out_c


def setup_inputs(seed: int = 0) -> dict:
    key = jax.random.key(seed)
    ks = iter(jax.random.split(key, 40))
    f32 = jnp.float32

    def nrm(shape, std):
        return jax.random.normal(next(ks), shape, f32) * std

    def uni(shape, lo, hi):
        return jax.random.uniform(next(ks), shape, f32, lo, hi)

    D = D_MODEL
    return {
        'x': nrm((BATCH, SEQ, D), 1.0),
        'c': nrm((BATCH, D), 1.0),
        'ctx': nrm((BATCH, CTX_LEN, D), 1.0),
        'c_ctx': nrm((D,), 1.0),
        'norm1_g': 1.0 + nrm((DEPTH, D), 0.02),
        'norm2_g': 1.0 + nrm((DEPTH, D), 0.02),
        'mod_w': nrm((DEPTH, D, 6 * D), 0.5 * D ** -0.5),
        'mod_b': nrm((DEPTH, 6 * D), 0.02),
        'w_in': nrm((DEPTH, D, IN_WIDTH), D ** -0.5),
        'w_out': nrm((DEPTH, MIX_WIDTH, D), MIX_WIDTH ** -0.5),
        'na_rpb': nrm((DEPTH, NA_HEADS, 2 * NA_WIN_ROWS - 1, 2 * NA_WIN_COLS - 1), 0.1),
        'rw_mu_prev': uni((DEPTH, RW_IN), 0.0, 0.5),
        'rw_mu_next': uni((DEPTH, RW_IN), 0.0, 0.5),
        'rw_w0': uni((DEPTH, 2, RW_WIDTH), -6.0, 1.0),
        'rw_w2': nrm((DEPTH, 2, RW_DECAY_RANK, RW_WIDTH), 0.5 * RW_DECAY_RANK ** -0.5),
        'rw_a0': nrm((DEPTH, 2, RW_WIDTH), 0.3),
        'rw_a2': nrm((DEPTH, 2, RW_A_RANK, RW_WIDTH), 0.5 * RW_A_RANK ** -0.5),
        'rw_g2': nrm((DEPTH, 2, RW_GATE_RANK, RW_WIDTH), RW_GATE_RANK ** -0.5),
        'rw_k_k': 0.85 + nrm((DEPTH, RW_WIDTH), 0.02),
        'rw_k_a': 1.0 + nrm((DEPTH, RW_WIDTH), 0.02),
        'rw_r_k': nrm((DEPTH, RW_HEADS, HEAD_DIM), 0.1),
        'rw_gn_g': 1.0 + nrm((DEPTH, RW_WIDTH), 0.02),
        'rw_gn_b': nrm((DEPTH, RW_WIDTH), 0.02),
        'cv_dw_w': nrm((DEPTH, CV_CONV_LEN, CV_WIDTH), CV_CONV_LEN ** -0.5),
        'cv_dw_b': nrm((DEPTH, CV_WIDTH), 0.02),
        'cv_ln_g': 1.0 + nrm((DEPTH, CV_WIDTH), 0.02),
        'cv_ln_b': nrm((DEPTH, CV_WIDTH), 0.02),
        'ffn_w_gate': nrm((N_DENSE_LAYERS, D, D_FF), D ** -0.5),
        'ffn_w_up': nrm((N_DENSE_LAYERS, D, D_FF), D ** -0.5),
        'ffn_w_down': nrm((N_DENSE_LAYERS, D_FF, D), D_FF ** -0.5),
        'moe_router': nrm((N_MOE_LAYERS, D, N_EXPERTS), D ** -0.5),
        'moe_w_gate': nrm((N_MOE_LAYERS, N_EXPERTS, D, D_FF_EXPERT), D ** -0.5),
        'moe_w_up': nrm((N_MOE_LAYERS, N_EXPERTS, D, D_FF_EXPERT), D ** -0.5),
        'moe_w_down': nrm((N_MOE_LAYERS, N_EXPERTS, D_FF_EXPERT, D), D_FF_EXPERT ** -0.5),
        'final_g': 1.0 + nrm((D,), 0.02),
    }


def reference(x, c, ctx, c_ctx, norm1_g, norm2_g, mod_w, mod_b, w_in, w_out, na_rpb, rw_mu_prev, rw_mu_next,
              rw_w0, rw_w2, rw_a0, rw_a2, rw_g2, rw_k_k, rw_k_a, rw_r_k, rw_gn_g, rw_gn_b, cv_dw_w, cv_dw_b,
              cv_ln_g, cv_ln_b, ffn_w_gate, ffn_w_up, ffn_w_down, moe_router, moe_w_gate, moe_w_up, moe_w_down,
              final_g):
    c_sil = jax.nn.silu(c)
    cc_sil = jax.nn.silu(c_ctx)
    xl, xc = x, ctx
    for layer in range(DEPTH):
        need_ctx = layer < DEPTH - 1
        sh1, sc1, g1, sh2, sc2, g2 = jnp.split((c_sil @ mod_w[layer] + mod_b[layer])[:, None, :], 6, axis=-1)
        csh1, csc1, cg1, csh2, csc2, cg2 = jnp.split(cc_sil @ mod_w[layer] + mod_b[layer], 6, axis=-1)
        hl = _modulate(_rmsnorm(xl, norm1_g[layer]), sh1, sc1)
        hc = _modulate(_rmsnorm(xc, norm1_g[layer]), csh1, csc1)
        mix_l, mix_c = _mixers(hl, hc, w_in[layer], w_out[layer], na_rpb[layer], rw_mu_prev[layer],
                               rw_mu_next[layer], rw_w0[layer], rw_w2[layer], rw_a0[layer], rw_a2[layer],
                               rw_g2[layer], rw_k_k[layer], rw_k_a[layer], rw_r_k[layer], rw_gn_g[layer],
                               rw_gn_b[layer], cv_dw_w[layer], cv_dw_b[layer], cv_ln_g[layer], cv_ln_b[layer],
                               need_ctx)
        xl = xl + g1 * mix_l
        hl = _modulate(_rmsnorm(xl, norm2_g[layer]), sh2, sc2)
        j = layer // 2
        if layer % 2 == 0:
            xl = xl + g2 * _swiglu(hl, ffn_w_gate[j], ffn_w_up[j], ffn_w_down[j])
        else:
            xl = xl + g2 * _moe_swiglu(hl, moe_router[j], moe_w_gate[j], moe_w_up[j], moe_w_down[j])
        if need_ctx:
            xc = xc + cg1 * mix_c
            hc = _modulate(_rmsnorm(xc, norm2_g[layer]), csh2, csc2)
            if layer % 2 == 0:
                xc = xc + cg2 * _swiglu(hc, ffn_w_gate[j], ffn_w_up[j], ffn_w_down[j])
            else:
                xc = xc + cg2 * _moe_swiglu(hc, moe_router[j], moe_w_gate[j], moe_w_up[j], moe_w_down[j])
    return _rmsnorm(xl, final_g)
```

```python
import functools

import jax
import jax.numpy as jnp
import numpy as np
from jax import lax
from jax.experimental import pallas as pl
from jax.experimental.pallas import tpu as pltpu

F32 = jnp.float32
BF16 = jnp.bfloat16

D_MODEL = 1024
DEPTH = 2
GRID_W = 64
HEAD_DIM = 64
NA_WIDTH = 512
NA_HEADS = 8
NA_WIN_ROWS = 8
NA_WIN_COLS = 16
RW_WIDTH = 256
RW_HEADS = 4
RW_DECAY_RANK = 64
RW_A_RANK = 64
RW_GATE_RANK = 128
RW_LORA = 256
RW_IN = 1280
RW_GN_EPS = 64e-5
CV_WIDTH = 256
CV_CONV_LEN = 31
QKV_WIDTH = 3 * NA_WIDTH
IN_WIDTH = QKV_WIDTH + RW_IN + 2 * CV_WIDTH
D_FF = 2816
N_EXPERTS = 8
TOP_K = 2
D_FF_EXPERT = 3584
RMS_EPS = 1e-6
LN_EPS = 1e-5

LANES = 128
SUBLANES = 8
VMEM_LIMIT_BYTES = 56 * 1024 * 1024

MASK_VALUE = -1e30
MOE_ROWS = 512
MOE_FF_TILE = 512
ROUTER_PAD = LANES
SCAN_STEPS = 32
N_CHAINS = 64


def _params(*sem):
    return pltpu.CompilerParams(dimension_semantics=sem, vmem_limit_bytes=VMEM_LIMIT_BYTES)


def _col_chunks(width, step=512):
    out, c = [], 0
    while c < width:
        w = min(step, width - c)
        out.append((c, w))
        c += w
    return out


def _sigmoid(x):
    return 1.0 / (1.0 + jnp.exp(-x))


def _norm_mod(x, g, shift, scale):
    ms = jnp.mean(x * x, axis=-1, keepdims=True)
    h = x * lax.rsqrt(ms + RMS_EPS) * g
    return h * (1.0 + scale) + shift


def _dot_exact_rhs(x, m):
    hi = x.astype(BF16)
    r1 = x - hi.astype(F32)
    mid = r1.astype(BF16)
    lo = (r1 - mid.astype(F32)).astype(BF16)
    dot = functools.partial(jnp.dot, preferred_element_type=F32)
    return dot(hi, m) + dot(mid, m) + dot(lo, m)


def _mod_kernel(c_ref, w_ref, b_ref, o_ref):
    c = c_ref[...]
    cs = c * _sigmoid(c)
    o_ref[0] = jnp.dot(cs, w_ref[0], precision=lax.Precision.HIGHEST,
                       preferred_element_type=F32) + b_ref[0]


def _modulation(c_all, mod_w, mod_b):
    depth, d, n = mod_w.shape
    rows = c_all.shape[0]
    tn = 768
    return pl.pallas_call(
        _mod_kernel,
        grid=(depth, n // tn),
        in_specs=[
            pl.BlockSpec((rows, d), lambda l, j: (0, 0)),
            pl.BlockSpec((1, d, tn), lambda l, j: (l, 0, j)),
            pl.BlockSpec((1, 1, tn), lambda l, j: (l, 0, j)),
        ],
        out_specs=pl.BlockSpec((1, rows, tn), lambda l, j: (l, 0, j)),
        out_shape=jax.ShapeDtypeStruct((depth, rows, n), F32),
        compiler_params=_params("arbitrary", "arbitrary"),
        name="modulation",
    )(c_all, mod_w, mod_b.reshape(depth, 1, n))


def _proj_kernel(x_ref, g_ref, sh_ref, sc_ref, w_ref, qkv_ref, rw_ref, cv_ref):
    hb = _norm_mod(x_ref[0], g_ref[...], sh_ref[0], sc_ref[0]).astype(BF16)
    for ref, base in ((qkv_ref, 0), (rw_ref, QKV_WIDTH), (cv_ref, QKV_WIDTH + RW_IN)):
        for c0, w in _col_chunks(ref.shape[-1]):
            y = jnp.dot(hb, w_ref[:, base + c0:base + c0 + w], preferred_element_type=F32)
            ref[0, :, c0:c0 + w] = y.astype(ref.dtype)


def _row_tile(t, target):
    tm = min(t, target)
    assert t % tm == 0
    return tm


def _mod_spec(m):
    nb = m.shape[0]
    return pl.BlockSpec((1, 1, D_MODEL), (lambda b, i: (b, 0, 0)) if nb > 1 else (lambda b, i: (0, 0, 0)))


def _in_projection(x, g, shift, scale, w_in_bf16):
    b, t, d = x.shape
    tm = _row_tile(t, 512)
    xspec = pl.BlockSpec((1, tm, d), lambda bi, i: (bi, i, 0))
    return pl.pallas_call(
        _proj_kernel,
        grid=(b, t // tm),
        in_specs=[
            xspec,
            pl.BlockSpec((1, d), lambda bi, i: (0, 0)),
            _mod_spec(shift), _mod_spec(scale),
            pl.BlockSpec((d, IN_WIDTH), lambda bi, i: (0, 0)),
        ],
        out_specs=[
            pl.BlockSpec((1, tm, QKV_WIDTH), lambda bi, i: (bi, i, 0)),
            pl.BlockSpec((1, tm, RW_IN), lambda bi, i: (bi, i, 0)),
            pl.BlockSpec((1, tm, 2 * CV_WIDTH), lambda bi, i: (bi, i, 0)),
        ],
        out_shape=[
            jax.ShapeDtypeStruct((b, t, QKV_WIDTH), BF16),
            jax.ShapeDtypeStruct((b, t, RW_IN), F32),
            jax.ShapeDtypeStruct((b, t, 2 * CV_WIDTH), F32),
        ],
        compiler_params=_params("arbitrary", "arbitrary"),
        name="in_projection",
    )(x, g.reshape(1, d), shift, scale, w_in_bf16)


def _head_pair_queries(q):
    lane = lax.broadcasted_iota(jnp.int32, q.shape, 1)
    qs = q * jnp.asarray(HEAD_DIM ** -0.5, q.dtype)
    zero = jnp.zeros_like(qs)
    return jnp.concatenate([jnp.where(lane < HEAD_DIM, qs, zero), jnp.where(lane >= HEAD_DIM, qs, zero)], axis=0)


def _head_pair_merge(o, rows):
    lane = lax.broadcasted_iota(jnp.int32, (rows, LANES), 1)
    return jnp.where(lane < HEAD_DIM, o[:rows], o[rows:])


_NT = (((1,), (1,)), ((), ()))


def _na_kernel(q_ref, k_ref, v_ref, kc_ref, vc_ref, bias_ref, o_ref, *, rows):
    i = pl.program_id(2)
    rs = jnp.clip(i - NA_WIN_ROWS // 2, 0, rows - NA_WIN_ROWS)
    start = pl.multiple_of(rs * GRID_W, GRID_W)
    n_win = NA_WIN_ROWS * GRID_W
    k_win = k_ref[0, pl.ds(start, n_win), :]
    v_win = v_ref[0, pl.ds(start, n_win), :]
    qb = _head_pair_queries(q_ref[0])
    s_win = lax.dot_general(qb, k_win, _NT, preferred_element_type=F32) + bias_ref[0, 0]
    s_ctx = lax.dot_general(qb, kc_ref[0], _NT, preferred_element_type=F32)
    m = jnp.maximum(jnp.max(s_win, axis=-1, keepdims=True), jnp.max(s_ctx, axis=-1, keepdims=True))
    p_win = jnp.exp(s_win - m)
    p_ctx = jnp.exp(s_ctx - m)
    denom = jnp.sum(p_win, axis=-1, keepdims=True) + jnp.sum(p_ctx, axis=-1, keepdims=True)
    o = (jnp.dot(p_win.astype(BF16), v_win, preferred_element_type=F32)
         + jnp.dot(p_ctx.astype(BF16), vc_ref[0], preferred_element_type=F32)) / denom
    o_ref[0] = _head_pair_merge(o, GRID_W).astype(o_ref.dtype)


def _na_bias_table(rpb):
    qc = np.arange(GRID_W)[:, None]
    kc = np.arange(GRID_W)[None, :]
    ws = np.clip(qc - NA_WIN_COLS // 2, 0, GRID_W - NA_WIN_COLS)
    mask = (kc >= ws) & (kc < ws + NA_WIN_COLS)
    rel = np.clip(kc - qc + NA_WIN_COLS - 1, 0, 2 * NA_WIN_COLS - 2)
    full = jnp.where(mask[None, None], rpb[:, :, rel].astype(F32), MASK_VALUE)
    dr = np.arange(NA_WIN_ROWS)[:, None] + np.arange(NA_WIN_ROWS)[None, :]
    t = full[:, dr]
    t = t.transpose(0, 1, 3, 2, 4).reshape(NA_HEADS // 2, 2, NA_WIN_ROWS, GRID_W, NA_WIN_ROWS * GRID_W)
    return t.transpose(0, 2, 1, 3, 4).reshape(NA_HEADS // 2, NA_WIN_ROWS, 2 * GRID_W, NA_WIN_ROWS * GRID_W)


def _neighbourhood_attention(qkv_l, qkv_c, bias_table):
    b, n, _ = qkv_l.shape
    l = qkv_c.shape[1]
    rows = n // GRID_W
    hp = NA_HEADS // 2
    koff, voff = NA_WIDTH // LANES, 2 * NA_WIDTH // LANES

    def bias_index(bi, h, i):
        rs = jnp.clip(i - NA_WIN_ROWS // 2, 0, rows - NA_WIN_ROWS)
        return (h, rs - i + NA_WIN_ROWS - 1, 0, 0)

    return pl.pallas_call(
        functools.partial(_na_kernel, rows=rows),
        grid=(b, hp, rows),
        in_specs=[
            pl.BlockSpec((1, GRID_W, LANES), lambda bi, h, i: (bi, i, h)),
            pl.BlockSpec((1, n, LANES), lambda bi, h, i: (bi, 0, koff + h)),
            pl.BlockSpec((1, n, LANES), lambda bi, h, i: (bi, 0, voff + h)),
            pl.BlockSpec((1, l, LANES), lambda bi, h, i: (bi, 0, koff + h)),
            pl.BlockSpec((1, l, LANES), lambda bi, h, i: (bi, 0, voff + h)),
            pl.BlockSpec((1, 1, 2 * GRID_W, NA_WIN_ROWS * GRID_W), bias_index),
        ],
        out_specs=pl.BlockSpec((1, GRID_W, LANES), lambda bi, h, i: (bi, i, h)),
        out_shape=jax.ShapeDtypeStruct((b, n, NA_WIDTH), BF16),
        compiler_params=_params("arbitrary", "arbitrary", "arbitrary"),
        name="neighbourhood_attention",
    )(qkv_l, qkv_l, qkv_l, qkv_c, qkv_c, bias_table)


def _ctx_attn_kernel(q_ref, k_ref, v_ref, o_ref):
    l = q_ref.shape[1]
    qb = _head_pair_queries(q_ref[0])
    s = lax.dot_general(qb, k_ref[0], _NT, preferred_element_type=F32)
    p = jnp.exp(s - jnp.max(s, axis=-1, keepdims=True))
    o = jnp.dot(p.astype(BF16), v_ref[0], preferred_element_type=F32) / jnp.sum(p, axis=-1, keepdims=True)
    o_ref[0] = _head_pair_merge(o, l).astype(o_ref.dtype)


def _context_attention(qkv_c):
    b, l, _ = qkv_c.shape
    hp = NA_HEADS // 2
    koff, voff = NA_WIDTH // LANES, 2 * NA_WIDTH // LANES
    return pl.pallas_call(
        _ctx_attn_kernel,
        grid=(b, hp),
        in_specs=[
            pl.BlockSpec((1, l, LANES), lambda bi, h: (bi, 0, h)),
            pl.BlockSpec((1, l, LANES), lambda bi, h: (bi, 0, koff + h)),
            pl.BlockSpec((1, l, LANES), lambda bi, h: (bi, 0, voff + h)),
        ],
        out_specs=pl.BlockSpec((1, l, LANES), lambda bi, h: (bi, 0, h)),
        out_shape=jax.ShapeDtypeStruct((b, l, NA_WIDTH), BF16),
        compiler_params=_params("arbitrary", "arbitrary"),
        name="context_attention",
    )(qkv_c, qkv_c, qkv_c)


SCAN_COLS = 9 * RW_WIDTH
POST_COLS = 4 * RW_WIDTH


def _softplus(x):
    return jnp.maximum(x, 0.0) + jnp.log(1.0 + jnp.exp(-jnp.abs(x)))


def _rw_prep_kernel(u_ref, up_ref, un_ref, mup_ref, mun_ref, ones_ref, kk_ref, ka_ref, rk_ref,
                    w0_ref, a0_ref, wa_ref, g2_ref, scan_ref, post_ref):
    i = pl.program_id(1)
    tt = u_ref.shape[1]
    u = u_ref[0]
    row = lax.broadcasted_iota(jnp.int32, u.shape, 0)
    prev_row = jnp.where(i > 0, up_ref[0, SUBLANES - 1:SUBLANES, :], 0.0)
    next_row = jnp.where(i < pl.num_programs(1) - 1, un_ref[0, 0:1, :], 0.0)
    prev = jnp.where(row == 0, prev_row, pltpu.roll(u, 1, 0))
    nxt = jnp.where(row == tt - 1, next_row, pltpu.roll(u, tt - 1, 0))
    us = u + mup_ref[...] * (prev - u) + mun_ref[...] * (nxt - u)

    ones = ones_ref[...]
    r = us[:, 0:RW_WIDTH]
    k = us[:, RW_WIDTH:2 * RW_WIDTH]
    v = us[:, 2 * RW_WIDTH:3 * RW_WIDTH]
    kk = k * kk_ref[...]
    kk = kk * lax.rsqrt(_dot_exact_rhs(kk * kk, ones) + 1e-12)
    scan_ref[0, :, 0:RW_WIDTH] = r
    scan_ref[0, :, RW_WIDTH:2 * RW_WIDTH] = v
    scan_ref[0, :, 2 * RW_WIDTH:3 * RW_WIDTH] = -kk
    for d in range(2):
        lo = 3 * RW_WIDTH + d * RW_LORA
        wa_in = us[:, lo:lo + LANES]
        lane = lax.broadcasted_iota(jnp.int32, wa_in.shape, 1)
        wa_in = jnp.where(lane < RW_DECAY_RANK, jnp.tanh(wa_in), wa_in)
        wa = jnp.dot(wa_in.astype(BF16), wa_ref[d], preferred_element_type=F32)
        log_w = -_softplus(-(w0_ref[d] + wa[:, 0:RW_WIDTH])) - 0.5
        decay = jnp.exp(-jnp.exp(log_w))
        a = _sigmoid(a0_ref[d] + wa[:, RW_WIDTH:2 * RW_WIDTH])
        gate_in = _sigmoid(us[:, lo + LANES:lo + 2 * LANES])
        g = jnp.dot(gate_in.astype(BF16), g2_ref[d], preferred_element_type=F32)
        kd = k * (1.0 + (a - 1.0) * ka_ref[...])
        bonus = _dot_exact_rhs(r * kd * rk_ref[...], ones) * v
        base = (3 + 3 * d) * RW_WIDTH
        scan_ref[0, :, base:base + RW_WIDTH] = decay
        scan_ref[0, :, base + RW_WIDTH:base + 2 * RW_WIDTH] = kd
        scan_ref[0, :, base + 2 * RW_WIDTH:base + 3 * RW_WIDTH] = kk * a
        post_ref[0, :, 2 * d * RW_WIDTH:(2 * d + 1) * RW_WIDTH] = g
        post_ref[0, :, (2 * d + 1) * RW_WIDTH:(2 * d + 2) * RW_WIDTH] = bonus


def _head_ones():
    h = np.arange(RW_WIDTH) // HEAD_DIM
    return jnp.asarray(h[:, None] == h[None, :], BF16)


def _rw_prep(u, p):
    b, t, _ = u.shape
    tt = _row_tile(t, 512)
    nt8 = t // SUBLANES

    def row(vec, n):
        return vec.reshape(1, n)

    full2 = lambda shape: pl.BlockSpec(shape, lambda bi, i: (0,) * len(shape))
    return pl.pallas_call(
        _rw_prep_kernel,
        grid=(b, t // tt),
        in_specs=[
            pl.BlockSpec((1, tt, RW_IN), lambda bi, i: (bi, i, 0)),
            pl.BlockSpec((1, SUBLANES, RW_IN), lambda bi, i: (bi, jnp.maximum(i * (tt // SUBLANES) - 1, 0), 0)),
            pl.BlockSpec((1, SUBLANES, RW_IN), lambda bi, i: (bi, jnp.minimum((i + 1) * (tt // SUBLANES), nt8 - 1), 0)),
            full2((1, RW_IN)), full2((1, RW_IN)),
            full2((RW_WIDTH, RW_WIDTH)),
            full2((1, RW_WIDTH)), full2((1, RW_WIDTH)), full2((1, RW_WIDTH)),
            full2((2, 1, RW_WIDTH)), full2((2, 1, RW_WIDTH)),
            full2((2, LANES, 2 * RW_WIDTH)), full2((2, RW_GATE_RANK, RW_WIDTH)),
        ],
        out_specs=[
            pl.BlockSpec((1, tt, SCAN_COLS), lambda bi, i: (bi, i, 0)),
            pl.BlockSpec((1, tt, POST_COLS), lambda bi, i: (bi, i, 0)),
        ],
        out_shape=[
            jax.ShapeDtypeStruct((b, t, SCAN_COLS), F32),
            jax.ShapeDtypeStruct((b, t, POST_COLS), F32),
        ],
        compiler_params=_params("arbitrary", "arbitrary"),
        name="rwkv_prep",
    )(u, u, u, row(p["mu_prev"], RW_IN), row(p["mu_next"], RW_IN), _head_ones(),
      row(p["k_k"], RW_WIDTH), row(p["k_a"], RW_WIDTH), row(p["r_k"], RW_WIDTH),
      p["w0"].reshape(2, 1, RW_WIDTH), p["a0"].reshape(2, 1, RW_WIDTH), p["wa"], p["g2"])


def _rw_lora_weights(w2, a2):
    z = jnp.zeros_like(w2)
    top = jnp.concatenate([w2, z], axis=-1)
    bot = jnp.concatenate([z, a2], axis=-1)
    return jnp.concatenate([top, bot], axis=1).astype(BF16)


def _scan_kernel(w_ref, kd_ref, b_ref, r_ref, zn_ref, v_ref, y_ref, s_ref, sz_ref):
    @pl.when(pl.program_id(0) == 0)
    def _():
        s_ref[...] = jnp.zeros_like(s_ref)
        sz_ref[...] = jnp.zeros_like(sz_ref)

    def step(t, sz):
        v = v_ref[t]
        ya = [jnp.zeros_like(v), jnp.zeros_like(v)]
        za = [jnp.zeros_like(v), jnp.zeros_like(v)]
        for k in range(HEAD_DIM):
            row = pl.ds(k, 1)
            s = s_ref[k] * w_ref[t, row, :] + sz * b_ref[t, row, :] + v * kd_ref[t, row, :]
            s_ref[k] = s
            ya[k % 2] = ya[k % 2] + s * r_ref[t, row, :]
            za[k % 2] = za[k % 2] + s * zn_ref[t, row, :]
        y_ref[t] = ya[0] + ya[1]
        return za[0] + za[1]

    sz_ref[...] = lax.fori_loop(0, w_ref.shape[0], step, sz_ref[...])


def _rwkv7_scan(w, kd, bv, r, zn, v):
    t = w.shape[0]
    tb = _row_tile(t, SCAN_STEPS)
    kspec = pl.BlockSpec((tb, HEAD_DIM, LANES), lambda i: (i, 0, 0))
    vspec = pl.BlockSpec((tb, HEAD_DIM // 2, LANES), lambda i: (i, 0, 0))
    return pl.pallas_call(
        _scan_kernel,
        grid=(t // tb,),
        in_specs=[kspec] * 5 + [vspec],
        out_specs=vspec,
        out_shape=jax.ShapeDtypeStruct((t, HEAD_DIM // 2, LANES), F32),
        scratch_shapes=[pltpu.VMEM((HEAD_DIM, HEAD_DIM // 2, LANES), F32),
                        pltpu.VMEM((HEAD_DIM // 2, LANES), F32)],
        compiler_params=_params("arbitrary"),
        name="rwkv_scan",
    )(w, kd, bv, r, zn, v)


def _to_chain_major(x):
    b, t = x.shape[:2]
    x = x.reshape(b, t, 2, RW_HEADS, HEAD_DIM).transpose(1, 4, 0, 2, 3).reshape(t, HEAD_DIM, b * 2 * RW_HEADS)
    return jnp.concatenate([x, x], axis=-1)


def _values_to_chain_major(x):
    b, t = x.shape[:2]
    x = x.reshape(b, t, 2, RW_HEADS, 2, HEAD_DIM // 2).transpose(1, 5, 4, 0, 2, 3)
    return x.reshape(t, HEAD_DIM // 2, 2 * b * 2 * RW_HEADS)


def _values_from_chain_major(y, b):
    t = y.shape[0]
    y = y.reshape(t, HEAD_DIM // 2, 2, b, 2, RW_HEADS).transpose(3, 0, 4, 5, 2, 1)
    return y.reshape(b, t, 2, RW_WIDTH)


def _rw_post_kernel(yf_ref, yb_ref, post_ref, ones_ref, gg_ref, gb_ref, o_ref):
    ones = ones_ref[...]
    out = None
    for d, y_ref in enumerate((yf_ref, yb_ref)):
        y = y_ref[0]
        mu = _dot_exact_rhs(y, ones) * (1.0 / HEAD_DIM)
        yc = y - mu
        var = _dot_exact_rhs(yc * yc, ones) * (1.0 / HEAD_DIM)
        yn = yc * lax.rsqrt(var + RW_GN_EPS) * gg_ref[...] + gb_ref[...]
        g = post_ref[0, :, 2 * d * RW_WIDTH:(2 * d + 1) * RW_WIDTH]
        bonus = post_ref[0, :, (2 * d + 1) * RW_WIDTH:(2 * d + 2) * RW_WIDTH]
        term = (yn + bonus) * g
        out = term if out is None else out + term
    o_ref[0] = out.astype(o_ref.dtype)


def _rw_post(y_f, y_b, post, gn_g, gn_b):
    b, t, _ = y_f.shape
    tt = _row_tile(t, 512)
    spec = lambda c: pl.BlockSpec((1, tt, c), lambda bi, i: (bi, i, 0))
    full2 = lambda shape: pl.BlockSpec(shape, lambda bi, i: (0,) * len(shape))
    return pl.pallas_call(
        _rw_post_kernel,
        grid=(b, t // tt),
        in_specs=[spec(RW_WIDTH), spec(RW_WIDTH), spec(POST_COLS), full2((RW_WIDTH, RW_WIDTH)),
                  full2((1, RW_WIDTH)), full2((1, RW_WIDTH))],
        out_specs=spec(RW_WIDTH),
        out_shape=jax.ShapeDtypeStruct((b, t, RW_WIDTH), BF16),
        compiler_params=_params("arbitrary", "arbitrary"),
        name="rwkv_post",
    )(y_f, y_b, post, _head_ones(), gn_g.reshape(1, RW_WIDTH), gn_b.reshape(1, RW_WIDTH))


def _bi_rwkv7(u_l, u_c, p, need_ctx_out):
    b, n, _ = u_l.shape
    l = u_c.shape[1]
    scan_l, post_l = _rw_prep(u_l, p)
    scan_c, post_c = _rw_prep(u_c, p)
    fw = jnp.concatenate([scan_c, scan_l], axis=1).reshape(b, l + n, 9, RW_WIDTH)
    bw = jnp.concatenate([scan_c[:, ::-1], scan_l[:, ::-1]], axis=1).reshape(b, l + n, 9, RW_WIDTH)

    def both(fw_col, bw_col):
        return jnp.stack([fw[:, :, fw_col], bw[:, :, bw_col]], axis=2)

    r = _to_chain_major(both(0, 0))
    v = _values_to_chain_major(both(1, 1))
    z = _to_chain_major(both(2, 2))
    w = _to_chain_major(both(3, 6))
    kd = _to_chain_major(both(4, 7))
    bv = _to_chain_major(both(5, 8))
    zn = jnp.concatenate([z[1:], jnp.zeros_like(z[:1])], axis=0)
    y = _values_from_chain_major(_rwkv7_scan(w, kd, bv, r, zn, v), b)
    y_f, y_b = y[:, :, 0], y[:, :, 1]
    out_l = _rw_post(y_f[:, l:], y_b[:, l:][:, ::-1], post_l, p["gn_g"], p["gn_b"])
    out_c = None
    if need_ctx_out:
        out_c = _rw_post(y_f[:, :l], y_b[:, :l][:, ::-1], post_c, p["gn_g"], p["gn_b"])
    return out_l, out_c


CONV_PAD = 16


def _conv_kernel(u_ref, w_ref, b_ref, lg_ref, lb_ref, o_ref, pad_ref, *, chunk):
    t = u_ref.shape[1]
    half = CV_CONV_LEN // 2
    u = u_ref[0]
    zeros = jnp.zeros((CONV_PAD, CV_WIDTH), F32)
    pad_ref[0:CONV_PAD, :] = zeros
    pad_ref[CONV_PAD + t:2 * CONV_PAD + t, :] = zeros
    pad_ref[CONV_PAD:CONV_PAD + t, :] = u[:, :CV_WIDTH] * _sigmoid(u[:, CV_WIDTH:])
    w = w_ref[...]

    def body(c, carry):
        base = pl.multiple_of(c * chunk, chunk)
        win = pad_ref[pl.ds(base, chunk + 2 * CONV_PAD), :]
        acc = jnp.zeros((chunk, CV_WIDTH), F32)
        for j in range(CV_CONV_LEN):
            off = CONV_PAD - half + j
            acc = acc + win[off:off + chunk, :] * w[j:j + 1, :]
        h = acc + b_ref[...]
        mu = jnp.mean(h, axis=-1, keepdims=True)
        hc = h - mu
        var = jnp.mean(hc * hc, axis=-1, keepdims=True)
        y = hc * lax.rsqrt(var + LN_EPS) * lg_ref[...] + lb_ref[...]
        o_ref[0, pl.ds(base, chunk), :] = (y * _sigmoid(y)).astype(o_ref.dtype)
        return carry

    lax.fori_loop(0, t // chunk, body, 0)


def _conformer_conv(u, dw_w, dw_b, ln_g, ln_b):
    b, t, _ = u.shape
    chunk = min(t, 256)
    row = lambda vec: vec.reshape(1, CV_WIDTH)
    full = lambda shape: pl.BlockSpec(shape, lambda bi: (0,) * len(shape))
    return pl.pallas_call(
        functools.partial(_conv_kernel, chunk=chunk),
        grid=(b,),
        in_specs=[pl.BlockSpec((1, t, 2 * CV_WIDTH), lambda bi: (bi, 0, 0)),
                  full((CV_CONV_LEN, CV_WIDTH)), full((1, CV_WIDTH)), full((1, CV_WIDTH)), full((1, CV_WIDTH))],
        out_specs=pl.BlockSpec((1, t, CV_WIDTH), lambda bi: (bi, 0, 0)),
        out_shape=jax.ShapeDtypeStruct((b, t, CV_WIDTH), BF16),
        scratch_shapes=[pltpu.VMEM((t + 2 * CONV_PAD, CV_WIDTH), F32)],
        compiler_params=_params("arbitrary"),
        name="conformer_conv",
    )(u, dw_w, row(dw_b), row(ln_g), row(ln_b))


def _wout_kernel(x_ref, na_ref, rw_ref, cv_ref, w_ref, gate_ref, o_ref):
    dot = functools.partial(jnp.dot, preferred_element_type=F32)
    acc = (dot(na_ref[0], w_ref[0:NA_WIDTH, :])
           + dot(rw_ref[0], w_ref[NA_WIDTH:NA_WIDTH + RW_WIDTH, :])
           + dot(cv_ref[0], w_ref[NA_WIDTH + RW_WIDTH:, :]))
    o_ref[0] = x_ref[0] + gate_ref[0] * acc


def _out_projection(x, na, rw, cv, w_out_bf16, gate):
    b, t, d = x.shape
    tm = _row_tile(t, 512)
    spec = lambda c: pl.BlockSpec((1, tm, c), lambda bi, i: (bi, i, 0))
    return pl.pallas_call(
        _wout_kernel,
        grid=(b, t // tm),
        in_specs=[spec(d), spec(NA_WIDTH), spec(RW_WIDTH), spec(CV_WIDTH),
                  pl.BlockSpec((d, d), lambda bi, i: (0, 0)), _mod_spec(gate)],
        out_specs=spec(d),
        out_shape=jax.ShapeDtypeStruct((b, t, d), F32),
        compiler_params=_params("arbitrary", "arbitrary"),
        name="out_projection",
    )(x, na, rw, cv, w_out_bf16, gate)


def _ffn_kernel(x_ref, g_ref, sh_ref, sc_ref, gate_ref, wg_ref, wu_ref, wd_ref, o_ref, acc_ref):
    x = x_ref[0]
    hb = _norm_mod(x, g_ref[...], sh_ref[0], sc_ref[0]).astype(BF16)
    dot = functools.partial(jnp.dot, preferred_element_type=F32)
    for n, (c0, w) in enumerate(_col_chunks(wg_ref.shape[1])):
        g = dot(hb, wg_ref[:, c0:c0 + w])
        u = dot(hb, wu_ref[:, c0:c0 + w])
        a = (g * _sigmoid(g) * u).astype(BF16)
        y = dot(a, wd_ref[c0:c0 + w, :])
        if n == 0:
            acc_ref[...] = y
        else:
            acc_ref[...] += y
    o_ref[0] = x + gate_ref[0] * acc_ref[...]


def _dense_ffn(x, g, shift, scale, gate, wg, wu, wd):
    b, t, d = x.shape
    tm = _row_tile(t, 512)
    dff = wg.shape[1]
    xspec = pl.BlockSpec((1, tm, d), lambda bi, i: (bi, i, 0))
    return pl.pallas_call(
        _ffn_kernel,
        grid=(b, t // tm),
        in_specs=[xspec, pl.BlockSpec((1, d), lambda bi, i: (0, 0)),
                  _mod_spec(shift), _mod_spec(scale), _mod_spec(gate),
                  pl.BlockSpec((d, dff), lambda bi, i: (0, 0)),
                  pl.BlockSpec((d, dff), lambda bi, i: (0, 0)),
                  pl.BlockSpec((dff, d), lambda bi, i: (0, 0))],
        out_specs=xspec,
        out_shape=jax.ShapeDtypeStruct((b, t, d), F32),
        scratch_shapes=[pltpu.VMEM((tm, d), F32)],
        compiler_params=_params("arbitrary", "arbitrary"),
        name="dense_ffn",
    )(x, g.reshape(1, d), shift, scale, gate, wg, wu, wd)


def _router_kernel(x_ref, g_ref, sh_ref, sc_ref, wr_ref, h_ref, logit_ref):
    h = _norm_mod(x_ref[0], g_ref[...], sh_ref[0], sc_ref[0])
    h_ref[0] = h
    logit_ref[0] = jnp.dot(h, wr_ref[...], precision=lax.Precision.HIGHEST, preferred_element_type=F32)


def _moe_router(x, g, shift, scale, router_pad):
    b, t, d = x.shape
    tm = _row_tile(t, 512)
    xspec = pl.BlockSpec((1, tm, d), lambda bi, i: (bi, i, 0))
    return pl.pallas_call(
        _router_kernel,
        grid=(b, t // tm),
        in_specs=[xspec, pl.BlockSpec((1, d), lambda bi, i: (0, 0)), _mod_spec(shift), _mod_spec(scale),
                  pl.BlockSpec((d, ROUTER_PAD), lambda bi, i: (0, 0))],
        out_specs=[xspec, pl.BlockSpec((1, tm, ROUTER_PAD), lambda bi, i: (bi, i, 0))],
        out_shape=[jax.ShapeDtypeStruct((b, t, d), F32), jax.ShapeDtypeStruct((b, t, ROUTER_PAD), F32)],
        compiler_params=_params("arbitrary", "arbitrary"),
        name="moe_router",
    )(x, g.reshape(1, d), shift, scale, router_pad)


def _row_copy(src_hbm, dst_vmem, sem, src_row, dst_row):
    return pltpu.make_async_copy(src_hbm.at[pl.ds(src_row, 1)], dst_vmem.at[pl.ds(dst_row, 1)], sem)


def _expert_kernel(be_ref, nused_ref, tok_ref, h_hbm, wg_ref, wu_ref, wd_ref, o_ref, xg_ref, xb_ref, acc_ref, sem):
    i = pl.program_id(0)
    j = pl.program_id(1)
    rows = xg_ref.shape[0]
    used = i < nused_ref[0]

    @pl.when(jnp.logical_and(used, j == 0))
    def _():
        def start(r, c):
            _row_copy(h_hbm, xg_ref, sem, tok_ref[0, 0, r], r).start()
            return c

        def wait(r, c):
            _row_copy(h_hbm, xg_ref, sem, 0, r).wait()
            return c

        lax.fori_loop(0, rows, start, 0)
        lax.fori_loop(0, rows, wait, 0)
        xb_ref[...] = xg_ref[...].astype(BF16)

    @pl.when(used)
    def _():
        dot = functools.partial(jnp.dot, preferred_element_type=F32)
        xb = xb_ref[...]
        g = dot(xb, wg_ref[0])
        u = dot(xb, wu_ref[0])
        a = (g * _sigmoid(g) * u).astype(BF16)
        y = dot(a, wd_ref[0])

        @pl.when(j == 0)
        def _():
            acc_ref[...] = y

        @pl.when(j > 0)
        def _():
            acc_ref[...] += y

    last = j == pl.num_programs(1) - 1

    @pl.when(jnp.logical_and(used, last))
    def _():
        o_ref[...] = acc_ref[...]

    @pl.when(jnp.logical_and(jnp.logical_not(used), last))
    def _():
        o_ref[...] = jnp.zeros_like(o_ref)


def _moe_experts(h_flat, slot_tok, block_e, n_used, wg, wu, wd):
    nb = slot_tok.shape[0]
    d = h_flat.shape[1]
    dff = wg.shape[2]
    nj = dff // MOE_FF_TILE
    grid_spec = pltpu.PrefetchScalarGridSpec(
        num_scalar_prefetch=2,
        grid=(nb, nj),
        in_specs=[
            pl.BlockSpec((1, 1, MOE_ROWS), lambda i, j, be, nu: (i, 0, 0), memory_space=pltpu.SMEM),
            pl.BlockSpec(memory_space=pl.ANY),
            pl.BlockSpec((1, d, MOE_FF_TILE), lambda i, j, be, nu: (be[i], 0, j)),
            pl.BlockSpec((1, d, MOE_FF_TILE), lambda i, j, be, nu: (be[i], 0, j)),
            pl.BlockSpec((1, MOE_FF_TILE, d), lambda i, j, be, nu: (be[i], j, 0)),
        ],
        out_specs=pl.BlockSpec((MOE_ROWS, d), lambda i, j, be, nu: (i, 0)),
        scratch_shapes=[pltpu.VMEM((MOE_ROWS, d), F32), pltpu.VMEM((MOE_ROWS, d), BF16),
                        pltpu.VMEM((MOE_ROWS, d), F32), pltpu.SemaphoreType.DMA(())],
    )
    return pl.pallas_call(
        _expert_kernel,
        grid_spec=grid_spec,
        out_shape=jax.ShapeDtypeStruct((nb * MOE_ROWS, d), F32),
        compiler_params=_params("arbitrary", "arbitrary"),
        name="moe_experts",
    )(block_e, n_used, slot_tok, h_flat, wg, wu, wd)


def _combine_kernel(dest_ref, x_ref, gates_ref, gate2_ref, fg_ref, yb_hbm, o_ref, y0_ref, y1_ref, sem):
    rows = y0_ref.shape[0]

    def start(r, c):
        _row_copy(yb_hbm, y0_ref, sem, dest_ref[0, 0, 2 * r], r).start()
        _row_copy(yb_hbm, y1_ref, sem, dest_ref[0, 0, 2 * r + 1], r).start()
        return c

    def wait(r, c):
        _row_copy(yb_hbm, y0_ref, sem, 0, r).wait()
        _row_copy(yb_hbm, y1_ref, sem, 0, r).wait()
        return c

    lax.fori_loop(0, rows, start, 0)
    lax.fori_loop(0, rows, wait, 0)
    gates = gates_ref[0]
    y = y0_ref[...] * gates[:, 0:1] + y1_ref[...] * gates[:, 1:2]
    x = x_ref[0] + gate2_ref[0] * y
    ms = jnp.mean(x * x, axis=-1, keepdims=True)
    o_ref[0] = x * lax.rsqrt(ms + RMS_EPS) * fg_ref[...]


def _moe_combine_final(x, gates, dest, yb, gate2, final_g):
    b, t, d = x.shape
    tm = dest.shape[-1] // TOP_K
    xspec = pl.BlockSpec((1, tm, d), lambda bi, i: (bi, i, 0))
    return pl.pallas_call(
        _combine_kernel,
        grid=(b, t // tm),
        in_specs=[
            pl.BlockSpec((1, 1, TOP_K * tm), lambda bi, i: (bi * (t // tm) + i, 0, 0), memory_space=pltpu.SMEM),
            xspec,
            pl.BlockSpec((1, tm, TOP_K), lambda bi, i: (bi, i, 0)),
            _mod_spec(gate2),
            pl.BlockSpec((1, d), lambda bi, i: (0, 0)),
            pl.BlockSpec(memory_space=pl.ANY),
        ],
        out_specs=xspec,
        out_shape=jax.ShapeDtypeStruct((b, t, d), F32),
        scratch_shapes=[pltpu.VMEM((tm, d), F32), pltpu.VMEM((tm, d), F32), pltpu.SemaphoreType.DMA(())],
        compiler_params=_params("arbitrary", "arbitrary"),
        name="moe_combine_final",
    )(dest.reshape(-1, 1, TOP_K * tm), x, gates, gate2, final_g.reshape(1, d), yb)


def _moe_layer_final(x, g, shift, scale, gate2, router, wg, wu, wd, final_g):
    b, t, d = x.shape
    n_tok = b * t
    n_asg = n_tok * TOP_K
    router_pad = jnp.pad(router, ((0, 0), (0, ROUTER_PAD - N_EXPERTS)))
    h, logits = _moe_router(x, g, shift, scale, router_pad)
    logits = logits.reshape(n_tok, ROUTER_PAD)[:, :N_EXPERTS]
    top_logit, top_e = lax.top_k(logits, TOP_K)
    gates = jax.nn.softmax(top_logit, axis=-1)
    flat_e = top_e.reshape(-1)
    onehot = (flat_e[:, None] == jnp.arange(N_EXPERTS, dtype=flat_e.dtype)[None, :]).astype(jnp.int32)
    rank = jnp.cumsum(onehot, axis=0) - onehot
    counts = jnp.sum(onehot, axis=0)
    padded = (counts + MOE_ROWS - 1) // MOE_ROWS * MOE_ROWS
    pad_end = jnp.cumsum(padded)
    pad_start = pad_end - padded
    dest = (pad_start[flat_e] + jnp.sum(rank * onehot, axis=1)).astype(jnp.int32)
    n_blocks = (n_asg + MOE_ROWS - 1) // MOE_ROWS + N_EXPERTS
    flat_tok = jnp.arange(n_asg, dtype=jnp.int32) // TOP_K
    slot_tok = jnp.zeros((n_blocks * MOE_ROWS,), jnp.int32).at[dest].set(flat_tok)
    block_e = jnp.minimum(jnp.searchsorted(pad_end, jnp.arange(n_blocks) * MOE_ROWS, side="right"),
                          N_EXPERTS - 1).astype(jnp.int32)
    n_used = (pad_end[-1:] // MOE_ROWS).astype(jnp.int32)
    yb = _moe_experts(h.reshape(n_tok, d), slot_tok.reshape(n_blocks, 1, MOE_ROWS), block_e, n_used, wg, wu, wd)
    tm = _row_tile(t, 256)
    return _moe_combine_final(x, gates.reshape(b, t, TOP_K), dest.reshape(b, t // tm, 1, TOP_K * tm), yb,
                              gate2, final_g)


def _final_norm_kernel(x_ref, g_ref, o_ref):
    x = x_ref[0]
    ms = jnp.mean(x * x, axis=-1, keepdims=True)
    o_ref[0] = x * lax.rsqrt(ms + RMS_EPS) * g_ref[...]


def _final_norm(x, g):
    b, t, d = x.shape
    tm = _row_tile(t, 512)
    xspec = pl.BlockSpec((1, tm, d), lambda bi, i: (bi, i, 0))
    return pl.pallas_call(
        _final_norm_kernel,
        grid=(b, t // tm),
        in_specs=[xspec, pl.BlockSpec((1, d), lambda bi, i: (0, 0))],
        out_specs=xspec,
        out_shape=jax.ShapeDtypeStruct((b, t, d), F32),
        compiler_params=_params("arbitrary", "arbitrary"),
        name="final_norm",
    )(x, g.reshape(1, d))


def kernel(x, c, ctx, c_ctx, norm1_g, norm2_g, mod_w, mod_b, w_in, w_out, na_rpb, rw_mu_prev, rw_mu_next, rw_w0, rw_w2, rw_a0, rw_a2, rw_g2, rw_k_k, rw_k_a, rw_r_k, rw_gn_g, rw_gn_b, cv_dw_w, cv_dw_b, cv_ln_g, cv_ln_b, ffn_w_gate, ffn_w_up, ffn_w_down, moe_router, moe_w_gate, moe_w_up, moe_w_down, final_g):
    b = x.shape[0]
    depth = mod_w.shape[0]
    c_rows = 2 * SUBLANES
    c_all = jnp.concatenate([c, c_ctx[None, :], jnp.zeros((c_rows - b - 1, D_MODEL), F32)], axis=0)
    mod = _modulation(c_all, mod_w, mod_b)

    xl, xc = x, ctx
    out = None
    for layer in range(depth):
        need_ctx = layer < depth - 1
        m = mod[layer].reshape(c_rows, 6, 1, D_MODEL)
        sh1, sc1, g1, sh2, sc2, g2 = (m[:b, k] for k in range(6))
        csh1, csc1, cg1, csh2, csc2, cg2 = (m[b:b + 1, k] for k in range(6))
        w_in_b = w_in[layer].astype(BF16)
        w_out_b = w_out[layer].astype(BF16)
        qkv_l, rw_l, cv_l = _in_projection(xl, norm1_g[layer], sh1, sc1, w_in_b)
        qkv_c, rw_c, cv_c = _in_projection(xc, norm1_g[layer], csh1, csc1, w_in_b)
        na_l = _neighbourhood_attention(qkv_l, qkv_c, _na_bias_table(na_rpb[layer]))
        rw_p = dict(mu_prev=rw_mu_prev[layer], mu_next=rw_mu_next[layer], w0=rw_w0[layer], a0=rw_a0[layer],
                    wa=_rw_lora_weights(rw_w2[layer], rw_a2[layer]), g2=rw_g2[layer].astype(BF16),
                    k_k=rw_k_k[layer], k_a=rw_k_a[layer], r_k=rw_r_k[layer].reshape(-1),
                    gn_g=rw_gn_g[layer], gn_b=rw_gn_b[layer])
        rwo_l, rwo_c = _bi_rwkv7(rw_l, rw_c, rw_p, need_ctx)
        cv_args = (cv_dw_w[layer], cv_dw_b[layer], cv_ln_g[layer], cv_ln_b[layer])
        cvo_l = _conformer_conv(cv_l, *cv_args)
        xl = _out_projection(xl, na_l, rwo_l, cvo_l, w_out_b, g1)
        if need_ctx:
            na_c = _context_attention(qkv_c)
            cvo_c = _conformer_conv(cv_c, *cv_args)
            xc = _out_projection(xc, na_c, rwo_c, cvo_c, w_out_b, cg1)
        j = layer // 2
        if layer % 2 == 0:
            ffn_w = (ffn_w_gate[j].astype(BF16), ffn_w_up[j].astype(BF16), ffn_w_down[j].astype(BF16))
            xl = _dense_ffn(xl, norm2_g[layer], sh2, sc2, g2, *ffn_w)
            if need_ctx:
                xc = _dense_ffn(xc, norm2_g[layer], csh2, csc2, cg2, *ffn_w)
            if layer == depth - 1:
                out = _final_norm(xl, final_g)
        else:
            moe_w = (moe_w_gate[j].astype(BF16), moe_w_up[j].astype(BF16), moe_w_down[j].astype(BF16))
            if need_ctx:
                raise NotImplementedError("context tokens through a mixture-of-experts layer")
            out = _moe_layer_final(xl, norm2_g[layer], sh2, sc2, g2, moe_router[j], *moe_w, final_g)
    return out
```

```python
import functools

import jax
import jax.numpy as jnp
import numpy as np
from jax import lax
from jax.experimental import pallas as pl
from jax.experimental.pallas import tpu as pltpu

F32 = jnp.float32
BF16 = jnp.bfloat16

D_MODEL = 1024
GRID_W = 64
HEAD_DIM = 64
NA_WIDTH = 512
NA_HEADS = 8
NA_WIN_ROWS = 8
NA_WIN_COLS = 16
RW_WIDTH = 256
RW_HEADS = 4
RW_DECAY_RANK = 64
RW_LORA = 256
RW_GATE_RANK = 128
RW_IN = 1280
RW_GN_EPS = 64e-5
CV_WIDTH = 256
CV_CONV_LEN = 31
QKV_WIDTH = 3 * NA_WIDTH
IN_WIDTH = QKV_WIDTH + RW_IN + 2 * CV_WIDTH
N_EXPERTS = 8
TOP_K = 2
RMS_EPS = 1e-6
LN_EPS = 1e-5

LANES = 128
SUBLANES = 8
VMEM_LIMIT_BYTES = 56 * 1024 * 1024

ROW_TILE = 256
MASK_VALUE = -1e30
MOE_ROWS = 512
MOE_FF_TILE = 512
ROUTER_PAD = LANES
SCAN_STEPS = 32
RW_TILE = 128


def _params(*sem):
    return pltpu.CompilerParams(dimension_semantics=sem, vmem_limit_bytes=VMEM_LIMIT_BYTES)


def _col_chunks(width, step=512):
    out, c = [], 0
    while c < width:
        w = min(step, width - c)
        out.append((c, w))
        c += w
    return out


def _sigmoid(x):
    return 1.0 / (1.0 + jnp.exp(-x))


def _norm_mod(x, g, shift, scale):
    ms = jnp.mean(x * x, axis=-1, keepdims=True)
    h = x * lax.rsqrt(ms + RMS_EPS) * g
    return h * (1.0 + scale) + shift


def _dot_exact_rhs(x, m):
    hi = x.astype(BF16)
    r1 = x - hi.astype(F32)
    mid = r1.astype(BF16)
    lo = (r1 - mid.astype(F32)).astype(BF16)
    dot = functools.partial(jnp.dot, preferred_element_type=F32)
    return dot(hi, m) + dot(mid, m) + dot(lo, m)


def _mod_spec(n_batch, ctx_tiles):
    return pl.BlockSpec((1, 1, D_MODEL), lambda b, i: (jnp.where(i < ctx_tiles, n_batch, b), 0, 0))


def _row_spec(width, tile_offset=0):
    return pl.BlockSpec((1, ROW_TILE, width), lambda b, i: (b, i + tile_offset, 0))


def _whole(shape):
    return pl.BlockSpec(shape, lambda *_: (0,) * len(shape))


def _mod_kernel(c_ref, w_ref, b_ref, o_ref):
    c = c_ref[...]
    cs = c * _sigmoid(c)
    o_ref[0] = jnp.dot(cs, w_ref[0], precision=lax.Precision.HIGHEST,
                       preferred_element_type=F32) + b_ref[0]


def _modulation(c_all, mod_w, mod_b):
    depth, d, n = mod_w.shape
    rows = c_all.shape[0]
    tn = 768
    return pl.pallas_call(
        _mod_kernel,
        grid=(depth, n // tn),
        in_specs=[
            pl.BlockSpec((rows, d), lambda l, j: (0, 0)),
            pl.BlockSpec((1, d, tn), lambda l, j: (l, 0, j)),
            pl.BlockSpec((1, 1, tn), lambda l, j: (l, 0, j)),
        ],
        out_specs=pl.BlockSpec((1, rows, tn), lambda l, j: (l, 0, j)),
        out_shape=jax.ShapeDtypeStruct((depth, rows, n), F32),
        compiler_params=_params("arbitrary", "arbitrary"),
        name="modulation",
    )(c_all, mod_w, mod_b.reshape(depth, 1, n))


def _proj_kernel(x_ref, g_ref, sh_ref, sc_ref, w_ref, qkv_ref, rw_ref, cv_ref):
    hb = _norm_mod(x_ref[0], g_ref[...], sh_ref[0], sc_ref[0]).astype(BF16)
    for ref, base in ((qkv_ref, 0), (rw_ref, QKV_WIDTH), (cv_ref, QKV_WIDTH + RW_IN)):
        for c0, w in _col_chunks(ref.shape[-1]):
            y = jnp.dot(hb, w_ref[:, base + c0:base + c0 + w], preferred_element_type=F32)
            ref[0, :, c0:c0 + w] = y.astype(ref.dtype)


def _in_projection(x, g, shift, scale, w_in_bf16, ctx_tiles):
    b, t, d = x.shape
    ms = _mod_spec(b, ctx_tiles)
    return pl.pallas_call(
        _proj_kernel,
        grid=(b, t // ROW_TILE),
        in_specs=[_row_spec(d), _whole((1, d)), ms, ms, _whole((d, IN_WIDTH))],
        out_specs=[_row_spec(QKV_WIDTH), _row_spec(RW_IN), _row_spec(2 * CV_WIDTH)],
        out_shape=[
            jax.ShapeDtypeStruct((b, t, QKV_WIDTH), BF16),
            jax.ShapeDtypeStruct((b, t, RW_IN), F32),
            jax.ShapeDtypeStruct((b, t, 2 * CV_WIDTH), F32),
        ],
        compiler_params=_params("arbitrary", "arbitrary"),
        name="in_projection",
    )(x, g.reshape(1, d), shift, scale, w_in_bf16)


def _head_pair_queries(q):
    lane = lax.broadcasted_iota(jnp.int32, q.shape, 1)
    qs = q * jnp.asarray(HEAD_DIM ** -0.5, q.dtype)
    zero = jnp.zeros_like(qs)
    return jnp.concatenate([jnp.where(lane < HEAD_DIM, qs, zero), jnp.where(lane >= HEAD_DIM, qs, zero)], axis=0)


def _head_pair_merge(o, rows):
    lane = lax.broadcasted_iota(jnp.int32, (rows, LANES), 1)
    return jnp.where(lane < HEAD_DIM, o[:rows], o[rows:])


_NT = (((1,), (1,)), ((), ()))


def _na_kernel(q_ref, k_ref, v_ref, bias_ref, o_ref, *, rows, n_ctx):
    i = pl.program_id(2)
    rs = jnp.clip(i - NA_WIN_ROWS // 2, 0, rows - NA_WIN_ROWS)
    start = pl.multiple_of(n_ctx + rs * GRID_W, GRID_W)
    n_win = NA_WIN_ROWS * GRID_W
    k_win = k_ref[0, pl.ds(start, n_win), :]
    v_win = v_ref[0, pl.ds(start, n_win), :]
    qb = _head_pair_queries(q_ref[0])
    s_win = lax.dot_general(qb, k_win, _NT, preferred_element_type=F32) + bias_ref[0, 0]
    s_ctx = lax.dot_general(qb, k_ref[0, 0:n_ctx, :], _NT, preferred_element_type=F32)
    m = jnp.maximum(jnp.max(s_win, axis=-1, keepdims=True), jnp.max(s_ctx, axis=-1, keepdims=True))
    p_win = jnp.exp(s_win - m)
    p_ctx = jnp.exp(s_ctx - m)
    denom = jnp.sum(p_win, axis=-1, keepdims=True) + jnp.sum(p_ctx, axis=-1, keepdims=True)
    o = (jnp.dot(p_win.astype(BF16), v_win, preferred_element_type=F32)
         + jnp.dot(p_ctx.astype(BF16), v_ref[0, 0:n_ctx, :], preferred_element_type=F32)) / denom
    o_ref[0] = _head_pair_merge(o, GRID_W).astype(o_ref.dtype)


def _na_bias_table(rpb):
    qc = np.arange(GRID_W)[:, None]
    kc = np.arange(GRID_W)[None, :]
    ws = np.clip(qc - NA_WIN_COLS // 2, 0, GRID_W - NA_WIN_COLS)
    mask = (kc >= ws) & (kc < ws + NA_WIN_COLS)
    rel = np.clip(kc - qc + NA_WIN_COLS - 1, 0, 2 * NA_WIN_COLS - 2)
    full = jnp.where(mask[None, None], rpb[:, :, rel].astype(F32), MASK_VALUE)
    dr = np.arange(NA_WIN_ROWS)[:, None] + np.arange(NA_WIN_ROWS)[None, :]
    t = full[:, dr]
    t = t.transpose(0, 1, 3, 2, 4).reshape(NA_HEADS // 2, 2, NA_WIN_ROWS, GRID_W, NA_WIN_ROWS * GRID_W)
    return t.transpose(0, 2, 1, 3, 4).reshape(NA_HEADS // 2, NA_WIN_ROWS, 2 * GRID_W, NA_WIN_ROWS * GRID_W)


def _neighbourhood_attention(qkv, bias_table, n_ctx):
    b, t, _ = qkv.shape
    n = t - n_ctx
    rows = n // GRID_W
    hp = NA_HEADS // 2
    koff, voff = NA_WIDTH // LANES, 2 * NA_WIDTH // LANES
    q_off = n_ctx // GRID_W

    def bias_index(bi, h, i):
        rs = jnp.clip(i - NA_WIN_ROWS // 2, 0, rows - NA_WIN_ROWS)
        return (h, rs - i + NA_WIN_ROWS - 1, 0, 0)

    return pl.pallas_call(
        functools.partial(_na_kernel, rows=rows, n_ctx=n_ctx),
        grid=(b, hp, rows),
        in_specs=[
            pl.BlockSpec((1, GRID_W, LANES), lambda bi, h, i: (bi, i + q_off, h)),
            pl.BlockSpec((1, t, LANES), lambda bi, h, i: (bi, 0, koff + h)),
            pl.BlockSpec((1, t, LANES), lambda bi, h, i: (bi, 0, voff + h)),
            pl.BlockSpec((1, 1, 2 * GRID_W, NA_WIN_ROWS * GRID_W), bias_index),
        ],
        out_specs=pl.BlockSpec((1, GRID_W, LANES), lambda bi, h, i: (bi, i, h)),
        out_shape=jax.ShapeDtypeStruct((b, n, NA_WIDTH), BF16),
        compiler_params=_params("arbitrary", "arbitrary", "arbitrary"),
        name="neighbourhood_attention",
    )(qkv, qkv, qkv, bias_table)


def _ctx_attn_kernel(q_ref, k_ref, v_ref, o_ref):
    l = q_ref.shape[1]
    qb = _head_pair_queries(q_ref[0])
    s = lax.dot_general(qb, k_ref[0], _NT, preferred_element_type=F32)
    p = jnp.exp(s - jnp.max(s, axis=-1, keepdims=True))
    o = jnp.dot(p.astype(BF16), v_ref[0], preferred_element_type=F32) / jnp.sum(p, axis=-1, keepdims=True)
    o_ref[0] = _head_pair_merge(o, l).astype(o_ref.dtype)


def _context_attention(qkv, n_ctx):
    b = qkv.shape[0]
    hp = NA_HEADS // 2
    koff, voff = NA_WIDTH // LANES, 2 * NA_WIDTH // LANES
    return pl.pallas_call(
        _ctx_attn_kernel,
        grid=(b, hp),
        in_specs=[
            pl.BlockSpec((1, n_ctx, LANES), lambda bi, h: (bi, 0, h)),
            pl.BlockSpec((1, n_ctx, LANES), lambda bi, h: (bi, 0, koff + h)),
            pl.BlockSpec((1, n_ctx, LANES), lambda bi, h: (bi, 0, voff + h)),
        ],
        out_specs=pl.BlockSpec((1, n_ctx, LANES), lambda bi, h: (bi, 0, h)),
        out_shape=jax.ShapeDtypeStruct((b, n_ctx, NA_WIDTH), BF16),
        compiler_params=_params("arbitrary", "arbitrary"),
        name="context_attention",
    )(qkv, qkv, qkv)


SCAN_GROUPS = ("r", "v", "z", "w0", "k0", "b0", "w1", "k1", "b1")
POST_COLS = 4 * RW_WIDTH


def _softplus(x):
    return jnp.maximum(x, 0.0) + jnp.log(1.0 + jnp.exp(-jnp.abs(x)))


def _rw_prep_kernel(u_ref, up_ref, un_ref, mup_ref, mun_ref, ones_ref, kk_ref, ka_ref, rk_ref,
                    w0_ref, a0_ref, wa_ref, g2_ref, scan_ref, post_ref, *, n_ctx):
    i = pl.program_id(0)
    b = pl.program_id(1)
    tt = u_ref.shape[1]
    n_tok = pl.num_programs(0) * tt
    u = u_ref[0]
    row = lax.broadcasted_iota(jnp.int32, u.shape, 0)
    tok = row + i * tt
    prev = jnp.where(row == 0, up_ref[0, SUBLANES - 1:SUBLANES, :], pltpu.roll(u, 1, 0))
    prev = jnp.where(tok == 0, 0.0, jnp.where(tok == n_ctx, 0.0, prev))
    nxt = jnp.where(row == tt - 1, un_ref[0, 0:1, :], pltpu.roll(u, tt - 1, 0))
    nxt = jnp.where(tok == n_tok - 1, 0.0, jnp.where(tok == n_ctx - 1, 0.0, nxt))
    us = u + mup_ref[...] * (prev - u) + mun_ref[...] * (nxt - u)

    def put(group, val):
        g = SCAN_GROUPS.index(group)
        for c in range(2):
            scan_ref[2 * g + c, pl.ds(b, tt, stride=SUBLANES), :] = val[:, c * LANES:(c + 1) * LANES]

    ones = ones_ref[...]
    r = us[:, 0:RW_WIDTH]
    k = us[:, RW_WIDTH:2 * RW_WIDTH]
    v = us[:, 2 * RW_WIDTH:3 * RW_WIDTH]
    kk = k * kk_ref[...]
    kk = kk * lax.rsqrt(_dot_exact_rhs(kk * kk, ones) + 1e-12)
    put("r", r)
    put("v", v)
    put("z", -kk)
    for d in range(2):
        lo = 3 * RW_WIDTH + d * RW_LORA
        wa_in = us[:, lo:lo + LANES]
        lane = lax.broadcasted_iota(jnp.int32, wa_in.shape, 1)
        wa_in = jnp.where(lane < RW_DECAY_RANK, jnp.tanh(wa_in), wa_in)
        wa = jnp.dot(wa_in.astype(BF16), wa_ref[d], preferred_element_type=F32)
        log_w = -_softplus(-(w0_ref[d] + wa[:, 0:RW_WIDTH])) - 0.5
        decay = jnp.exp(-jnp.exp(log_w))
        a = _sigmoid(a0_ref[d] + wa[:, RW_WIDTH:2 * RW_WIDTH])
        gate_in = _sigmoid(us[:, lo + LANES:lo + 2 * LANES])
        g = jnp.dot(gate_in.astype(BF16), g2_ref[d], preferred_element_type=F32)
        kd = k * (1.0 + (a - 1.0) * ka_ref[...])
        bonus = _dot_exact_rhs(r * kd * rk_ref[...], ones) * v
        put("w%d" % d, decay)
        put("k%d" % d, kd)
        put("b%d" % d, kk * a)
        post_ref[0, :, 2 * d * RW_WIDTH:(2 * d + 1) * RW_WIDTH] = g
        post_ref[0, :, (2 * d + 1) * RW_WIDTH:(2 * d + 2) * RW_WIDTH] = bonus


def _head_ones():
    h = np.arange(RW_WIDTH) // HEAD_DIM
    return jnp.asarray(h[:, None] == h[None, :], BF16)


def _rw_prep(u, p, n_ctx):
    b, t, _ = u.shape
    assert b == SUBLANES
    tt = RW_TILE
    nt8 = t // SUBLANES
    row = lambda vec, n: vec.reshape(1, n)
    return pl.pallas_call(
        functools.partial(_rw_prep_kernel, n_ctx=n_ctx),
        grid=(t // tt, b),
        in_specs=[
            pl.BlockSpec((1, tt, RW_IN), lambda i, bi: (bi, i, 0)),
            pl.BlockSpec((1, SUBLANES, RW_IN), lambda i, bi: (bi, jnp.maximum(i * (tt // SUBLANES) - 1, 0), 0)),
            pl.BlockSpec((1, SUBLANES, RW_IN), lambda i, bi: (bi, jnp.minimum((i + 1) * (tt // SUBLANES), nt8 - 1), 0)),
            _whole((1, RW_IN)), _whole((1, RW_IN)),
            _whole((RW_WIDTH, RW_WIDTH)),
            _whole((1, RW_WIDTH)), _whole((1, RW_WIDTH)), _whole((1, RW_WIDTH)),
            _whole((2, 1, RW_WIDTH)), _whole((2, 1, RW_WIDTH)),
            _whole((2, LANES, 2 * RW_WIDTH)), _whole((2, RW_GATE_RANK, RW_WIDTH)),
        ],
        out_specs=[
            pl.BlockSpec((2 * len(SCAN_GROUPS), tt * b, LANES), lambda i, bi: (0, i, 0)),
            pl.BlockSpec((1, tt, POST_COLS), lambda i, bi: (bi, i, 0)),
        ],
        out_shape=[
            jax.ShapeDtypeStruct((2 * len(SCAN_GROUPS), t * b, LANES), F32),
            jax.ShapeDtypeStruct((b, t, POST_COLS), F32),
        ],
        compiler_params=_params("arbitrary", "arbitrary"),
        name="rwkv_prep",
    )(u, u, u, row(p["mu_prev"], RW_IN), row(p["mu_next"], RW_IN), _head_ones(),
      row(p["k_k"], RW_WIDTH), row(p["k_a"], RW_WIDTH), row(p["r_k"], RW_WIDTH),
      p["w0"].reshape(2, 1, RW_WIDTH), p["a0"].reshape(2, 1, RW_WIDTH), p["wa"], p["g2"])


def _rw_lora_weights(w2, a2):
    z = jnp.zeros_like(w2)
    top = jnp.concatenate([w2, z], axis=-1)
    bot = jnp.concatenate([z, a2], axis=-1)
    return jnp.concatenate([top, bot], axis=1).astype(BF16)


_QUARTER = LANES // 4
_VROWS = HEAD_DIM // 4


def _chain_tile(f_ref, b_ref, sf, sb):
    pieces = [f_ref[0, pl.ds(sf, SUBLANES), :], f_ref[1, pl.ds(sf, SUBLANES), :],
              b_ref[0, pl.ds(sb, SUBLANES), :], b_ref[1, pl.ds(sb, SUBLANES), :]]
    return jnp.concatenate(pieces * 4, axis=0).T


def _scan_kernel(rf, vf, zf, wf, kf, bf, rb, vb, zb, wb, kb, bb, yf_ref, yb_ref, s_ref, t_ref):
    steps = rf.shape[1] // SUBLANES

    @pl.when(pl.program_id(0) == 0)
    def _():
        s_ref[...] = jnp.zeros_like(s_ref)

    lane_q = lax.broadcasted_iota(jnp.int32, (_VROWS, LANES), 1) // _QUARTER
    col_q = (lax.broadcasted_iota(jnp.int32, (_QUARTER, LANES), 1) % HEAD_DIM) // _VROWS

    def step(s, carry):
        sf = pl.multiple_of(s * SUBLANES, SUBLANES)
        sb = pl.multiple_of((steps - 1 - s) * SUBLANES, SUBLANES)
        for n, (f, b) in enumerate(((zf, zb), (wf, wb), (bf, bb), (kf, kb), (rf, rb))):
            t_ref[n] = _chain_tile(f, b, sf, sb)
        vt = _chain_tile(vf, vb, sf, sb)
        ys = []
        for h in range(2):
            base = h * HEAD_DIM
            v = jnp.zeros((_VROWS, LANES), F32)
            for q in range(4):
                v = jnp.where(lane_q == q, vt[base + q * _VROWS:base + (q + 1) * _VROWS], v)
            sz = [jnp.zeros((_VROWS, LANES), F32), jnp.zeros((_VROWS, LANES), F32)]
            for k in range(HEAD_DIM):
                sz[k % 2] = sz[k % 2] + s_ref[base + k] * t_ref[0, pl.ds(base + k, 1), :]
            sz = sz[0] + sz[1]
            y = [jnp.zeros((_VROWS, LANES), F32), jnp.zeros((_VROWS, LANES), F32)]
            for k in range(HEAD_DIM):
                row = pl.ds(base + k, 1)
                st = s_ref[base + k] * t_ref[1, row, :] + sz * t_ref[2, row, :] + v * t_ref[3, row, :]
                s_ref[base + k] = st
                y[k % 2] = y[k % 2] + st * t_ref[4, row, :]
            ys.append(y[0] + y[1])
        yt = jnp.concatenate([ys[0]] * 4 + [ys[1]] * 4, axis=0).T
        nat = jnp.zeros((_QUARTER, LANES), F32)
        for q in range(4):
            nat = jnp.where(col_q == q, yt[q * _QUARTER:(q + 1) * _QUARTER], nat)
        yf_ref[0, pl.ds(sf, SUBLANES), :] = nat[0:8]
        yf_ref[1, pl.ds(sf, SUBLANES), :] = nat[8:16]
        yb_ref[0, pl.ds(sb, SUBLANES), :] = nat[16:24]
        yb_ref[1, pl.ds(sb, SUBLANES), :] = nat[24:32]
        return carry

    lax.fori_loop(0, steps, step, 0)


def _rwkv7_scan(ops, n_batch, n_ctx):
    rows = ops.shape[1]
    blk = SCAN_STEPS * n_batch
    nb = rows // blk
    nb_c = n_ctx * n_batch // blk

    def mirror(i):
        return jnp.where(i < nb_c, nb_c - 1 - i, nb_c + nb - 1 - i)

    def spec(group, backward):
        g = SCAN_GROUPS.index(group)
        if backward:
            return pl.BlockSpec((2, blk, LANES), lambda i: (g, mirror(i), 0))
        return pl.BlockSpec((2, blk, LANES), lambda i: (g, i, 0))

    fw = [spec(g, False) for g in ("r", "v", "z", "w0", "k0", "b0")]
    bw = [spec(g, True) for g in ("r", "v", "z", "w1", "k1", "b1")]
    out_sds = jax.ShapeDtypeStruct((2, rows, LANES), F32)
    return pl.pallas_call(
        _scan_kernel,
        grid=(nb,),
        in_specs=fw + bw,
        out_specs=[pl.BlockSpec((2, blk, LANES), lambda i: (0, i, 0)),
                   pl.BlockSpec((2, blk, LANES), lambda i: (0, mirror(i), 0))],
        out_shape=[out_sds, out_sds],
        scratch_shapes=[pltpu.VMEM((2 * HEAD_DIM, _VROWS, LANES), F32), pltpu.VMEM((5, LANES, LANES), F32)],
        compiler_params=_params("arbitrary"),
        name="rwkv_scan",
    )(*([ops] * 12))


def _rw_post_kernel(yf_ref, yb_ref, post_ref, ones_ref, gg_ref, gb_ref, o_ref):
    b = pl.program_id(1)
    tt = o_ref.shape[1]
    ones = ones_ref[...]
    out = None
    for d, y_ref in enumerate((yf_ref, yb_ref)):
        y = jnp.concatenate([y_ref[0, pl.ds(b, tt, stride=SUBLANES), :],
                             y_ref[1, pl.ds(b, tt, stride=SUBLANES), :]], axis=1)
        mu = _dot_exact_rhs(y, ones) * (1.0 / HEAD_DIM)
        yc = y - mu
        var = _dot_exact_rhs(yc * yc, ones) * (1.0 / HEAD_DIM)
        yn = yc * lax.rsqrt(var + RW_GN_EPS) * gg_ref[...] + gb_ref[...]
        g = post_ref[0, :, 2 * d * RW_WIDTH:(2 * d + 1) * RW_WIDTH]
        bonus = post_ref[0, :, (2 * d + 1) * RW_WIDTH:(2 * d + 2) * RW_WIDTH]
        term = (yn + bonus) * g
        out = term if out is None else out + term
    o_ref[0] = out.astype(o_ref.dtype)


def _rw_post(y_f, y_b, post, gn_g, gn_b):
    b, t, _ = post.shape
    tt = RW_TILE
    yspec = pl.BlockSpec((2, tt * b, LANES), lambda i, bi: (0, i, 0))
    return pl.pallas_call(
        _rw_post_kernel,
        grid=(t // tt, b),
        in_specs=[yspec, yspec, pl.BlockSpec((1, tt, POST_COLS), lambda i, bi: (bi, i, 0)),
                  _whole((RW_WIDTH, RW_WIDTH)), _whole((1, RW_WIDTH)), _whole((1, RW_WIDTH))],
        out_specs=pl.BlockSpec((1, tt, RW_WIDTH), lambda i, bi: (bi, i, 0)),
        out_shape=jax.ShapeDtypeStruct((b, t, RW_WIDTH), BF16),
        compiler_params=_params("arbitrary", "arbitrary"),
        name="rwkv_post",
    )(y_f, y_b, post, _head_ones(), gn_g.reshape(1, RW_WIDTH), gn_b.reshape(1, RW_WIDTH))


def _bi_rwkv7(u, p, n_ctx):
    ops, post = _rw_prep(u, p, n_ctx)
    y_f, y_b = _rwkv7_scan(ops, u.shape[0], n_ctx)
    return _rw_post(y_f, y_b, post, p["gn_g"], p["gn_b"])


CONV_PAD = 16


def _conv_kernel(u_ref, w_ref, b_ref, lg_ref, lb_ref, o_ref, pad_ref, *, n_ctx):
    t = u_ref.shape[1]
    half = CV_CONV_LEN // 2
    chunk = ROW_TILE
    u = u_ref[0]
    h = u[:, :CV_WIDTH] * _sigmoid(u[:, CV_WIDTH:])
    zeros = jnp.zeros((CONV_PAD, CV_WIDTH), F32)
    lat0 = 2 * CONV_PAD + n_ctx
    pad_ref[0:CONV_PAD, :] = zeros
    pad_ref[CONV_PAD:CONV_PAD + n_ctx, :] = h[:n_ctx]
    pad_ref[CONV_PAD + n_ctx:lat0, :] = zeros
    pad_ref[lat0:lat0 + t - n_ctx, :] = h[n_ctx:]
    pad_ref[lat0 + t - n_ctx:lat0 + t - n_ctx + CONV_PAD, :] = zeros
    w = w_ref[...]

    def body(c, carry):
        out0 = pl.multiple_of(c * chunk, chunk)
        base = pl.multiple_of(out0 + jnp.where(c >= n_ctx // chunk, CONV_PAD, 0), CONV_PAD)
        win = pad_ref[pl.ds(base, chunk + 2 * CONV_PAD), :]
        acc = jnp.zeros((chunk, CV_WIDTH), F32)
        for j in range(CV_CONV_LEN):
            off = CONV_PAD - half + j
            acc = acc + win[off:off + chunk, :] * w[j:j + 1, :]
        hh = acc + b_ref[...]
        mu = jnp.mean(hh, axis=-1, keepdims=True)
        hc = hh - mu
        var = jnp.mean(hc * hc, axis=-1, keepdims=True)
        y = hc * lax.rsqrt(var + LN_EPS) * lg_ref[...] + lb_ref[...]
        o_ref[0, pl.ds(out0, chunk), :] = (y * _sigmoid(y)).astype(o_ref.dtype)
        return carry

    lax.fori_loop(0, t // chunk, body, 0)


def _conformer_conv(u, dw_w, dw_b, ln_g, ln_b, n_ctx):
    b, t, _ = u.shape
    assert n_ctx % ROW_TILE == 0 and t % ROW_TILE == 0
    row = lambda vec: vec.reshape(1, CV_WIDTH)
    return pl.pallas_call(
        functools.partial(_conv_kernel, n_ctx=n_ctx),
        grid=(b,),
        in_specs=[pl.BlockSpec((1, t, 2 * CV_WIDTH), lambda bi: (bi, 0, 0)),
                  _whole((CV_CONV_LEN, CV_WIDTH)), _whole((1, CV_WIDTH)), _whole((1, CV_WIDTH)),
                  _whole((1, CV_WIDTH))],
        out_specs=pl.BlockSpec((1, t, CV_WIDTH), lambda bi: (bi, 0, 0)),
        out_shape=jax.ShapeDtypeStruct((b, t, CV_WIDTH), BF16),
        scratch_shapes=[pltpu.VMEM((t + 3 * CONV_PAD, CV_WIDTH), F32)],
        compiler_params=_params("arbitrary"),
        name="conformer_conv",
    )(u, dw_w, row(dw_b), row(ln_g), row(ln_b))


def _wout_kernel(x_ref, nal_ref, nac_ref, rw_ref, cv_ref, w_ref, gate_ref, o_ref, *, ctx_tiles):
    dot = functools.partial(jnp.dot, preferred_element_type=F32)
    na = nal_ref[0]
    if ctx_tiles:
        na = jnp.where(pl.program_id(1) < ctx_tiles, nac_ref[0], na)
    acc = (dot(na, w_ref[0:NA_WIDTH, :])
           + dot(rw_ref[0], w_ref[NA_WIDTH:NA_WIDTH + RW_WIDTH, :])
           + dot(cv_ref[0], w_ref[NA_WIDTH + RW_WIDTH:, :]))
    o_ref[0] = x_ref[0] + gate_ref[0] * acc


def _out_projection(x, na_l, na_c, rw, cv, w_out_bf16, gate, n_ctx, with_ctx):
    b, t, d = x.shape
    ctx_tiles = n_ctx // ROW_TILE
    assert ctx_tiles == 1 or not with_ctx
    off = 0 if with_ctx else ctx_tiles
    n_rows = t if with_ctx else t - n_ctx
    if with_ctx:
        nal_spec = pl.BlockSpec((1, ROW_TILE, NA_WIDTH), lambda bi, i: (bi, jnp.maximum(i - ctx_tiles, 0), 0))
    else:
        nal_spec = _row_spec(NA_WIDTH)
        na_c = na_l
    return pl.pallas_call(
        functools.partial(_wout_kernel, ctx_tiles=ctx_tiles if with_ctx else 0),
        grid=(b, n_rows // ROW_TILE),
        in_specs=[_row_spec(d, off), nal_spec,
                  pl.BlockSpec((1, ROW_TILE, NA_WIDTH), lambda bi, i: (bi, 0, 0)),
                  _row_spec(RW_WIDTH, off), _row_spec(CV_WIDTH, off),
                  _whole((d, d)), _mod_spec(b, ctx_tiles if with_ctx else 0)],
        out_specs=_row_spec(d),
        out_shape=jax.ShapeDtypeStruct((b, n_rows, d), F32),
        compiler_params=_params("arbitrary", "arbitrary"),
        name="out_projection",
    )(x, na_l, na_c, rw, cv, w_out_bf16, gate)


def _ffn_kernel(x_ref, g_ref, sh_ref, sc_ref, gate_ref, wg_ref, wu_ref, wd_ref, o_ref, acc_ref):
    x = x_ref[0]
    hb = _norm_mod(x, g_ref[...], sh_ref[0], sc_ref[0]).astype(BF16)
    dot = functools.partial(jnp.dot, preferred_element_type=F32)
    for n, (c0, w) in enumerate(_col_chunks(wg_ref.shape[1])):
        g = dot(hb, wg_ref[:, c0:c0 + w])
        u = dot(hb, wu_ref[:, c0:c0 + w])
        a = (g * _sigmoid(g) * u).astype(BF16)
        y = dot(a, wd_ref[c0:c0 + w, :])
        if n == 0:
            acc_ref[...] = y
        else:
            acc_ref[...] += y
    o_ref[0] = x + gate_ref[0] * acc_ref[...]


def _dense_ffn(x, g, shift, scale, gate, wg, wu, wd, ctx_tiles):
    b, t, d = x.shape
    dff = wg.shape[1]
    ms = _mod_spec(b, ctx_tiles)
    return pl.pallas_call(
        _ffn_kernel,
        grid=(b, t // ROW_TILE),
        in_specs=[_row_spec(d), _whole((1, d)), ms, ms, ms,
                  _whole((d, dff)), _whole((d, dff)), _whole((dff, d))],
        out_specs=_row_spec(d),
        out_shape=jax.ShapeDtypeStruct((b, t, d), F32),
        scratch_shapes=[pltpu.VMEM((ROW_TILE, d), F32)],
        compiler_params=_params("arbitrary", "arbitrary"),
        name="dense_ffn",
    )(x, g.reshape(1, d), shift, scale, gate, wg, wu, wd)


def _router_kernel(x_ref, g_ref, sh_ref, sc_ref, wr_ref, h_ref, logit_ref):
    h = _norm_mod(x_ref[0], g_ref[...], sh_ref[0], sc_ref[0])
    h_ref[0] = h
    logit_ref[0] = jnp.dot(h, wr_ref[...], precision=lax.Precision.HIGHEST, preferred_element_type=F32)


def _moe_router(x, g, shift, scale, router_pad):
    b, t, d = x.shape
    ms = _mod_spec(b, 0)
    return pl.pallas_call(
        _router_kernel,
        grid=(b, t // ROW_TILE),
        in_specs=[_row_spec(d), _whole((1, d)), ms, ms, _whole((d, ROUTER_PAD))],
        out_specs=[_row_spec(d), _row_spec(ROUTER_PAD)],
        out_shape=[jax.ShapeDtypeStruct((b, t, d), F32), jax.ShapeDtypeStruct((b, t, ROUTER_PAD), F32)],
        compiler_params=_params("arbitrary", "arbitrary"),
        name="moe_router",
    )(x, g.reshape(1, d), shift, scale, router_pad)


def _row_copy(src_hbm, dst_vmem, sem, src_row, dst_row):
    return pltpu.make_async_copy(src_hbm.at[pl.ds(src_row, 1)], dst_vmem.at[pl.ds(dst_row, 1)], sem)


def _expert_kernel(be_ref, nused_ref, tok_ref, h_hbm, wg_ref, wu_ref, wd_ref, o_ref, xg_ref, xb_ref, acc_ref, sem):
    i = pl.program_id(0)
    j = pl.program_id(1)
    rows = xg_ref.shape[0]
    used = i < nused_ref[0]

    @pl.when(jnp.logical_and(used, j == 0))
    def _():
        def start(r, c):
            _row_copy(h_hbm, xg_ref, sem, tok_ref[0, 0, r], r).start()
            return c

        def wait(r, c):
            _row_copy(h_hbm, xg_ref, sem, 0, r).wait()
            return c

        lax.fori_loop(0, rows, start, 0)
        lax.fori_loop(0, rows, wait, 0)
        xb_ref[...] = xg_ref[...].astype(BF16)

    @pl.when(used)
    def _():
        dot = functools.partial(jnp.dot, preferred_element_type=F32)
        xb = xb_ref[...]
        g = dot(xb, wg_ref[0])
        u = dot(xb, wu_ref[0])
        a = (g * _sigmoid(g) * u).astype(BF16)
        y = dot(a, wd_ref[0])

        @pl.when(j == 0)
        def _():
            acc_ref[...] = y

        @pl.when(j > 0)
        def _():
            acc_ref[...] += y

    last = j == pl.num_programs(1) - 1

    @pl.when(jnp.logical_and(used, last))
    def _():
        o_ref[...] = acc_ref[...]

    @pl.when(jnp.logical_and(jnp.logical_not(used), last))
    def _():
        o_ref[...] = jnp.zeros_like(o_ref)


def _moe_experts(h_flat, slot_tok, block_e, n_used, wg, wu, wd):
    nb = slot_tok.shape[0]
    d = h_flat.shape[1]
    dff = wg.shape[2]
    nj = dff // MOE_FF_TILE
    grid_spec = pltpu.PrefetchScalarGridSpec(
        num_scalar_prefetch=2,
        grid=(nb, nj),
        in_specs=[
            pl.BlockSpec((1, 1, MOE_ROWS), lambda i, j, be, nu: (i, 0, 0), memory_space=pltpu.SMEM),
            pl.BlockSpec(memory_space=pl.ANY),
            pl.BlockSpec((1, d, MOE_FF_TILE), lambda i, j, be, nu: (be[i], 0, j)),
            pl.BlockSpec((1, d, MOE_FF_TILE), lambda i, j, be, nu: (be[i], 0, j)),
            pl.BlockSpec((1, MOE_FF_TILE, d), lambda i, j, be, nu: (be[i], j, 0)),
        ],
        out_specs=pl.BlockSpec((MOE_ROWS, d), lambda i, j, be, nu: (i, 0)),
        scratch_shapes=[pltpu.VMEM((MOE_ROWS, d), F32), pltpu.VMEM((MOE_ROWS, d), BF16),
                        pltpu.VMEM((MOE_ROWS, d), F32), pltpu.SemaphoreType.DMA(())],
    )
    return pl.pallas_call(
        _expert_kernel,
        grid_spec=grid_spec,
        out_shape=jax.ShapeDtypeStruct((nb * MOE_ROWS, d), F32),
        compiler_params=_params("arbitrary", "arbitrary"),
        name="moe_experts",
    )(block_e, n_used, slot_tok, h_flat, wg, wu, wd)


def _combine_kernel(dest_ref, x_ref, gates_ref, gate2_ref, fg_ref, yb_hbm, o_ref, y0_ref, y1_ref, sem):
    rows = y0_ref.shape[0]

    def start(r, c):
        _row_copy(yb_hbm, y0_ref, sem, dest_ref[0, 0, 2 * r], r).start()
        _row_copy(yb_hbm, y1_ref, sem, dest_ref[0, 0, 2 * r + 1], r).start()
        return c

    def wait(r, c):
        _row_copy(yb_hbm, y0_ref, sem, 0, r).wait()
        _row_copy(yb_hbm, y1_ref, sem, 0, r).wait()
        return c

    lax.fori_loop(0, rows, start, 0)
    lax.fori_loop(0, rows, wait, 0)
    gates = gates_ref[0]
    y = y0_ref[...] * gates[:, 0:1] + y1_ref[...] * gates[:, 1:2]
    x = x_ref[0] + gate2_ref[0] * y
    ms = jnp.mean(x * x, axis=-1, keepdims=True)
    o_ref[0] = x * lax.rsqrt(ms + RMS_EPS) * fg_ref[...]


def _moe_combine_final(x, gates, dest, yb, gate2, final_g):
    b, t, d = x.shape
    tm = ROW_TILE
    return pl.pallas_call(
        _combine_kernel,
        grid=(b, t // tm),
        in_specs=[
            pl.BlockSpec((1, 1, TOP_K * tm), lambda bi, i: (bi * (t // tm) + i, 0, 0), memory_space=pltpu.SMEM),
            _row_spec(d),
            _row_spec(TOP_K),
            _mod_spec(b, 0),
            _whole((1, d)),
            pl.BlockSpec(memory_space=pl.ANY),
        ],
        out_specs=_row_spec(d),
        out_shape=jax.ShapeDtypeStruct((b, t, d), F32),
        scratch_shapes=[pltpu.VMEM((tm, d), F32), pltpu.VMEM((tm, d), F32), pltpu.SemaphoreType.DMA(())],
        compiler_params=_params("arbitrary", "arbitrary"),
        name="moe_combine_final",
    )(dest, x, gates, gate2, final_g.reshape(1, d), yb)


def _moe_layer_final(x, g, shift, scale, gate2, router, wg, wu, wd, final_g):
    b, t, d = x.shape
    n_tok = b * t
    n_asg = n_tok * TOP_K
    router_pad = jnp.pad(router, ((0, 0), (0, ROUTER_PAD - N_EXPERTS)))
    h, logits = _moe_router(x, g, shift, scale, router_pad)
    logits = logits.reshape(n_tok, ROUTER_PAD)[:, :N_EXPERTS]
    top_logit, top_e = lax.top_k(logits, TOP_K)
    gates = jax.nn.softmax(top_logit, axis=-1)
    flat_e = top_e.reshape(-1)
    onehot = (flat_e[:, None] == jnp.arange(N_EXPERTS, dtype=flat_e.dtype)[None, :]).astype(jnp.int32)
    rank = jnp.cumsum(onehot, axis=0) - onehot
    counts = jnp.sum(onehot, axis=0)
    padded = (counts + MOE_ROWS - 1) // MOE_ROWS * MOE_ROWS
    pad_end = jnp.cumsum(padded)
    pad_start = pad_end - padded
    dest = (pad_start[flat_e] + jnp.sum(rank * onehot, axis=1)).astype(jnp.int32)
    n_blocks = (n_asg + MOE_ROWS - 1) // MOE_ROWS + N_EXPERTS
    flat_tok = jnp.arange(n_asg, dtype=jnp.int32) // TOP_K
    slot_tok = jnp.zeros((n_blocks * MOE_ROWS,), jnp.int32).at[dest].set(flat_tok)
    block_e = jnp.minimum(jnp.searchsorted(pad_end, jnp.arange(n_blocks) * MOE_ROWS, side="right"),
                          N_EXPERTS - 1).astype(jnp.int32)
    n_used = (pad_end[-1:] // MOE_ROWS).astype(jnp.int32)
    yb = _moe_experts(h.reshape(n_tok, d), slot_tok.reshape(n_blocks, 1, MOE_ROWS), block_e, n_used, wg, wu, wd)
    return _moe_combine_final(x, gates.reshape(b, t, TOP_K), dest.reshape(-1, 1, TOP_K * ROW_TILE), yb,
                              gate2, final_g)


def _final_norm_kernel(x_ref, g_ref, o_ref):
    x = x_ref[0]
    ms = jnp.mean(x * x, axis=-1, keepdims=True)
    o_ref[0] = x * lax.rsqrt(ms + RMS_EPS) * g_ref[...]


def _final_norm(x, g, tile_offset):
    b, t, d = x.shape
    n_rows = t - tile_offset * ROW_TILE
    return pl.pallas_call(
        _final_norm_kernel,
        grid=(b, n_rows // ROW_TILE),
        in_specs=[_row_spec(d, tile_offset), _whole((1, d))],
        out_specs=_row_spec(d),
        out_shape=jax.ShapeDtypeStruct((b, n_rows, d), F32),
        compiler_params=_params("arbitrary", "arbitrary"),
        name="final_norm",
    )(x, g.reshape(1, d))


def kernel(x, c, ctx, c_ctx, norm1_g, norm2_g, mod_w, mod_b, w_in, w_out, na_rpb, rw_mu_prev, rw_mu_next, rw_w0, rw_w2, rw_a0, rw_a2, rw_g2, rw_k_k, rw_k_a, rw_r_k, rw_gn_g, rw_gn_b, cv_dw_w, cv_dw_b, cv_ln_g, cv_ln_b, ffn_w_gate, ffn_w_up, ffn_w_down, moe_router, moe_w_gate, moe_w_up, moe_w_down, final_g):
    b, n, _ = x.shape
    n_ctx = ctx.shape[1]
    depth = mod_w.shape[0]
    assert n_ctx == ROW_TILE and n % ROW_TILE == 0 and b == SUBLANES
    ctx_tiles = n_ctx // ROW_TILE
    c_rows = 2 * SUBLANES
    c_all = jnp.concatenate([c, c_ctx[None, :], jnp.zeros((c_rows - b - 1, D_MODEL), F32)], axis=0)
    mod = _modulation(c_all, mod_w, mod_b)

    xa = jnp.concatenate([ctx, x], axis=1)
    out = None
    for layer in range(depth):
        last = layer == depth - 1
        m = mod[layer, :b + 1].reshape(b + 1, 6, 1, D_MODEL)
        sh1, sc1, g1, sh2, sc2, g2 = (m[:, k] for k in range(6))
        w_in_b = w_in[layer].astype(BF16)
        w_out_b = w_out[layer].astype(BF16)
        qkv, rw_in, cv_in = _in_projection(xa, norm1_g[layer], sh1, sc1, w_in_b, ctx_tiles)
        na_l = _neighbourhood_attention(qkv, _na_bias_table(na_rpb[layer]), n_ctx)
        rw_p = dict(mu_prev=rw_mu_prev[layer], mu_next=rw_mu_next[layer], w0=rw_w0[layer], a0=rw_a0[layer],
                    wa=_rw_lora_weights(rw_w2[layer], rw_a2[layer]), g2=rw_g2[layer].astype(BF16),
                    k_k=rw_k_k[layer], k_a=rw_k_a[layer], r_k=rw_r_k[layer].reshape(-1),
                    gn_g=rw_gn_g[layer], gn_b=rw_gn_b[layer])
        rw_o = _bi_rwkv7(rw_in, rw_p, n_ctx)
        cv_o = _conformer_conv(cv_in, cv_dw_w[layer], cv_dw_b[layer], cv_ln_g[layer], cv_ln_b[layer], n_ctx)
        na_c = None if last else _context_attention(qkv, n_ctx)
        xa = _out_projection(xa, na_l, na_c, rw_o, cv_o, w_out_b, g1, n_ctx, with_ctx=not last)
        j = layer // 2
        if layer % 2 == 0:
            ffn_w = (ffn_w_gate[j].astype(BF16), ffn_w_up[j].astype(BF16), ffn_w_down[j].astype(BF16))
            xa = _dense_ffn(xa, norm2_g[layer], sh2, sc2, g2, *ffn_w, 0 if last else ctx_tiles)
            if last:
                out = _final_norm(xa, final_g, 0)
        else:
            if not last:
                raise NotImplementedError("context tokens through a mixture-of-experts layer")
            moe_w = (moe_w_gate[j].astype(BF16), moe_w_up[j].astype(BF16), moe_w_down[j].astype(BF16))
            out = _moe_layer_final(xa, norm2_g[layer], sh2, sc2, g2, moe_router[j], *moe_w, final_g)
    return out
```

```python
import functools

import jax
import jax.numpy as jnp
import numpy as np
from jax import lax
from jax.experimental import pallas as pl
from jax.experimental.pallas import tpu as pltpu

F32 = jnp.float32
BF16 = jnp.bfloat16

D_MODEL = 1024
GRID_W = 64
HEAD_DIM = 64
NA_WIDTH = 512
NA_HEADS = 8
NA_WIN_ROWS = 8
NA_WIN_COLS = 16
RW_WIDTH = 256
RW_HEADS = 4
RW_DECAY_RANK = 64
RW_LORA = 256
RW_GATE_RANK = 128
RW_IN = 1280
RW_GN_EPS = 64e-5
CV_WIDTH = 256
CV_CONV_LEN = 31
QKV_WIDTH = 3 * NA_WIDTH
IN_WIDTH = QKV_WIDTH + RW_IN + 2 * CV_WIDTH
N_EXPERTS = 8
TOP_K = 2
RMS_EPS = 1e-6
LN_EPS = 1e-5

LANES = 128
SUBLANES = 8
VMEM_LIMIT_BYTES = 56 * 1024 * 1024

ROW_TILE = 256
MASK_VALUE = -1e30
NA_ROWS_PER_STEP = 4
MOE_ROWS = 1024
MOE_FF_TILE = 512
ROUTER_PAD = LANES
SCAN_STEPS = 32
RW_TILE = 128


def _params(*sem):
    return pltpu.CompilerParams(dimension_semantics=sem, vmem_limit_bytes=VMEM_LIMIT_BYTES)


def _col_chunks(width, step=512):
    out, c = [], 0
    while c < width:
        w = min(step, width - c)
        out.append((c, w))
        c += w
    return out


def _sigmoid(x):
    return 1.0 / (1.0 + jnp.exp(-x))


def _norm_mod(x, g, shift, scale):
    ms = jnp.mean(x * x, axis=-1, keepdims=True)
    h = x * lax.rsqrt(ms + RMS_EPS) * g
    return h * (1.0 + scale) + shift


def _dot_exact_rhs(x, m):
    hi = x.astype(BF16)
    r1 = x - hi.astype(F32)
    mid = r1.astype(BF16)
    lo = (r1 - mid.astype(F32)).astype(BF16)
    dot = functools.partial(jnp.dot, preferred_element_type=F32)
    return dot(hi, m) + dot(mid, m) + dot(lo, m)


def _mod_spec(n_batch, ctx_tiles):
    return pl.BlockSpec((1, 1, D_MODEL), lambda b, i: (jnp.where(i < ctx_tiles, n_batch, b), 0, 0))


def _row_spec(width, tile_offset=0):
    return pl.BlockSpec((1, ROW_TILE, width), lambda b, i: (b, i + tile_offset, 0))


def _whole(shape):
    return pl.BlockSpec(shape, lambda *_: (0,) * len(shape))


def _mod_kernel(c_ref, w_ref, b_ref, o_ref):
    c = c_ref[...]
    cs = c * _sigmoid(c)
    o_ref[0] = jnp.dot(cs, w_ref[0], precision=lax.Precision.HIGHEST,
                       preferred_element_type=F32) + b_ref[0]


def _modulation(c_all, mod_w, mod_b):
    depth, d, n = mod_w.shape
    rows = c_all.shape[0]
    tn = 768
    return pl.pallas_call(
        _mod_kernel,
        grid=(depth, n // tn),
        in_specs=[
            pl.BlockSpec((rows, d), lambda l, j: (0, 0)),
            pl.BlockSpec((1, d, tn), lambda l, j: (l, 0, j)),
            pl.BlockSpec((1, 1, tn), lambda l, j: (l, 0, j)),
        ],
        out_specs=pl.BlockSpec((1, rows, tn), lambda l, j: (l, 0, j)),
        out_shape=jax.ShapeDtypeStruct((depth, rows, n), F32),
        compiler_params=_params("arbitrary", "arbitrary"),
        name="modulation",
    )(c_all, mod_w, mod_b.reshape(depth, 1, n))


def _proj_kernel(x_ref, g_ref, sh_ref, sc_ref, w_ref, qkv_ref, rw_ref, cv_ref):
    hb = _norm_mod(x_ref[0], g_ref[...], sh_ref[0], sc_ref[0]).astype(BF16)
    for ref, base in ((qkv_ref, 0), (rw_ref, QKV_WIDTH), (cv_ref, QKV_WIDTH + RW_IN)):
        for c0, w in _col_chunks(ref.shape[-1]):
            y = jnp.dot(hb, w_ref[:, base + c0:base + c0 + w], preferred_element_type=F32)
            ref[0, :, c0:c0 + w] = y.astype(ref.dtype)


def _in_projection(x, g, shift, scale, w_in_bf16, ctx_tiles):
    b, t, d = x.shape
    ms = _mod_spec(b, ctx_tiles)
    return pl.pallas_call(
        _proj_kernel,
        grid=(b, t // ROW_TILE),
        in_specs=[_row_spec(d), _whole((1, d)), ms, ms, _whole((d, IN_WIDTH))],
        out_specs=[_row_spec(QKV_WIDTH), _row_spec(RW_IN), _row_spec(2 * CV_WIDTH)],
        out_shape=[
            jax.ShapeDtypeStruct((b, t, QKV_WIDTH), BF16),
            jax.ShapeDtypeStruct((b, t, RW_IN), F32),
            jax.ShapeDtypeStruct((b, t, 2 * CV_WIDTH), F32),
        ],
        compiler_params=_params("arbitrary", "arbitrary"),
        name="in_projection",
    )(x, g.reshape(1, d), shift, scale, w_in_bf16)


def _head_pair_queries(q):
    lane = lax.broadcasted_iota(jnp.int32, q.shape, 1)
    qs = q * jnp.asarray(HEAD_DIM ** -0.5, q.dtype)
    zero = jnp.zeros_like(qs)
    return jnp.concatenate([jnp.where(lane < HEAD_DIM, qs, zero), jnp.where(lane >= HEAD_DIM, qs, zero)], axis=0)


def _head_pair_merge(o, rows):
    lane = lax.broadcasted_iota(jnp.int32, (rows, LANES), 1)
    return jnp.where(lane < HEAD_DIM, o[:rows], o[rows:])


_NT = (((1,), (1,)), ((), ()))


def _na_kernel(q_ref, k_ref, v_ref, bias_ref, o_ref, *, rows, n_ctx):
    n_win = NA_WIN_ROWS * GRID_W
    k_ctx = k_ref[0, 0:n_ctx, :]
    v_ctx = v_ref[0, 0:n_ctx, :]
    for j in range(NA_ROWS_PER_STEP):
        i = pl.program_id(2) * NA_ROWS_PER_STEP + j
        rs = jnp.clip(i - NA_WIN_ROWS // 2, 0, rows - NA_WIN_ROWS)
        start = pl.multiple_of(n_ctx + rs * GRID_W, GRID_W)
        k_win = k_ref[0, pl.ds(start, n_win), :]
        v_win = v_ref[0, pl.ds(start, n_win), :]
        qb = _head_pair_queries(q_ref[0, j * GRID_W:(j + 1) * GRID_W, :])
        bias = bias_ref[0, rs - i + NA_WIN_ROWS - 1]
        s_win = lax.dot_general(qb, k_win, _NT, preferred_element_type=F32) + bias
        s_ctx = lax.dot_general(qb, k_ctx, _NT, preferred_element_type=F32)
        m = jnp.maximum(jnp.max(s_win, axis=-1, keepdims=True), jnp.max(s_ctx, axis=-1, keepdims=True))
        p_win = jnp.exp(s_win - m)
        p_ctx = jnp.exp(s_ctx - m)
        denom = jnp.sum(p_win, axis=-1, keepdims=True) + jnp.sum(p_ctx, axis=-1, keepdims=True)
        o = (jnp.dot(p_win.astype(BF16), v_win, preferred_element_type=F32)
             + jnp.dot(p_ctx.astype(BF16), v_ctx, preferred_element_type=F32)) / denom
        o_ref[0, j * GRID_W:(j + 1) * GRID_W, :] = _head_pair_merge(o, GRID_W).astype(o_ref.dtype)


def _na_bias_table(rpb):
    qc = np.arange(GRID_W)[:, None]
    kc = np.arange(GRID_W)[None, :]
    ws = np.clip(qc - NA_WIN_COLS // 2, 0, GRID_W - NA_WIN_COLS)
    mask = (kc >= ws) & (kc < ws + NA_WIN_COLS)
    rel = np.clip(kc - qc + NA_WIN_COLS - 1, 0, 2 * NA_WIN_COLS - 2)
    full = jnp.where(mask[None, None], rpb[:, :, rel].astype(F32), MASK_VALUE)
    dr = np.arange(NA_WIN_ROWS)[:, None] + np.arange(NA_WIN_ROWS)[None, :]
    t = full[:, dr]
    t = t.transpose(0, 1, 3, 2, 4).reshape(NA_HEADS // 2, 2, NA_WIN_ROWS, GRID_W, NA_WIN_ROWS * GRID_W)
    return t.transpose(0, 2, 1, 3, 4).reshape(NA_HEADS // 2, NA_WIN_ROWS, 2 * GRID_W, NA_WIN_ROWS * GRID_W)


def _neighbourhood_attention(qkv, bias_table, n_ctx):
    b, t, _ = qkv.shape
    n = t - n_ctx
    rows = n // GRID_W
    hp = NA_HEADS // 2
    koff, voff = NA_WIDTH // LANES, 2 * NA_WIDTH // LANES
    qrows = NA_ROWS_PER_STEP * GRID_W
    assert n_ctx % qrows == 0 and rows % NA_ROWS_PER_STEP == 0
    q_off = n_ctx // qrows
    return pl.pallas_call(
        functools.partial(_na_kernel, rows=rows, n_ctx=n_ctx),
        grid=(b, hp, rows // NA_ROWS_PER_STEP),
        in_specs=[
            pl.BlockSpec((1, qrows, LANES), lambda bi, h, i: (bi, i + q_off, h)),
            pl.BlockSpec((1, t, LANES), lambda bi, h, i: (bi, 0, koff + h)),
            pl.BlockSpec((1, t, LANES), lambda bi, h, i: (bi, 0, voff + h)),
            pl.BlockSpec((1, NA_WIN_ROWS, 2 * GRID_W, NA_WIN_ROWS * GRID_W), lambda bi, h, i: (h, 0, 0, 0)),
        ],
        out_specs=pl.BlockSpec((1, qrows, LANES), lambda bi, h, i: (bi, i, h)),
        out_shape=jax.ShapeDtypeStruct((b, n, NA_WIDTH), BF16),
        compiler_params=_params("arbitrary", "arbitrary", "arbitrary"),
        name="neighbourhood_attention",
    )(qkv, qkv, qkv, bias_table)


def _ctx_attn_kernel(q_ref, k_ref, v_ref, o_ref):
    l = q_ref.shape[1]
    qb = _head_pair_queries(q_ref[0])
    s = lax.dot_general(qb, k_ref[0], _NT, preferred_element_type=F32)
    p = jnp.exp(s - jnp.max(s, axis=-1, keepdims=True))
    o = jnp.dot(p.astype(BF16), v_ref[0], preferred_element_type=F32) / jnp.sum(p, axis=-1, keepdims=True)
    o_ref[0] = _head_pair_merge(o, l).astype(o_ref.dtype)


def _context_attention(qkv, n_ctx):
    b = qkv.shape[0]
    hp = NA_HEADS // 2
    koff, voff = NA_WIDTH // LANES, 2 * NA_WIDTH // LANES
    return pl.pallas_call(
        _ctx_attn_kernel,
        grid=(b, hp),
        in_specs=[
            pl.BlockSpec((1, n_ctx, LANES), lambda bi, h: (bi, 0, h)),
            pl.BlockSpec((1, n_ctx, LANES), lambda bi, h: (bi, 0, koff + h)),
            pl.BlockSpec((1, n_ctx, LANES), lambda bi, h: (bi, 0, voff + h)),
        ],
        out_specs=pl.BlockSpec((1, n_ctx, LANES), lambda bi, h: (bi, 0, h)),
        out_shape=jax.ShapeDtypeStruct((b, n_ctx, NA_WIDTH), BF16),
        compiler_params=_params("arbitrary", "arbitrary"),
        name="context_attention",
    )(qkv, qkv, qkv)


SCAN_GROUPS = ("r", "v", "z", "w0", "k0", "b0", "w1", "k1", "b1")
POST_COLS = 4 * RW_WIDTH


def _softplus(x):
    return jnp.maximum(x, 0.0) + jnp.log(1.0 + jnp.exp(-jnp.abs(x)))


def _rw_prep_kernel(u_ref, up_ref, un_ref, mup_ref, mun_ref, ones_ref, kk_ref, ka_ref, rk_ref,
                    w0_ref, a0_ref, wa_ref, g2_ref, scan_ref, post_ref, *, n_ctx):
    i = pl.program_id(0)
    b = pl.program_id(1)
    tt = u_ref.shape[1]
    n_tok = pl.num_programs(0) * tt
    u = u_ref[0]
    row = lax.broadcasted_iota(jnp.int32, u.shape, 0)
    tok = row + i * tt
    prev = jnp.where(row == 0, up_ref[0, SUBLANES - 1:SUBLANES, :], pltpu.roll(u, 1, 0))
    prev = jnp.where(tok == 0, 0.0, jnp.where(tok == n_ctx, 0.0, prev))
    nxt = jnp.where(row == tt - 1, un_ref[0, 0:1, :], pltpu.roll(u, tt - 1, 0))
    nxt = jnp.where(tok == n_tok - 1, 0.0, jnp.where(tok == n_ctx - 1, 0.0, nxt))
    us = u + mup_ref[...] * (prev - u) + mun_ref[...] * (nxt - u)

    def put(group, val):
        g = SCAN_GROUPS.index(group)
        for c in range(2):
            scan_ref[2 * g + c, pl.ds(b, tt, stride=SUBLANES), :] = val[:, c * LANES:(c + 1) * LANES]

    ones = ones_ref[...]
    r = us[:, 0:RW_WIDTH]
    k = us[:, RW_WIDTH:2 * RW_WIDTH]
    v = us[:, 2 * RW_WIDTH:3 * RW_WIDTH]
    kk = k * kk_ref[...]
    kk = kk * lax.rsqrt(_dot_exact_rhs(kk * kk, ones) + 1e-12)
    put("r", r)
    put("v", v)
    put("z", -kk)
    for d in range(2):
        lo = 3 * RW_WIDTH + d * RW_LORA
        wa_in = us[:, lo:lo + LANES]
        lane = lax.broadcasted_iota(jnp.int32, wa_in.shape, 1)
        wa_in = jnp.where(lane < RW_DECAY_RANK, jnp.tanh(wa_in), wa_in)
        wa = jnp.dot(wa_in.astype(BF16), wa_ref[d], preferred_element_type=F32)
        log_w = -_softplus(-(w0_ref[d] + wa[:, 0:RW_WIDTH])) - 0.5
        decay = jnp.exp(-jnp.exp(log_w))
        a = _sigmoid(a0_ref[d] + wa[:, RW_WIDTH:2 * RW_WIDTH])
        gate_in = _sigmoid(us[:, lo + LANES:lo + 2 * LANES])
        g = jnp.dot(gate_in.astype(BF16), g2_ref[d], preferred_element_type=F32)
        kd = k * (1.0 + (a - 1.0) * ka_ref[...])
        bonus = _dot_exact_rhs(r * kd * rk_ref[...], ones) * v
        put("w%d" % d, decay)
        put("k%d" % d, kd)
        put("b%d" % d, kk * a)
        post_ref[0, :, 2 * d * RW_WIDTH:(2 * d + 1) * RW_WIDTH] = g
        post_ref[0, :, (2 * d + 1) * RW_WIDTH:(2 * d + 2) * RW_WIDTH] = bonus


def _head_ones():
    h = np.arange(RW_WIDTH) // HEAD_DIM
    return jnp.asarray(h[:, None] == h[None, :], BF16)


def _rw_prep(u, p, n_ctx):
    b, t, _ = u.shape
    assert b == SUBLANES
    tt = RW_TILE
    nt8 = t // SUBLANES
    row = lambda vec, n: vec.reshape(1, n)
    return pl.pallas_call(
        functools.partial(_rw_prep_kernel, n_ctx=n_ctx),
        grid=(t // tt, b),
        in_specs=[
            pl.BlockSpec((1, tt, RW_IN), lambda i, bi: (bi, i, 0)),
            pl.BlockSpec((1, SUBLANES, RW_IN), lambda i, bi: (bi, jnp.maximum(i * (tt // SUBLANES) - 1, 0), 0)),
            pl.BlockSpec((1, SUBLANES, RW_IN), lambda i, bi: (bi, jnp.minimum((i + 1) * (tt // SUBLANES), nt8 - 1), 0)),
            _whole((1, RW_IN)), _whole((1, RW_IN)),
            _whole((RW_WIDTH, RW_WIDTH)),
            _whole((1, RW_WIDTH)), _whole((1, RW_WIDTH)), _whole((1, RW_WIDTH)),
            _whole((2, 1, RW_WIDTH)), _whole((2, 1, RW_WIDTH)),
            _whole((2, LANES, 2 * RW_WIDTH)), _whole((2, RW_GATE_RANK, RW_WIDTH)),
        ],
        out_specs=[
            pl.BlockSpec((2 * len(SCAN_GROUPS), tt * b, LANES), lambda i, bi: (0, i, 0)),
            pl.BlockSpec((1, tt, POST_COLS), lambda i, bi: (bi, i, 0)),
        ],
        out_shape=[
            jax.ShapeDtypeStruct((2 * len(SCAN_GROUPS), t * b, LANES), F32),
            jax.ShapeDtypeStruct((b, t, POST_COLS), F32),
        ],
        compiler_params=_params("arbitrary", "arbitrary"),
        name="rwkv_prep",
    )(u, u, u, row(p["mu_prev"], RW_IN), row(p["mu_next"], RW_IN), _head_ones(),
      row(p["k_k"], RW_WIDTH), row(p["k_a"], RW_WIDTH), row(p["r_k"], RW_WIDTH),
      p["w0"].reshape(2, 1, RW_WIDTH), p["a0"].reshape(2, 1, RW_WIDTH), p["wa"], p["g2"])


def _rw_lora_weights(w2, a2):
    z = jnp.zeros_like(w2)
    top = jnp.concatenate([w2, z], axis=-1)
    bot = jnp.concatenate([z, a2], axis=-1)
    return jnp.concatenate([top, bot], axis=1).astype(BF16)


_QUARTER = LANES // 4
_VROWS = HEAD_DIM // 4


def _chain_tile(f_ref, b_ref, sf, sb):
    pieces = [f_ref[0, pl.ds(sf, SUBLANES), :], f_ref[1, pl.ds(sf, SUBLANES), :],
              b_ref[0, pl.ds(sb, SUBLANES), :], b_ref[1, pl.ds(sb, SUBLANES), :]]
    return jnp.concatenate(pieces * 4, axis=0).T


def _scan_kernel(rf, vf, zf, wf, kf, bf, rb, vb, zb, wb, kb, bb, yf_ref, yb_ref, s_ref, t_ref, v_ref, y_ref):
    steps = rf.shape[1] // SUBLANES
    zero = jnp.zeros((_VROWS, LANES), F32)

    @pl.when(pl.program_id(0) == 0)
    def _():
        s_ref[...] = jnp.zeros_like(s_ref)
        y_ref[...] = jnp.zeros_like(y_ref)

    lane_q = lax.broadcasted_iota(jnp.int32, (_VROWS, LANES), 1) // _QUARTER
    col_q = (lax.broadcasted_iota(jnp.int32, (_QUARTER, LANES), 1) % HEAD_DIM) // _VROWS

    def offsets(s):
        s = jnp.clip(s, 0, steps - 1)
        return pl.multiple_of(s * SUBLANES, SUBLANES), pl.multiple_of((steps - 1 - s) * SUBLANES, SUBLANES)

    def prepare(s, p):
        sf, sb = offsets(s)
        zf_off, zb_off = offsets(s + 1)
        t_ref[p, 0] = _chain_tile(zf, zb, zf_off, zb_off)
        for n, (f, b) in enumerate(((wf, wb), (bf, bb), (kf, kb), (rf, rb))):
            t_ref[p, n + 1] = _chain_tile(f, b, sf, sb)
        vt = _chain_tile(vf, vb, sf, sb)
        for h in range(2):
            v = zero
            for q in range(4):
                lo = h * HEAD_DIM + q * _VROWS
                v = jnp.where(lane_q == q, vt[lo:lo + _VROWS], v)
            v_ref[p, h] = v

    def flush(s, p):
        sf, sb = offsets(s)
        yt = jnp.concatenate([y_ref[p, 0]] * 4 + [y_ref[p, 1]] * 4, axis=0).T
        nat = jnp.zeros((_QUARTER, LANES), F32)
        for q in range(4):
            nat = jnp.where(col_q == q, yt[q * _QUARTER:(q + 1) * _QUARTER], nat)
        yf_ref[0, pl.ds(sf, SUBLANES), :] = nat[0:8]
        yf_ref[1, pl.ds(sf, SUBLANES), :] = nat[8:16]
        yb_ref[0, pl.ds(sb, SUBLANES), :] = nat[16:24]
        yb_ref[1, pl.ds(sb, SUBLANES), :] = nat[24:32]

    def update(p, sz):
        sz_next = []
        for h in range(2):
            base = h * HEAD_DIM
            v = v_ref[p, h]
            y = [zero, zero]
            zn = [zero, zero]
            for k in range(HEAD_DIM):
                row = pl.ds(base + k, 1)
                st = s_ref[base + k] * t_ref[p, 1, row, :] + sz[h] * t_ref[p, 2, row, :] + v * t_ref[p, 3, row, :]
                s_ref[base + k] = st
                y[k % 2] = y[k % 2] + st * t_ref[p, 4, row, :]
                zn[k % 2] = zn[k % 2] + st * t_ref[p, 0, row, :]
            y_ref[p, h] = y[0] + y[1]
            sz_next.append(zn[0] + zn[1])
        return tuple(sz_next)

    f0, b0 = offsets(0)
    t_ref[1, 0] = _chain_tile(zf, zb, f0, b0)
    sz0 = []
    for h in range(2):
        acc = [zero, zero]
        for k in range(HEAD_DIM):
            r = h * HEAD_DIM + k
            acc[k % 2] = acc[k % 2] + s_ref[r] * t_ref[1, 0, pl.ds(r, 1), :]
        sz0.append(acc[0] + acc[1])
    prepare(0, 0)

    def pair(j, sz):
        s = 2 * j
        for p in range(2):
            prepare(s + p + 1, 1 - p)
            sz = update(p, sz)
            flush(s + p - 1, 1 - p)
        return sz

    lax.fori_loop(0, steps // 2, pair, tuple(sz0))
    flush(steps - 1, 1)


def _rwkv7_scan(ops, n_batch, n_ctx):
    rows = ops.shape[1]
    blk = SCAN_STEPS * n_batch
    nb = rows // blk
    nb_c = n_ctx * n_batch // blk

    def mirror(i):
        return jnp.where(i < nb_c, nb_c - 1 - i, nb_c + nb - 1 - i)

    def spec(group, backward):
        g = SCAN_GROUPS.index(group)
        if backward:
            return pl.BlockSpec((2, blk, LANES), lambda i: (g, mirror(i), 0))
        return pl.BlockSpec((2, blk, LANES), lambda i: (g, i, 0))

    fw = [spec(g, False) for g in ("r", "v", "z", "w0", "k0", "b0")]
    bw = [spec(g, True) for g in ("r", "v", "z", "w1", "k1", "b1")]
    out_sds = jax.ShapeDtypeStruct((2, rows, LANES), F32)
    return pl.pallas_call(
        _scan_kernel,
        grid=(nb,),
        in_specs=fw + bw,
        out_specs=[pl.BlockSpec((2, blk, LANES), lambda i: (0, i, 0)),
                   pl.BlockSpec((2, blk, LANES), lambda i: (0, mirror(i), 0))],
        out_shape=[out_sds, out_sds],
        scratch_shapes=[pltpu.VMEM((2 * HEAD_DIM, _VROWS, LANES), F32), pltpu.VMEM((2, 5, LANES, LANES), F32),
                        pltpu.VMEM((2, 2, _VROWS, LANES), F32), pltpu.VMEM((2, 2, _VROWS, LANES), F32)],
        compiler_params=_params("arbitrary"),
        name="rwkv_scan",
    )(*([ops] * 12))


def _rw_post_kernel(yf_ref, yb_ref, post_ref, ones_ref, gg_ref, gb_ref, o_ref):
    b = pl.program_id(1)
    tt = o_ref.shape[1]
    ones = ones_ref[...]
    out = None
    for d, y_ref in enumerate((yf_ref, yb_ref)):
        y = jnp.concatenate([y_ref[0, pl.ds(b, tt, stride=SUBLANES), :],
                             y_ref[1, pl.ds(b, tt, stride=SUBLANES), :]], axis=1)
        mu = _dot_exact_rhs(y, ones) * (1.0 / HEAD_DIM)
        yc = y - mu
        var = _dot_exact_rhs(yc * yc, ones) * (1.0 / HEAD_DIM)
        yn = yc * lax.rsqrt(var + RW_GN_EPS) * gg_ref[...] + gb_ref[...]
        g = post_ref[0, :, 2 * d * RW_WIDTH:(2 * d + 1) * RW_WIDTH]
        bonus = post_ref[0, :, (2 * d + 1) * RW_WIDTH:(2 * d + 2) * RW_WIDTH]
        term = (yn + bonus) * g
        out = term if out is None else out + term
    o_ref[0] = out.astype(o_ref.dtype)


def _rw_post(y_f, y_b, post, gn_g, gn_b):
    b, t, _ = post.shape
    tt = RW_TILE
    yspec = pl.BlockSpec((2, tt * b, LANES), lambda i, bi: (0, i, 0))
    return pl.pallas_call(
        _rw_post_kernel,
        grid=(t // tt, b),
        in_specs=[yspec, yspec, pl.BlockSpec((1, tt, POST_COLS), lambda i, bi: (bi, i, 0)),
                  _whole((RW_WIDTH, RW_WIDTH)), _whole((1, RW_WIDTH)), _whole((1, RW_WIDTH))],
        out_specs=pl.BlockSpec((1, tt, RW_WIDTH), lambda i, bi: (bi, i, 0)),
        out_shape=jax.ShapeDtypeStruct((b, t, RW_WIDTH), BF16),
        compiler_params=_params("arbitrary", "arbitrary"),
        name="rwkv_post",
    )(y_f, y_b, post, _head_ones(), gn_g.reshape(1, RW_WIDTH), gn_b.reshape(1, RW_WIDTH))


def _bi_rwkv7(u, p, n_ctx):
    ops, post = _rw_prep(u, p, n_ctx)
    y_f, y_b = _rwkv7_scan(ops, u.shape[0], n_ctx)
    return _rw_post(y_f, y_b, post, p["gn_g"], p["gn_b"])


CONV_PAD = 16


def _conv_kernel(u_ref, w_ref, b_ref, lg_ref, lb_ref, o_ref, pad_ref, *, n_ctx):
    t = u_ref.shape[1]
    half = CV_CONV_LEN // 2
    chunk = ROW_TILE
    u = u_ref[0]
    h = u[:, :CV_WIDTH] * _sigmoid(u[:, CV_WIDTH:])
    zeros = jnp.zeros((CONV_PAD, CV_WIDTH), F32)
    lat0 = 2 * CONV_PAD + n_ctx
    pad_ref[0:CONV_PAD, :] = zeros
    pad_ref[CONV_PAD:CONV_PAD + n_ctx, :] = h[:n_ctx]
    pad_ref[CONV_PAD + n_ctx:lat0, :] = zeros
    pad_ref[lat0:lat0 + t - n_ctx, :] = h[n_ctx:]
    pad_ref[lat0 + t - n_ctx:lat0 + t - n_ctx + CONV_PAD, :] = zeros
    w = w_ref[...]

    def body(c, carry):
        out0 = pl.multiple_of(c * chunk, chunk)
        base = pl.multiple_of(out0 + jnp.where(c >= n_ctx // chunk, CONV_PAD, 0), CONV_PAD)
        win = pad_ref[pl.ds(base, chunk + 2 * CONV_PAD), :]
        acc = jnp.zeros((chunk, CV_WIDTH), F32)
        for j in range(CV_CONV_LEN):
            off = CONV_PAD - half + j
            acc = acc + win[off:off + chunk, :] * w[j:j + 1, :]
        hh = acc + b_ref[...]
        mu = jnp.mean(hh, axis=-1, keepdims=True)
        hc = hh - mu
        var = jnp.mean(hc * hc, axis=-1, keepdims=True)
        y = hc * lax.rsqrt(var + LN_EPS) * lg_ref[...] + lb_ref[...]
        o_ref[0, pl.ds(out0, chunk), :] = (y * _sigmoid(y)).astype(o_ref.dtype)
        return carry

    lax.fori_loop(0, t // chunk, body, 0)


def _conformer_conv(u, dw_w, dw_b, ln_g, ln_b, n_ctx):
    b, t, _ = u.shape
    assert n_ctx % ROW_TILE == 0 and t % ROW_TILE == 0
    row = lambda vec: vec.reshape(1, CV_WIDTH)
    return pl.pallas_call(
        functools.partial(_conv_kernel, n_ctx=n_ctx),
        grid=(b,),
        in_specs=[pl.BlockSpec((1, t, 2 * CV_WIDTH), lambda bi: (bi, 0, 0)),
                  _whole((CV_CONV_LEN, CV_WIDTH)), _whole((1, CV_WIDTH)), _whole((1, CV_WIDTH)),
                  _whole((1, CV_WIDTH))],
        out_specs=pl.BlockSpec((1, t, CV_WIDTH), lambda bi: (bi, 0, 0)),
        out_shape=jax.ShapeDtypeStruct((b, t, CV_WIDTH), BF16),
        scratch_shapes=[pltpu.VMEM((t + 3 * CONV_PAD, CV_WIDTH), F32)],
        compiler_params=_params("arbitrary"),
        name="conformer_conv",
    )(u, dw_w, row(dw_b), row(ln_g), row(ln_b))


def _wout_kernel(x_ref, nal_ref, nac_ref, rw_ref, cv_ref, w_ref, gate_ref, o_ref, *, ctx_tiles):
    dot = functools.partial(jnp.dot, preferred_element_type=F32)
    na = nal_ref[0]
    if ctx_tiles:
        na = jnp.where(pl.program_id(1) < ctx_tiles, nac_ref[0], na)
    acc = (dot(na, w_ref[0:NA_WIDTH, :])
           + dot(rw_ref[0], w_ref[NA_WIDTH:NA_WIDTH + RW_WIDTH, :])
           + dot(cv_ref[0], w_ref[NA_WIDTH + RW_WIDTH:, :]))
    o_ref[0] = x_ref[0] + gate_ref[0] * acc


def _out_projection(x, na_l, na_c, rw, cv, w_out_bf16, gate, n_ctx, with_ctx):
    b, t, d = x.shape
    ctx_tiles = n_ctx // ROW_TILE
    assert ctx_tiles == 1 or not with_ctx
    off = 0 if with_ctx else ctx_tiles
    n_rows = t if with_ctx else t - n_ctx
    if with_ctx:
        nal_spec = pl.BlockSpec((1, ROW_TILE, NA_WIDTH), lambda bi, i: (bi, jnp.maximum(i - ctx_tiles, 0), 0))
    else:
        nal_spec = _row_spec(NA_WIDTH)
        na_c = na_l
    return pl.pallas_call(
        functools.partial(_wout_kernel, ctx_tiles=ctx_tiles if with_ctx else 0),
        grid=(b, n_rows // ROW_TILE),
        in_specs=[_row_spec(d, off), nal_spec,
                  pl.BlockSpec((1, ROW_TILE, NA_WIDTH), lambda bi, i: (bi, 0, 0)),
                  _row_spec(RW_WIDTH, off), _row_spec(CV_WIDTH, off),
                  _whole((d, d)), _mod_spec(b, ctx_tiles if with_ctx else 0)],
        out_specs=_row_spec(d),
        out_shape=jax.ShapeDtypeStruct((b, n_rows, d), F32),
        compiler_params=_params("arbitrary", "arbitrary"),
        name="out_projection",
    )(x, na_l, na_c, rw, cv, w_out_bf16, gate)


def _ffn_kernel(x_ref, g_ref, sh_ref, sc_ref, gate_ref, wg_ref, wu_ref, wd_ref, o_ref, acc_ref):
    x = x_ref[0]
    hb = _norm_mod(x, g_ref[...], sh_ref[0], sc_ref[0]).astype(BF16)
    dot = functools.partial(jnp.dot, preferred_element_type=F32)
    for n, (c0, w) in enumerate(_col_chunks(wg_ref.shape[1])):
        g = dot(hb, wg_ref[:, c0:c0 + w])
        u = dot(hb, wu_ref[:, c0:c0 + w])
        a = (g * _sigmoid(g) * u).astype(BF16)
        y = dot(a, wd_ref[c0:c0 + w, :])
        if n == 0:
            acc_ref[...] = y
        else:
            acc_ref[...] += y
    o_ref[0] = x + gate_ref[0] * acc_ref[...]


def _dense_ffn(x, g, shift, scale, gate, wg, wu, wd, ctx_tiles):
    b, t, d = x.shape
    dff = wg.shape[1]
    ms = _mod_spec(b, ctx_tiles)
    return pl.pallas_call(
        _ffn_kernel,
        grid=(b, t // ROW_TILE),
        in_specs=[_row_spec(d), _whole((1, d)), ms, ms, ms,
                  _whole((d, dff)), _whole((d, dff)), _whole((dff, d))],
        out_specs=_row_spec(d),
        out_shape=jax.ShapeDtypeStruct((b, t, d), F32),
        scratch_shapes=[pltpu.VMEM((ROW_TILE, d), F32)],
        compiler_params=_params("arbitrary", "arbitrary"),
        name="dense_ffn",
    )(x, g.reshape(1, d), shift, scale, gate, wg, wu, wd)


def _router_kernel(x_ref, g_ref, sh_ref, sc_ref, wr_ref, h_ref, logit_ref):
    h = _norm_mod(x_ref[0], g_ref[...], sh_ref[0], sc_ref[0])
    h_ref[0] = h
    logit_ref[0] = jnp.dot(h, wr_ref[...], precision=lax.Precision.HIGHEST, preferred_element_type=F32)


def _moe_router(x, g, shift, scale, router_pad):
    b, t, d = x.shape
    ms = _mod_spec(b, 0)
    return pl.pallas_call(
        _router_kernel,
        grid=(b, t // ROW_TILE),
        in_specs=[_row_spec(d), _whole((1, d)), ms, ms, _whole((d, ROUTER_PAD))],
        out_specs=[_row_spec(d), _row_spec(ROUTER_PAD)],
        out_shape=[jax.ShapeDtypeStruct((b, t, d), F32), jax.ShapeDtypeStruct((b, t, ROUTER_PAD), F32)],
        compiler_params=_params("arbitrary", "arbitrary"),
        name="moe_router",
    )(x, g.reshape(1, d), shift, scale, router_pad)


def _row_copy(src_hbm, dst_vmem, sem, src_row, dst_row):
    return pltpu.make_async_copy(src_hbm.at[pl.ds(src_row, 1)], dst_vmem.at[pl.ds(dst_row, 1)], sem)


GATHER_UNROLL = 8


def _start_row_gather(src_hbm, dst_vmem, sem, index_of_row, rows):
    def body(r, c):
        _row_copy(src_hbm, dst_vmem, sem, index_of_row(r), r).start()
        return c

    lax.fori_loop(0, rows, body, 0, unroll=GATHER_UNROLL)


def _wait_row_gather(src_hbm, dst_vmem, sem, rows):
    def body(r, c):
        _row_copy(src_hbm, dst_vmem, sem, 0, r).wait()
        return c

    lax.fori_loop(0, rows, body, 0, unroll=GATHER_UNROLL)


def _expert_kernel(be_ref, nused_ref, tok_ref, tok_next_ref, h_hbm, wg_ref, wu_ref, wd_ref, o_ref,
                   xg_ref, xb_ref, acc_ref, sem):
    i = pl.program_id(0)
    j = pl.program_id(1)
    rows = xb_ref.shape[0]
    n_used = nused_ref[0]
    used = i < n_used
    slot = i % 2

    @pl.when(jnp.logical_and(i == 0, j == 0))
    def _():
        _start_row_gather(h_hbm, xg_ref.at[0], sem.at[0], lambda r: tok_ref[0, 0, r], rows)

    @pl.when(jnp.logical_and(used, j == 0))
    def _():
        _wait_row_gather(h_hbm, xg_ref.at[slot], sem.at[slot], rows)
        xb_ref[...] = xg_ref[slot].astype(BF16)

    @pl.when(jnp.logical_and(i + 1 < n_used, j == 1))
    def _():
        _start_row_gather(h_hbm, xg_ref.at[1 - slot], sem.at[1 - slot], lambda r: tok_next_ref[0, 0, r], rows)

    @pl.when(used)
    def _():
        dot = functools.partial(jnp.dot, preferred_element_type=F32)
        xb = xb_ref[...]
        g = dot(xb, wg_ref[0])
        u = dot(xb, wu_ref[0])
        a = (g * _sigmoid(g) * u).astype(BF16)
        y = dot(a, wd_ref[0])

        @pl.when(j == 0)
        def _():
            acc_ref[...] = y

        @pl.when(j > 0)
        def _():
            acc_ref[...] += y

    last = j == pl.num_programs(1) - 1

    @pl.when(jnp.logical_and(used, last))
    def _():
        o_ref[...] = acc_ref[...]

    @pl.when(jnp.logical_and(jnp.logical_not(used), last))
    def _():
        o_ref[...] = jnp.zeros_like(o_ref)


def _moe_experts(h_flat, slot_tok, block_e, n_used, wg, wu, wd):
    nb = slot_tok.shape[0]
    d = h_flat.shape[1]
    dff = wg.shape[2]
    nj = dff // MOE_FF_TILE
    assert nj >= 2
    grid_spec = pltpu.PrefetchScalarGridSpec(
        num_scalar_prefetch=2,
        grid=(nb, nj),
        in_specs=[
            pl.BlockSpec((1, 1, MOE_ROWS), lambda i, j, be, nu: (i, 0, 0), memory_space=pltpu.SMEM),
            pl.BlockSpec((1, 1, MOE_ROWS), lambda i, j, be, nu: (jnp.minimum(i + 1, nb - 1), 0, 0),
                         memory_space=pltpu.SMEM),
            pl.BlockSpec(memory_space=pl.ANY),
            pl.BlockSpec((1, d, MOE_FF_TILE), lambda i, j, be, nu: (be[i], 0, j)),
            pl.BlockSpec((1, d, MOE_FF_TILE), lambda i, j, be, nu: (be[i], 0, j)),
            pl.BlockSpec((1, MOE_FF_TILE, d), lambda i, j, be, nu: (be[i], j, 0)),
        ],
        out_specs=pl.BlockSpec((MOE_ROWS, d), lambda i, j, be, nu: (i, 0)),
        scratch_shapes=[pltpu.VMEM((2, MOE_ROWS, d), F32), pltpu.VMEM((MOE_ROWS, d), BF16),
                        pltpu.VMEM((MOE_ROWS, d), F32), pltpu.SemaphoreType.DMA((2,))],
    )
    return pl.pallas_call(
        _expert_kernel,
        grid_spec=grid_spec,
        out_shape=jax.ShapeDtypeStruct((nb * MOE_ROWS, d), F32),
        compiler_params=_params("arbitrary", "arbitrary"),
        name="moe_experts",
    )(block_e, n_used, slot_tok, slot_tok, h_flat, wg, wu, wd)


def _combine_kernel(dest_ref, dest_next_ref, x_ref, gates_ref, gate2_ref, fg_ref, yb_hbm, o_ref, y_ref, sem):
    i = pl.program_id(0)
    rows = x_ref.shape[1]
    slot = i % 2

    @pl.when(i == 0)
    def _():
        _start_row_gather(yb_hbm, y_ref.at[0], sem.at[0], lambda r: dest_ref[0, 0, r], TOP_K * rows)

    @pl.when(i + 1 < pl.num_programs(0))
    def _():
        _start_row_gather(yb_hbm, y_ref.at[1 - slot], sem.at[1 - slot], lambda r: dest_next_ref[0, 0, r],
                          TOP_K * rows)

    _wait_row_gather(yb_hbm, y_ref.at[slot], sem.at[slot], TOP_K * rows)
    gates = gates_ref[0]
    y = y_ref[slot, 0:rows, :] * gates[:, 0:1] + y_ref[slot, rows:2 * rows, :] * gates[:, 1:2]
    x = x_ref[0] + gate2_ref[0] * y
    ms = jnp.mean(x * x, axis=-1, keepdims=True)
    o_ref[0] = x * lax.rsqrt(ms + RMS_EPS) * fg_ref[...]


def _moe_combine_final(x, gates, dest, yb, gate2, final_g):
    nt, tm, d = x.shape
    tiles_per_sample = nt // gate2.shape[0]
    tile = lambda w: pl.BlockSpec((1, tm, w), lambda i: (i, 0, 0))
    return pl.pallas_call(
        _combine_kernel,
        grid=(nt,),
        in_specs=[
            pl.BlockSpec((1, 1, TOP_K * tm), lambda i: (i, 0, 0), memory_space=pltpu.SMEM),
            pl.BlockSpec((1, 1, TOP_K * tm), lambda i: (jnp.minimum(i + 1, nt - 1), 0, 0),
                         memory_space=pltpu.SMEM),
            tile(d), tile(TOP_K),
            pl.BlockSpec((1, 1, d), lambda i: (i // tiles_per_sample, 0, 0)),
            pl.BlockSpec((1, d), lambda i: (0, 0)),
            pl.BlockSpec(memory_space=pl.ANY),
        ],
        out_specs=tile(d),
        out_shape=jax.ShapeDtypeStruct((nt, tm, d), F32),
        scratch_shapes=[pltpu.VMEM((2, TOP_K * tm, d), F32), pltpu.SemaphoreType.DMA((2,))],
        compiler_params=_params("arbitrary"),
        name="moe_combine_final",
    )(dest, dest, x, gates, gate2, final_g.reshape(1, d), yb)


def _moe_layer_final(x, g, shift, scale, gate2, router, wg, wu, wd, final_g):
    b, t, d = x.shape
    n_tok = b * t
    n_asg = n_tok * TOP_K
    router_pad = jnp.pad(router, ((0, 0), (0, ROUTER_PAD - N_EXPERTS)))
    h, logits = _moe_router(x, g, shift, scale, router_pad)
    logits = logits.reshape(n_tok, ROUTER_PAD)[:, :N_EXPERTS]
    top_logit, top_e = lax.top_k(logits, TOP_K)
    gates = jax.nn.softmax(top_logit, axis=-1)
    flat_e = top_e.reshape(-1)
    onehot = (flat_e[:, None] == jnp.arange(N_EXPERTS, dtype=flat_e.dtype)[None, :]).astype(jnp.int32)
    rank = jnp.cumsum(onehot, axis=0) - onehot
    counts = jnp.sum(onehot, axis=0)
    padded = (counts + MOE_ROWS - 1) // MOE_ROWS * MOE_ROWS
    pad_end = jnp.cumsum(padded)
    pad_start = pad_end - padded
    dest = (pad_start[flat_e] + jnp.sum(rank * onehot, axis=1)).astype(jnp.int32)
    n_blocks = (n_asg + MOE_ROWS - 1) // MOE_ROWS + N_EXPERTS
    flat_tok = jnp.arange(n_asg, dtype=jnp.int32) // TOP_K
    slot_tok = jnp.zeros((n_blocks * MOE_ROWS,), jnp.int32).at[dest].set(flat_tok)
    block_e = jnp.minimum(jnp.searchsorted(pad_end, jnp.arange(n_blocks) * MOE_ROWS, side="right"),
                          N_EXPERTS - 1).astype(jnp.int32)
    n_used = (pad_end[-1:] // MOE_ROWS).astype(jnp.int32)
    yb = _moe_experts(h.reshape(n_tok, d), slot_tok.reshape(n_blocks, 1, MOE_ROWS), block_e, n_used, wg, wu, wd)
    nt = n_tok // ROW_TILE
    tile_dest = dest.reshape(nt, ROW_TILE, TOP_K).transpose(0, 2, 1).reshape(nt, 1, TOP_K * ROW_TILE)
    out = _moe_combine_final(x.reshape(nt, ROW_TILE, d), gates.reshape(nt, ROW_TILE, TOP_K), tile_dest, yb,
                             gate2[:b], final_g)
    return out.reshape(b, t, d)


def _final_norm_kernel(x_ref, g_ref, o_ref):
    x = x_ref[0]
    ms = jnp.mean(x * x, axis=-1, keepdims=True)
    o_ref[0] = x * lax.rsqrt(ms + RMS_EPS) * g_ref[...]


def _final_norm(x, g, tile_offset):
    b, t, d = x.shape
    n_rows = t - tile_offset * ROW_TILE
    return pl.pallas_call(
        _final_norm_kernel,
        grid=(b, n_rows // ROW_TILE),
        in_specs=[_row_spec(d, tile_offset), _whole((1, d))],
        out_specs=_row_spec(d),
        out_shape=jax.ShapeDtypeStruct((b, n_rows, d), F32),
        compiler_params=_params("arbitrary", "arbitrary"),
        name="final_norm",
    )(x, g.reshape(1, d))


def kernel(x, c, ctx, c_ctx, norm1_g, norm2_g, mod_w, mod_b, w_in, w_out, na_rpb, rw_mu_prev, rw_mu_next, rw_w0, rw_w2, rw_a0, rw_a2, rw_g2, rw_k_k, rw_k_a, rw_r_k, rw_gn_g, rw_gn_b, cv_dw_w, cv_dw_b, cv_ln_g, cv_ln_b, ffn_w_gate, ffn_w_up, ffn_w_down, moe_router, moe_w_gate, moe_w_up, moe_w_down, final_g):
    b, n, _ = x.shape
    n_ctx = ctx.shape[1]
    depth = mod_w.shape[0]
    assert n_ctx == ROW_TILE and n % ROW_TILE == 0 and b == SUBLANES
    ctx_tiles = n_ctx // ROW_TILE
    c_rows = 2 * SUBLANES
    c_all = jnp.concatenate([c, c_ctx[None, :], jnp.zeros((c_rows - b - 1, D_MODEL), F32)], axis=0)
    mod = _modulation(c_all, mod_w, mod_b)

    xa = jnp.concatenate([ctx, x], axis=1)
    out = None
    for layer in range(depth):
        last = layer == depth - 1
        m = mod[layer, :b + 1].reshape(b + 1, 6, 1, D_MODEL)
        sh1, sc1, g1, sh2, sc2, g2 = (m[:, k] for k in range(6))
        w_in_b = w_in[layer].astype(BF16)
        w_out_b = w_out[layer].astype(BF16)
        qkv, rw_in, cv_in = _in_projection(xa, norm1_g[layer], sh1, sc1, w_in_b, ctx_tiles)
        na_l = _neighbourhood_attention(qkv, _na_bias_table(na_rpb[layer]), n_ctx)
        rw_p = dict(mu_prev=rw_mu_prev[layer], mu_next=rw_mu_next[layer], w0=rw_w0[layer], a0=rw_a0[layer],
                    wa=_rw_lora_weights(rw_w2[layer], rw_a2[layer]), g2=rw_g2[layer].astype(BF16),
                    k_k=rw_k_k[layer], k_a=rw_k_a[layer], r_k=rw_r_k[layer].reshape(-1),
                    gn_g=rw_gn_g[layer], gn_b=rw_gn_b[layer])
        rw_o = _bi_rwkv7(rw_in, rw_p, n_ctx)
        cv_o = _conformer_conv(cv_in, cv_dw_w[layer], cv_dw_b[layer], cv_ln_g[layer], cv_ln_b[layer], n_ctx)
        na_c = None if last else _context_attention(qkv, n_ctx)
        xa = _out_projection(xa, na_l, na_c, rw_o, cv_o, w_out_b, g1, n_ctx, with_ctx=not last)
        j = layer // 2
        if layer % 2 == 0:
            ffn_w = (ffn_w_gate[j].astype(BF16), ffn_w_up[j].astype(BF16), ffn_w_down[j].astype(BF16))
            xa = _dense_ffn(xa, norm2_g[layer], sh2, sc2, g2, *ffn_w, 0 if last else ctx_tiles)
            if last:
                out = _final_norm(xa, final_g, 0)
        else:
            if not last:
                raise NotImplementedError("context tokens through a mixture-of-experts layer")
            moe_w = (moe_w_gate[j].astype(BF16), moe_w_up[j].astype(BF16), moe_w_down[j].astype(BF16))
            out = _moe_layer_final(xa, norm2_g[layer], sh2, sc2, g2, moe_router[j], *moe_w, final_g)
    return out
```

```python
import functools

import jax
import jax.numpy as jnp
import numpy as np
from jax import lax
from jax.experimental import pallas as pl
from jax.experimental.pallas import tpu as pltpu

F32 = jnp.float32
BF16 = jnp.bfloat16

D_MODEL = 1024
GRID_W = 64
HEAD_DIM = 64
NA_WIDTH = 512
NA_HEADS = 8
NA_WIN_ROWS = 8
NA_WIN_COLS = 16
RW_WIDTH = 256
RW_HEADS = 4
RW_DECAY_RANK = 64
RW_LORA = 256
RW_GATE_RANK = 128
RW_IN = 1280
RW_GN_EPS = 64e-5
CV_WIDTH = 256
CV_CONV_LEN = 31
QKV_WIDTH = 3 * NA_WIDTH
IN_WIDTH = QKV_WIDTH + RW_IN + 2 * CV_WIDTH
N_EXPERTS = 8
TOP_K = 2
RMS_EPS = 1e-6
LN_EPS = 1e-5

LANES = 128
SUBLANES = 8
VMEM_LIMIT_BYTES = 56 * 1024 * 1024

ROW_TILE = 256
SAMPLES_PER_STEP = 4
MASK_VALUE = -1e30
NA_ROWS_PER_STEP = 8
MOE_ROWS = 1024
MOE_FF_TILE = 512
ROUTER_PAD = LANES
SCAN_STEPS = 64
RW_TILE = 256


def _params(*sem):
    return pltpu.CompilerParams(dimension_semantics=sem, vmem_limit_bytes=VMEM_LIMIT_BYTES)


def _col_chunks(width, step=512):
    out, c = [], 0
    while c < width:
        w = min(step, width - c)
        out.append((c, w))
        c += w
    return out


def _sigmoid(x):
    return 1.0 / (1.0 + jnp.exp(-x))


def _norm_mod(x, g, shift, scale):
    ms = jnp.mean(x * x, axis=-1, keepdims=True)
    h = x * lax.rsqrt(ms + RMS_EPS) * g
    return h * (1.0 + scale) + shift


def _dot_exact_rhs(x, m):
    hi = x.astype(BF16)
    r1 = x - hi.astype(F32)
    mid = r1.astype(BF16)
    lo = (r1 - mid.astype(F32)).astype(BF16)
    dot = functools.partial(jnp.dot, preferred_element_type=F32)
    return dot(hi, m) + dot(mid, m) + dot(lo, m)


def _mod_spec(n_batch, ctx_tiles):
    g = SAMPLES_PER_STEP
    return pl.BlockSpec((g, 1, D_MODEL), lambda b, i: (jnp.where(i < ctx_tiles, n_batch // g, b), 0, 0))


def _row_spec(width, tile_offset=0):
    return pl.BlockSpec((SAMPLES_PER_STEP, ROW_TILE, width), lambda b, i: (b, i + tile_offset, 0))


def _stack(x):
    return x.reshape(x.shape[0] * x.shape[1], x.shape[2])


def _unstack(x):
    return x.reshape(SAMPLES_PER_STEP, x.shape[0] // SAMPLES_PER_STEP, x.shape[1])


def _whole(shape):
    return pl.BlockSpec(shape, lambda *_: (0,) * len(shape), pipeline_mode=pl.Buffered(1))


def _mod_kernel(c_ref, w_ref, b_ref, o_ref):
    c = c_ref[...]
    cs = c * _sigmoid(c)
    o_ref[0] = jnp.dot(cs, w_ref[0], precision=lax.Precision.HIGHEST,
                       preferred_element_type=F32) + b_ref[0]


def _modulation(c_all, mod_w, mod_b):
    depth, d, n = mod_w.shape
    rows = c_all.shape[0]
    tn = 768
    return pl.pallas_call(
        _mod_kernel,
        grid=(depth, n // tn),
        in_specs=[
            pl.BlockSpec((rows, d), lambda l, j: (0, 0)),
            pl.BlockSpec((1, d, tn), lambda l, j: (l, 0, j)),
            pl.BlockSpec((1, 1, tn), lambda l, j: (l, 0, j)),
        ],
        out_specs=pl.BlockSpec((1, rows, tn), lambda l, j: (l, 0, j)),
        out_shape=jax.ShapeDtypeStruct((depth, rows, n), F32),
        compiler_params=_params("arbitrary", "arbitrary"),
        name="modulation",
    )(c_all, mod_w, mod_b.reshape(depth, 1, n))


def _proj_kernel(x_ref, g_ref, sh_ref, sc_ref, w_ref, qkv_ref, rw_ref, cv_ref):
    hb = _stack(_norm_mod(x_ref[...], g_ref[...], sh_ref[...], sc_ref[...])).astype(BF16)
    for ref, base in ((qkv_ref, 0), (rw_ref, QKV_WIDTH), (cv_ref, QKV_WIDTH + RW_IN)):
        for c0, w in _col_chunks(ref.shape[-1]):
            y = jnp.dot(hb, w_ref[:, base + c0:base + c0 + w], preferred_element_type=F32)
            ref[:, :, c0:c0 + w] = _unstack(y.astype(ref.dtype))


def _in_projection(x, g, shift, scale, w_in_bf16, ctx_tiles):
    b, t, d = x.shape
    ms = _mod_spec(b, ctx_tiles)
    return pl.pallas_call(
        _proj_kernel,
        grid=(b // SAMPLES_PER_STEP, t // ROW_TILE),
        in_specs=[_row_spec(d), _whole((1, d)), ms, ms, _whole((d, IN_WIDTH))],
        out_specs=[_row_spec(QKV_WIDTH), _row_spec(RW_IN), _row_spec(2 * CV_WIDTH)],
        out_shape=[
            jax.ShapeDtypeStruct((b, t, QKV_WIDTH), BF16),
            jax.ShapeDtypeStruct((b, t, RW_IN), F32),
            jax.ShapeDtypeStruct((b, t, 2 * CV_WIDTH), F32),
        ],
        compiler_params=_params("arbitrary", "arbitrary"),
        name="in_projection",
    )(x, g.reshape(1, d), shift, scale, w_in_bf16)


def _head_pair_queries(q):
    lane = lax.broadcasted_iota(jnp.int32, q.shape, 1)
    qs = q * jnp.asarray(HEAD_DIM ** -0.5, q.dtype)
    zero = jnp.zeros_like(qs)
    return jnp.concatenate([jnp.where(lane < HEAD_DIM, qs, zero), jnp.where(lane >= HEAD_DIM, qs, zero)], axis=0)


def _head_pair_merge(o, rows):
    lane = lax.broadcasted_iota(jnp.int32, (rows, LANES), 1)
    return jnp.where(lane < HEAD_DIM, o[:rows], o[rows:])


_NT = (((1,), (1,)), ((), ()))


def _na_kernel(q_ref, k_ref, v_ref, bias_ref, o_ref, *, rows, n_ctx):
    n_win = NA_WIN_ROWS * GRID_W
    k_ctx = k_ref[0, 0:n_ctx, :]
    v_ctx = v_ref[0, 0:n_ctx, :]
    for j in range(NA_ROWS_PER_STEP):
        i = pl.program_id(2) * NA_ROWS_PER_STEP + j
        rs = jnp.clip(i - NA_WIN_ROWS // 2, 0, rows - NA_WIN_ROWS)
        start = pl.multiple_of(n_ctx + rs * GRID_W, GRID_W)
        k_win = k_ref[0, pl.ds(start, n_win), :]
        v_win = v_ref[0, pl.ds(start, n_win), :]
        qb = _head_pair_queries(q_ref[0, pl.ds(pl.multiple_of(n_ctx + i * GRID_W, GRID_W), GRID_W), :])
        bias = bias_ref[0, rs - i + NA_WIN_ROWS - 1]
        s_win = lax.dot_general(qb, k_win, _NT, preferred_element_type=F32) + bias
        s_ctx = lax.dot_general(qb, k_ctx, _NT, preferred_element_type=F32)
        m = jnp.maximum(jnp.max(s_win, axis=-1, keepdims=True), jnp.max(s_ctx, axis=-1, keepdims=True))
        p_win = jnp.exp(s_win - m)
        p_ctx = jnp.exp(s_ctx - m)
        denom = jnp.sum(p_win, axis=-1, keepdims=True) + jnp.sum(p_ctx, axis=-1, keepdims=True)
        o = (jnp.dot(p_win.astype(BF16), v_win, preferred_element_type=F32)
             + jnp.dot(p_ctx.astype(BF16), v_ctx, preferred_element_type=F32)) / denom
        o_ref[0, j * GRID_W:(j + 1) * GRID_W, :] = _head_pair_merge(o, GRID_W).astype(o_ref.dtype)


def _na_bias_table(rpb):
    qc = np.arange(GRID_W)[:, None]
    kc = np.arange(GRID_W)[None, :]
    ws = np.clip(qc - NA_WIN_COLS // 2, 0, GRID_W - NA_WIN_COLS)
    mask = (kc >= ws) & (kc < ws + NA_WIN_COLS)
    rel = np.clip(kc - qc + NA_WIN_COLS - 1, 0, 2 * NA_WIN_COLS - 2)
    full = jnp.where(mask[None, None], rpb[:, :, rel].astype(F32), MASK_VALUE)
    dr = np.arange(NA_WIN_ROWS)[:, None] + np.arange(NA_WIN_ROWS)[None, :]
    t = full[:, dr]
    t = t.transpose(0, 1, 3, 2, 4).reshape(NA_HEADS // 2, 2, NA_WIN_ROWS, GRID_W, NA_WIN_ROWS * GRID_W)
    return t.transpose(0, 2, 1, 3, 4).reshape(NA_HEADS // 2, NA_WIN_ROWS, 2 * GRID_W, NA_WIN_ROWS * GRID_W)


def _neighbourhood_attention(qkv, bias_table, n_ctx):
    b, t, _ = qkv.shape
    n = t - n_ctx
    rows = n // GRID_W
    hp = NA_HEADS // 2
    koff, voff = NA_WIDTH // LANES, 2 * NA_WIDTH // LANES
    qrows = NA_ROWS_PER_STEP * GRID_W
    assert rows % NA_ROWS_PER_STEP == 0
    return pl.pallas_call(
        functools.partial(_na_kernel, rows=rows, n_ctx=n_ctx),
        grid=(b, hp, rows // NA_ROWS_PER_STEP),
        in_specs=[
            pl.BlockSpec((1, t, LANES), lambda bi, h, i: (bi, 0, h)),
            pl.BlockSpec((1, t, LANES), lambda bi, h, i: (bi, 0, koff + h)),
            pl.BlockSpec((1, t, LANES), lambda bi, h, i: (bi, 0, voff + h)),
            pl.BlockSpec((1, NA_WIN_ROWS, 2 * GRID_W, NA_WIN_ROWS * GRID_W), lambda bi, h, i: (h, 0, 0, 0)),
        ],
        out_specs=pl.BlockSpec((1, qrows, LANES), lambda bi, h, i: (bi, i, h)),
        out_shape=jax.ShapeDtypeStruct((b, n, NA_WIDTH), BF16),
        compiler_params=_params("arbitrary", "arbitrary", "arbitrary"),
        name="neighbourhood_attention",
    )(qkv, qkv, qkv, bias_table)


def _ctx_attn_kernel(q_ref, k_ref, v_ref, o_ref):
    l = q_ref.shape[1]
    qb = _head_pair_queries(q_ref[0])
    s = lax.dot_general(qb, k_ref[0], _NT, preferred_element_type=F32)
    p = jnp.exp(s - jnp.max(s, axis=-1, keepdims=True))
    o = jnp.dot(p.astype(BF16), v_ref[0], preferred_element_type=F32) / jnp.sum(p, axis=-1, keepdims=True)
    o_ref[0] = _head_pair_merge(o, l).astype(o_ref.dtype)


def _context_attention(qkv, n_ctx):
    b = qkv.shape[0]
    hp = NA_HEADS // 2
    koff, voff = NA_WIDTH // LANES, 2 * NA_WIDTH // LANES
    return pl.pallas_call(
        _ctx_attn_kernel,
        grid=(b, hp),
        in_specs=[
            pl.BlockSpec((1, n_ctx, LANES), lambda bi, h: (bi, 0, h)),
            pl.BlockSpec((1, n_ctx, LANES), lambda bi, h: (bi, 0, koff + h)),
            pl.BlockSpec((1, n_ctx, LANES), lambda bi, h: (bi, 0, voff + h)),
        ],
        out_specs=pl.BlockSpec((1, n_ctx, LANES), lambda bi, h: (bi, 0, h)),
        out_shape=jax.ShapeDtypeStruct((b, n_ctx, NA_WIDTH), BF16),
        compiler_params=_params("arbitrary", "arbitrary"),
        name="context_attention",
    )(qkv, qkv, qkv)


SCAN_GROUPS = ("r", "v", "z", "w0", "k0", "b0", "w1", "k1", "b1")
POST_COLS = 4 * RW_WIDTH


def _softplus(x):
    return jnp.maximum(x, 0.0) + jnp.log(1.0 + jnp.exp(-jnp.abs(x)))


def _rw_prep_kernel(u_ref, up_ref, un_ref, mup_ref, mun_ref, ones_ref, kk_ref, ka_ref, rk_ref,
                    w0_ref, a0_ref, wa_ref, g2_ref, scan_ref, post_ref, *, n_ctx):
    i = pl.program_id(0)
    b = pl.program_id(1)
    tt = u_ref.shape[1]
    n_tok = pl.num_programs(0) * tt
    u = u_ref[0]
    row = lax.broadcasted_iota(jnp.int32, u.shape, 0)
    tok = row + i * tt
    prev = jnp.where(row == 0, up_ref[0, SUBLANES - 1:SUBLANES, :], pltpu.roll(u, 1, 0))
    prev = jnp.where(tok == 0, 0.0, jnp.where(tok == n_ctx, 0.0, prev))
    nxt = jnp.where(row == tt - 1, un_ref[0, 0:1, :], pltpu.roll(u, tt - 1, 0))
    nxt = jnp.where(tok == n_tok - 1, 0.0, jnp.where(tok == n_ctx - 1, 0.0, nxt))
    us = u + mup_ref[...] * (prev - u) + mun_ref[...] * (nxt - u)

    def put(group, val):
        g = SCAN_GROUPS.index(group)
        for c in range(2):
            scan_ref[2 * g + c, pl.ds(b, tt, stride=SUBLANES), :] = val[:, c * LANES:(c + 1) * LANES]

    ones = ones_ref[...]
    r = us[:, 0:RW_WIDTH]
    k = us[:, RW_WIDTH:2 * RW_WIDTH]
    v = us[:, 2 * RW_WIDTH:3 * RW_WIDTH]
    kk = k * kk_ref[...]
    kk = kk * lax.rsqrt(_dot_exact_rhs(kk * kk, ones) + 1e-12)
    put("r", r)
    put("v", v)
    put("z", -kk)
    for d in range(2):
        lo = 3 * RW_WIDTH + d * RW_LORA
        wa_in = us[:, lo:lo + LANES]
        lane = lax.broadcasted_iota(jnp.int32, wa_in.shape, 1)
        wa_in = jnp.where(lane < RW_DECAY_RANK, jnp.tanh(wa_in), wa_in)
        wa = jnp.dot(wa_in.astype(BF16), wa_ref[d], preferred_element_type=F32)
        log_w = -_softplus(-(w0_ref[d] + wa[:, 0:RW_WIDTH])) - 0.5
        decay = jnp.exp(-jnp.exp(log_w))
        a = _sigmoid(a0_ref[d] + wa[:, RW_WIDTH:2 * RW_WIDTH])
        gate_in = _sigmoid(us[:, lo + LANES:lo + 2 * LANES])
        g = jnp.dot(gate_in.astype(BF16), g2_ref[d], preferred_element_type=F32)
        kd = k * (1.0 + (a - 1.0) * ka_ref[...])
        bonus = _dot_exact_rhs(r * kd * rk_ref[...], ones) * v
        put("w%d" % d, decay)
        put("k%d" % d, kd)
        put("b%d" % d, kk * a)
        post_ref[0, :, 2 * d * RW_WIDTH:(2 * d + 1) * RW_WIDTH] = g
        post_ref[0, :, (2 * d + 1) * RW_WIDTH:(2 * d + 2) * RW_WIDTH] = bonus


def _head_ones():
    h = np.arange(RW_WIDTH) // HEAD_DIM
    return jnp.asarray(h[:, None] == h[None, :], BF16)


def _rw_prep(u, p, n_ctx):
    b, t, _ = u.shape
    assert b == SUBLANES
    tt = RW_TILE
    nt8 = t // SUBLANES
    row = lambda vec, n: vec.reshape(1, n)
    return pl.pallas_call(
        functools.partial(_rw_prep_kernel, n_ctx=n_ctx),
        grid=(t // tt, b),
        in_specs=[
            pl.BlockSpec((1, tt, RW_IN), lambda i, bi: (bi, i, 0)),
            pl.BlockSpec((1, SUBLANES, RW_IN), lambda i, bi: (bi, jnp.maximum(i * (tt // SUBLANES) - 1, 0), 0)),
            pl.BlockSpec((1, SUBLANES, RW_IN), lambda i, bi: (bi, jnp.minimum((i + 1) * (tt // SUBLANES), nt8 - 1), 0)),
            _whole((1, RW_IN)), _whole((1, RW_IN)),
            _whole((RW_WIDTH, RW_WIDTH)),
            _whole((1, RW_WIDTH)), _whole((1, RW_WIDTH)), _whole((1, RW_WIDTH)),
            _whole((2, 1, RW_WIDTH)), _whole((2, 1, RW_WIDTH)),
            _whole((2, LANES, 2 * RW_WIDTH)), _whole((2, RW_GATE_RANK, RW_WIDTH)),
        ],
        out_specs=[
            pl.BlockSpec((2 * len(SCAN_GROUPS), tt * b, LANES), lambda i, bi: (0, i, 0)),
            pl.BlockSpec((1, tt, POST_COLS), lambda i, bi: (bi, i, 0)),
        ],
        out_shape=[
            jax.ShapeDtypeStruct((2 * len(SCAN_GROUPS), t * b, LANES), F32),
            jax.ShapeDtypeStruct((b, t, POST_COLS), F32),
        ],
        compiler_params=_params("arbitrary", "arbitrary"),
        name="rwkv_prep",
    )(u, u, u, row(p["mu_prev"], RW_IN), row(p["mu_next"], RW_IN), _head_ones(),
      row(p["k_k"], RW_WIDTH), row(p["k_a"], RW_WIDTH), row(p["r_k"], RW_WIDTH),
      p["w0"].reshape(2, 1, RW_WIDTH), p["a0"].reshape(2, 1, RW_WIDTH), p["wa"], p["g2"])


def _rw_lora_weights(w2, a2):
    z = jnp.zeros_like(w2)
    top = jnp.concatenate([w2, z], axis=-1)
    bot = jnp.concatenate([z, a2], axis=-1)
    return jnp.concatenate([top, bot], axis=1).astype(BF16)


_QUARTER = LANES // 4
_VROWS = HEAD_DIM // 4


def _chain_tile(f_ref, b_ref, sf, sb):
    pieces = [f_ref[0, pl.ds(sf, SUBLANES), :], f_ref[1, pl.ds(sf, SUBLANES), :],
              b_ref[0, pl.ds(sb, SUBLANES), :], b_ref[1, pl.ds(sb, SUBLANES), :]]
    return jnp.concatenate(pieces * 4, axis=0).T


def _scan_kernel(rf, vf, zf, wf, kf, bf, rb, vb, zb, wb, kb, bb, yf_ref, yb_ref, s_ref, t_ref, v_ref, y_ref):
    steps = rf.shape[1] // SUBLANES
    zero = jnp.zeros((_VROWS, LANES), F32)

    @pl.when(pl.program_id(0) == 0)
    def _():
        s_ref[...] = jnp.zeros_like(s_ref)
        y_ref[...] = jnp.zeros_like(y_ref)

    lane_q = lax.broadcasted_iota(jnp.int32, (_VROWS, LANES), 1) // _QUARTER
    col_q = (lax.broadcasted_iota(jnp.int32, (_QUARTER, LANES), 1) % HEAD_DIM) // _VROWS

    def offsets(s):
        s = jnp.clip(s, 0, steps - 1)
        return pl.multiple_of(s * SUBLANES, SUBLANES), pl.multiple_of((steps - 1 - s) * SUBLANES, SUBLANES)

    def prepare(s, p):
        sf, sb = offsets(s)
        zf_off, zb_off = offsets(s + 1)
        t_ref[p, 0] = _chain_tile(zf, zb, zf_off, zb_off)
        for n, (f, b) in enumerate(((wf, wb), (bf, bb), (kf, kb), (rf, rb))):
            t_ref[p, n + 1] = _chain_tile(f, b, sf, sb)
        vt = _chain_tile(vf, vb, sf, sb)
        for h in range(2):
            v = zero
            for q in range(4):
                lo = h * HEAD_DIM + q * _VROWS
                v = jnp.where(lane_q == q, vt[lo:lo + _VROWS], v)
            v_ref[p, h] = v

    def flush(s, p):
        sf, sb = offsets(s)
        yt = jnp.concatenate([y_ref[p, 0]] * 4 + [y_ref[p, 1]] * 4, axis=0).T
        nat = jnp.zeros((_QUARTER, LANES), F32)
        for q in range(4):
            nat = jnp.where(col_q == q, yt[q * _QUARTER:(q + 1) * _QUARTER], nat)
        yf_ref[0, pl.ds(sf, SUBLANES), :] = nat[0:8]
        yf_ref[1, pl.ds(sf, SUBLANES), :] = nat[8:16]
        yb_ref[0, pl.ds(sb, SUBLANES), :] = nat[16:24]
        yb_ref[1, pl.ds(sb, SUBLANES), :] = nat[24:32]

    def update(p, sz):
        sz_next = []
        for h in range(2):
            base = h * HEAD_DIM
            v = v_ref[p, h]
            y = [zero, zero]
            zn = [zero, zero]
            for k in range(HEAD_DIM):
                row = pl.ds(base + k, 1)
                st = s_ref[base + k] * t_ref[p, 1, row, :] + sz[h] * t_ref[p, 2, row, :] + v * t_ref[p, 3, row, :]
                s_ref[base + k] = st
                y[k % 2] = y[k % 2] + st * t_ref[p, 4, row, :]
                zn[k % 2] = zn[k % 2] + st * t_ref[p, 0, row, :]
            y_ref[p, h] = y[0] + y[1]
            sz_next.append(zn[0] + zn[1])
        return tuple(sz_next)

    f0, b0 = offsets(0)
    t_ref[1, 0] = _chain_tile(zf, zb, f0, b0)
    sz0 = []
    for h in range(2):
        acc = [zero, zero]
        for k in range(HEAD_DIM):
            r = h * HEAD_DIM + k
            acc[k % 2] = acc[k % 2] + s_ref[r] * t_ref[1, 0, pl.ds(r, 1), :]
        sz0.append(acc[0] + acc[1])
    prepare(0, 0)

    def pair(j, sz):
        s = 2 * j
        for p in range(2):
            prepare(s + p + 1, 1 - p)
            sz = update(p, sz)
            flush(s + p - 1, 1 - p)
        return sz

    lax.fori_loop(0, steps // 2, pair, tuple(sz0))
    flush(steps - 1, 1)


def _rwkv7_scan(ops, n_batch, n_ctx):
    rows = ops.shape[1]
    blk = SCAN_STEPS * n_batch
    nb = rows // blk
    nb_c = n_ctx * n_batch // blk

    def mirror(i):
        return jnp.where(i < nb_c, nb_c - 1 - i, nb_c + nb - 1 - i)

    def spec(group, backward):
        g = SCAN_GROUPS.index(group)
        if backward:
            return pl.BlockSpec((2, blk, LANES), lambda i: (g, mirror(i), 0))
        return pl.BlockSpec((2, blk, LANES), lambda i: (g, i, 0))

    fw = [spec(g, False) for g in ("r", "v", "z", "w0", "k0", "b0")]
    bw = [spec(g, True) for g in ("r", "v", "z", "w1", "k1", "b1")]
    out_sds = jax.ShapeDtypeStruct((2, rows, LANES), F32)
    return pl.pallas_call(
        _scan_kernel,
        grid=(nb,),
        in_specs=fw + bw,
        out_specs=[pl.BlockSpec((2, blk, LANES), lambda i: (0, i, 0)),
                   pl.BlockSpec((2, blk, LANES), lambda i: (0, mirror(i), 0))],
        out_shape=[out_sds, out_sds],
        scratch_shapes=[pltpu.VMEM((2 * HEAD_DIM, _VROWS, LANES), F32), pltpu.VMEM((2, 5, LANES, LANES), F32),
                        pltpu.VMEM((2, 2, _VROWS, LANES), F32), pltpu.VMEM((2, 2, _VROWS, LANES), F32)],
        compiler_params=_params("arbitrary"),
        name="rwkv_scan",
    )(*([ops] * 12))


def _rw_post_kernel(yf_ref, yb_ref, post_ref, ones_ref, gg_ref, gb_ref, o_ref):
    b = pl.program_id(1)
    tt = o_ref.shape[1]
    ones = ones_ref[...]
    out = None
    for d, y_ref in enumerate((yf_ref, yb_ref)):
        y = jnp.concatenate([y_ref[0, pl.ds(b, tt, stride=SUBLANES), :],
                             y_ref[1, pl.ds(b, tt, stride=SUBLANES), :]], axis=1)
        mu = _dot_exact_rhs(y, ones) * (1.0 / HEAD_DIM)
        yc = y - mu
        var = _dot_exact_rhs(yc * yc, ones) * (1.0 / HEAD_DIM)
        yn = yc * lax.rsqrt(var + RW_GN_EPS) * gg_ref[...] + gb_ref[...]
        g = post_ref[0, :, 2 * d * RW_WIDTH:(2 * d + 1) * RW_WIDTH]
        bonus = post_ref[0, :, (2 * d + 1) * RW_WIDTH:(2 * d + 2) * RW_WIDTH]
        term = (yn + bonus) * g
        out = term if out is None else out + term
    o_ref[0] = out.astype(o_ref.dtype)


def _rw_post(y_f, y_b, post, gn_g, gn_b):
    b, t, _ = post.shape
    tt = RW_TILE
    yspec = pl.BlockSpec((2, tt * b, LANES), lambda i, bi: (0, i, 0))
    return pl.pallas_call(
        _rw_post_kernel,
        grid=(t // tt, b),
        in_specs=[yspec, yspec, pl.BlockSpec((1, tt, POST_COLS), lambda i, bi: (bi, i, 0)),
                  _whole((RW_WIDTH, RW_WIDTH)), _whole((1, RW_WIDTH)), _whole((1, RW_WIDTH))],
        out_specs=pl.BlockSpec((1, tt, RW_WIDTH), lambda i, bi: (bi, i, 0)),
        out_shape=jax.ShapeDtypeStruct((b, t, RW_WIDTH), BF16),
        compiler_params=_params("arbitrary", "arbitrary"),
        name="rwkv_post",
    )(y_f, y_b, post, _head_ones(), gn_g.reshape(1, RW_WIDTH), gn_b.reshape(1, RW_WIDTH))


def _bi_rwkv7(u, p, n_ctx):
    ops, post = _rw_prep(u, p, n_ctx)
    y_f, y_b = _rwkv7_scan(ops, u.shape[0], n_ctx)
    return _rw_post(y_f, y_b, post, p["gn_g"], p["gn_b"])


CONV_PAD = 16


def _conv_kernel(u_ref, w_ref, b_ref, lg_ref, lb_ref, o_ref, pad_ref, *, n_ctx):
    t = u_ref.shape[1]
    half = CV_CONV_LEN // 2
    chunk = ROW_TILE
    u = u_ref[0]
    h = u[:, :CV_WIDTH] * _sigmoid(u[:, CV_WIDTH:])
    zeros = jnp.zeros((CONV_PAD, CV_WIDTH), F32)
    lat0 = 2 * CONV_PAD + n_ctx
    pad_ref[0:CONV_PAD, :] = zeros
    pad_ref[CONV_PAD:CONV_PAD + n_ctx, :] = h[:n_ctx]
    pad_ref[CONV_PAD + n_ctx:lat0, :] = zeros
    pad_ref[lat0:lat0 + t - n_ctx, :] = h[n_ctx:]
    pad_ref[lat0 + t - n_ctx:lat0 + t - n_ctx + CONV_PAD, :] = zeros
    w = w_ref[...]

    def body(c, carry):
        out0 = pl.multiple_of(c * chunk, chunk)
        base = pl.multiple_of(out0 + jnp.where(c >= n_ctx // chunk, CONV_PAD, 0), CONV_PAD)
        win = pad_ref[pl.ds(base, chunk + 2 * CONV_PAD), :]
        acc = jnp.zeros((chunk, CV_WIDTH), F32)
        for j in range(CV_CONV_LEN):
            off = CONV_PAD - half + j
            acc = acc + win[off:off + chunk, :] * w[j:j + 1, :]
        hh = acc + b_ref[...]
        mu = jnp.mean(hh, axis=-1, keepdims=True)
        hc = hh - mu
        var = jnp.mean(hc * hc, axis=-1, keepdims=True)
        y = hc * lax.rsqrt(var + LN_EPS) * lg_ref[...] + lb_ref[...]
        o_ref[0, pl.ds(out0, chunk), :] = (y * _sigmoid(y)).astype(o_ref.dtype)
        return carry

    lax.fori_loop(0, t // chunk, body, 0)


def _conformer_conv(u, dw_w, dw_b, ln_g, ln_b, n_ctx):
    b, t, _ = u.shape
    assert n_ctx % ROW_TILE == 0 and t % ROW_TILE == 0
    row = lambda vec: vec.reshape(1, CV_WIDTH)
    return pl.pallas_call(
        functools.partial(_conv_kernel, n_ctx=n_ctx),
        grid=(b,),
        in_specs=[pl.BlockSpec((1, t, 2 * CV_WIDTH), lambda bi: (bi, 0, 0)),
                  _whole((CV_CONV_LEN, CV_WIDTH)), _whole((1, CV_WIDTH)), _whole((1, CV_WIDTH)),
                  _whole((1, CV_WIDTH))],
        out_specs=pl.BlockSpec((1, t, CV_WIDTH), lambda bi: (bi, 0, 0)),
        out_shape=jax.ShapeDtypeStruct((b, t, CV_WIDTH), BF16),
        scratch_shapes=[pltpu.VMEM((t + 3 * CONV_PAD, CV_WIDTH), F32)],
        compiler_params=_params("arbitrary"),
        name="conformer_conv",
    )(u, dw_w, row(dw_b), row(ln_g), row(ln_b))


def _wout_kernel(x_ref, nal_ref, nac_ref, rw_ref, cv_ref, w_ref, gate_ref, o_ref, *, ctx_tiles):
    dot = functools.partial(jnp.dot, preferred_element_type=F32)
    na = nal_ref[...]
    if ctx_tiles:
        na = jnp.where(pl.program_id(1) < ctx_tiles, nac_ref[...], na)
    acc = (dot(_stack(na), w_ref[0:NA_WIDTH, :])
           + dot(_stack(rw_ref[...]), w_ref[NA_WIDTH:NA_WIDTH + RW_WIDTH, :])
           + dot(_stack(cv_ref[...]), w_ref[NA_WIDTH + RW_WIDTH:, :]))
    o_ref[...] = x_ref[...] + gate_ref[...] * _unstack(acc)


def _out_projection(x, na_l, na_c, rw, cv, w_out_bf16, gate, n_ctx, with_ctx):
    b, t, d = x.shape
    ctx_tiles = n_ctx // ROW_TILE
    assert ctx_tiles == 1 or not with_ctx
    off = 0 if with_ctx else ctx_tiles
    n_rows = t if with_ctx else t - n_ctx
    g = SAMPLES_PER_STEP
    if with_ctx:
        nal_spec = pl.BlockSpec((g, ROW_TILE, NA_WIDTH), lambda bi, i: (bi, jnp.maximum(i - ctx_tiles, 0), 0))
    else:
        nal_spec = _row_spec(NA_WIDTH)
        na_c = na_l
    return pl.pallas_call(
        functools.partial(_wout_kernel, ctx_tiles=ctx_tiles if with_ctx else 0),
        grid=(b // g, n_rows // ROW_TILE),
        in_specs=[_row_spec(d, off), nal_spec,
                  pl.BlockSpec((g, ROW_TILE, NA_WIDTH), lambda bi, i: (bi, 0, 0)),
                  _row_spec(RW_WIDTH, off), _row_spec(CV_WIDTH, off),
                  _whole((d, d)), _mod_spec(b, ctx_tiles if with_ctx else 0)],
        out_specs=_row_spec(d),
        out_shape=jax.ShapeDtypeStruct((b, n_rows, d), F32),
        compiler_params=_params("arbitrary", "arbitrary"),
        name="out_projection",
    )(x, na_l, na_c, rw, cv, w_out_bf16, gate)


def _ffn_kernel(x_ref, g_ref, sh_ref, sc_ref, gate_ref, wg_ref, wu_ref, wd_ref, o_ref, acc_ref):
    x = x_ref[...]
    hb = _stack(_norm_mod(x, g_ref[...], sh_ref[...], sc_ref[...])).astype(BF16)
    dot = functools.partial(jnp.dot, preferred_element_type=F32)
    for n, (c0, w) in enumerate(_col_chunks(wg_ref.shape[1])):
        g = dot(hb, wg_ref[:, c0:c0 + w])
        u = dot(hb, wu_ref[:, c0:c0 + w])
        a = (g * _sigmoid(g) * u).astype(BF16)
        y = dot(a, wd_ref[c0:c0 + w, :])
        if n == 0:
            acc_ref[...] = y
        else:
            acc_ref[...] += y
    o_ref[...] = x + gate_ref[...] * _unstack(acc_ref[...])


def _dense_ffn(x, g, shift, scale, gate, wg, wu, wd, ctx_tiles):
    b, t, d = x.shape
    dff = wg.shape[1]
    ms = _mod_spec(b, ctx_tiles)
    return pl.pallas_call(
        _ffn_kernel,
        grid=(b // SAMPLES_PER_STEP, t // ROW_TILE),
        in_specs=[_row_spec(d), _whole((1, d)), ms, ms, ms,
                  _whole((d, dff)), _whole((d, dff)), _whole((dff, d))],
        out_specs=_row_spec(d),
        out_shape=jax.ShapeDtypeStruct((b, t, d), F32),
        scratch_shapes=[pltpu.VMEM((SAMPLES_PER_STEP * ROW_TILE, d), F32)],
        compiler_params=_params("arbitrary", "arbitrary"),
        name="dense_ffn",
    )(x, g.reshape(1, d), shift, scale, gate, wg, wu, wd)


def _router_kernel(x_ref, g_ref, sh_ref, sc_ref, wr_ref, h_ref, logit_ref):
    h = _norm_mod(x_ref[...], g_ref[...], sh_ref[...], sc_ref[...])
    h_ref[...] = h
    logit_ref[...] = _unstack(jnp.dot(_stack(h), wr_ref[...], precision=lax.Precision.HIGHEST,
                                      preferred_element_type=F32))


def _moe_router(x, g, shift, scale, router_pad):
    b, t, d = x.shape
    ms = _mod_spec(b, 0)
    return pl.pallas_call(
        _router_kernel,
        grid=(b // SAMPLES_PER_STEP, t // ROW_TILE),
        in_specs=[_row_spec(d), _whole((1, d)), ms, ms, _whole((d, ROUTER_PAD))],
        out_specs=[_row_spec(d), _row_spec(ROUTER_PAD)],
        out_shape=[jax.ShapeDtypeStruct((b, t, d), F32), jax.ShapeDtypeStruct((b, t, ROUTER_PAD), F32)],
        compiler_params=_params("arbitrary", "arbitrary"),
        name="moe_router",
    )(x, g.reshape(1, d), shift, scale, router_pad)


def _row_copy(src_hbm, dst_vmem, sem, src_row, dst_row):
    return pltpu.make_async_copy(src_hbm.at[pl.ds(src_row, 1)], dst_vmem.at[pl.ds(dst_row, 1)], sem)


GATHER_UNROLL = 8


def _start_row_gather(src_hbm, dst_vmem, sem, index_of_row, rows):
    def body(r, c):
        _row_copy(src_hbm, dst_vmem, sem, index_of_row(r), r).start()
        return c

    lax.fori_loop(0, rows, body, 0, unroll=GATHER_UNROLL)


def _wait_row_gather(src_hbm, dst_vmem, sem, rows):
    def body(r, c):
        _row_copy(src_hbm, dst_vmem, sem, 0, r).wait()
        return c

    lax.fori_loop(0, rows, body, 0, unroll=GATHER_UNROLL)


def _expert_kernel(be_ref, nused_ref, tok_ref, tok_next_ref, h_hbm, wg_ref, wu_ref, wd_ref, o_ref,
                   xg_ref, xb_ref, acc_ref, sem):
    i = pl.program_id(0)
    j = pl.program_id(1)
    rows = xb_ref.shape[0]
    n_used = nused_ref[0]
    used = i < n_used
    slot = i % 2

    @pl.when(jnp.logical_and(i == 0, j == 0))
    def _():
        _start_row_gather(h_hbm, xg_ref.at[0], sem.at[0], lambda r: tok_ref[0, 0, r], rows)

    @pl.when(jnp.logical_and(used, j == 0))
    def _():
        _wait_row_gather(h_hbm, xg_ref.at[slot], sem.at[slot], rows)
        xb_ref[...] = xg_ref[slot].astype(BF16)

    @pl.when(jnp.logical_and(i + 1 < n_used, j == 1))
    def _():
        _start_row_gather(h_hbm, xg_ref.at[1 - slot], sem.at[1 - slot], lambda r: tok_next_ref[0, 0, r], rows)

    @pl.when(used)
    def _():
        dot = functools.partial(jnp.dot, preferred_element_type=F32)
        xb = xb_ref[...]
        g = dot(xb, wg_ref[0])
        u = dot(xb, wu_ref[0])
        a = (g * _sigmoid(g) * u).astype(BF16)
        y = dot(a, wd_ref[0])

        @pl.when(j == 0)
        def _():
            acc_ref[...] = y

        @pl.when(j > 0)
        def _():
            acc_ref[...] += y

    last = j == pl.num_programs(1) - 1

    @pl.when(jnp.logical_and(used, last))
    def _():
        o_ref[...] = acc_ref[...]

    @pl.when(jnp.logical_and(jnp.logical_not(used), last))
    def _():
        o_ref[...] = jnp.zeros_like(o_ref)


def _moe_experts(h_flat, slot_tok, block_e, n_used, wg, wu, wd):
    nb = slot_tok.shape[0]
    d = h_flat.shape[1]
    dff = wg.shape[2]
    nj = dff // MOE_FF_TILE
    assert nj >= 2
    grid_spec = pltpu.PrefetchScalarGridSpec(
        num_scalar_prefetch=2,
        grid=(nb, nj),
        in_specs=[
            pl.BlockSpec((1, 1, MOE_ROWS), lambda i, j, be, nu: (i, 0, 0), memory_space=pltpu.SMEM),
            pl.BlockSpec((1, 1, MOE_ROWS), lambda i, j, be, nu: (jnp.minimum(i + 1, nb - 1), 0, 0),
                         memory_space=pltpu.SMEM),
            pl.BlockSpec(memory_space=pl.ANY),
            pl.BlockSpec((1, d, MOE_FF_TILE), lambda i, j, be, nu: (be[i], 0, j)),
            pl.BlockSpec((1, d, MOE_FF_TILE), lambda i, j, be, nu: (be[i], 0, j)),
            pl.BlockSpec((1, MOE_FF_TILE, d), lambda i, j, be, nu: (be[i], j, 0)),
        ],
        out_specs=pl.BlockSpec((MOE_ROWS, d), lambda i, j, be, nu: (i, 0)),
        scratch_shapes=[pltpu.VMEM((2, MOE_ROWS, d), F32), pltpu.VMEM((MOE_ROWS, d), BF16),
                        pltpu.VMEM((MOE_ROWS, d), F32), pltpu.SemaphoreType.DMA((2,))],
    )
    return pl.pallas_call(
        _expert_kernel,
        grid_spec=grid_spec,
        out_shape=jax.ShapeDtypeStruct((nb * MOE_ROWS, d), F32),
        compiler_params=_params("arbitrary", "arbitrary"),
        name="moe_experts",
    )(block_e, n_used, slot_tok, slot_tok, h_flat, wg, wu, wd)


def _combine_kernel(dest_ref, dest_next_ref, x_ref, gates_ref, gate2_ref, fg_ref, yb_hbm, o_ref, y_ref, sem):
    i = pl.program_id(0)
    rows = x_ref.shape[1]
    slot = i % 2

    @pl.when(i == 0)
    def _():
        _start_row_gather(yb_hbm, y_ref.at[0], sem.at[0], lambda r: dest_ref[0, 0, r], TOP_K * rows)

    @pl.when(i + 1 < pl.num_programs(0))
    def _():
        _start_row_gather(yb_hbm, y_ref.at[1 - slot], sem.at[1 - slot], lambda r: dest_next_ref[0, 0, r],
                          TOP_K * rows)

    _wait_row_gather(yb_hbm, y_ref.at[slot], sem.at[slot], TOP_K * rows)
    gates = gates_ref[0]
    y = y_ref[slot, 0:rows, :] * gates[:, 0:1] + y_ref[slot, rows:2 * rows, :] * gates[:, 1:2]
    x = x_ref[0] + gate2_ref[0] * y
    ms = jnp.mean(x * x, axis=-1, keepdims=True)
    o_ref[0] = x * lax.rsqrt(ms + RMS_EPS) * fg_ref[...]


def _moe_combine_final(x, gates, dest, yb, gate2, final_g):
    nt, tm, d = x.shape
    tiles_per_sample = nt // gate2.shape[0]
    tile = lambda w: pl.BlockSpec((1, tm, w), lambda i: (i, 0, 0))
    return pl.pallas_call(
        _combine_kernel,
        grid=(nt,),
        in_specs=[
            pl.BlockSpec((1, 1, TOP_K * tm), lambda i: (i, 0, 0), memory_space=pltpu.SMEM),
            pl.BlockSpec((1, 1, TOP_K * tm), lambda i: (jnp.minimum(i + 1, nt - 1), 0, 0),
                         memory_space=pltpu.SMEM),
            tile(d), tile(TOP_K),
            pl.BlockSpec((1, 1, d), lambda i: (i // tiles_per_sample, 0, 0)),
            pl.BlockSpec((1, d), lambda i: (0, 0)),
            pl.BlockSpec(memory_space=pl.ANY),
        ],
        out_specs=tile(d),
        out_shape=jax.ShapeDtypeStruct((nt, tm, d), F32),
        scratch_shapes=[pltpu.VMEM((2, TOP_K * tm, d), F32), pltpu.SemaphoreType.DMA((2,))],
        compiler_params=_params("arbitrary"),
        name="moe_combine_final",
    )(dest, dest, x, gates, gate2, final_g.reshape(1, d), yb)


def _moe_layer_final(x, g, shift, scale, gate2, router, wg, wu, wd, final_g):
    b, t, d = x.shape
    n_tok = b * t
    n_asg = n_tok * TOP_K
    router_pad = jnp.pad(router, ((0, 0), (0, ROUTER_PAD - N_EXPERTS)))
    h, logits = _moe_router(x, g, shift, scale, router_pad)
    logits = logits.reshape(n_tok, ROUTER_PAD)[:, :N_EXPERTS]
    top_logit, top_e = lax.top_k(logits, TOP_K)
    gates = jax.nn.softmax(top_logit, axis=-1)
    flat_e = top_e.reshape(-1)
    onehot = (flat_e[:, None] == jnp.arange(N_EXPERTS, dtype=flat_e.dtype)[None, :]).astype(jnp.int32)
    rank = jnp.cumsum(onehot, axis=0) - onehot
    counts = jnp.sum(onehot, axis=0)
    padded = (counts + MOE_ROWS - 1) // MOE_ROWS * MOE_ROWS
    pad_end = jnp.cumsum(padded)
    pad_start = pad_end - padded
    dest = (pad_start[flat_e] + jnp.sum(rank * onehot, axis=1)).astype(jnp.int32)
    n_blocks = (n_asg + MOE_ROWS - 1) // MOE_ROWS + N_EXPERTS
    flat_tok = jnp.arange(n_asg, dtype=jnp.int32) // TOP_K
    slot_tok = jnp.zeros((n_blocks * MOE_ROWS,), jnp.int32).at[dest].set(flat_tok)
    block_e = jnp.minimum(jnp.searchsorted(pad_end, jnp.arange(n_blocks) * MOE_ROWS, side="right"),
                          N_EXPERTS - 1).astype(jnp.int32)
    n_used = (pad_end[-1:] // MOE_ROWS).astype(jnp.int32)
    yb = _moe_experts(h.reshape(n_tok, d), slot_tok.reshape(n_blocks, 1, MOE_ROWS), block_e, n_used, wg, wu, wd)
    nt = n_tok // ROW_TILE
    tile_dest = dest.reshape(nt, ROW_TILE, TOP_K).transpose(0, 2, 1).reshape(nt, 1, TOP_K * ROW_TILE)
    out = _moe_combine_final(x.reshape(nt, ROW_TILE, d), gates.reshape(nt, ROW_TILE, TOP_K), tile_dest, yb,
                             gate2[:b], final_g)
    return out.reshape(b, t, d)


def _final_norm_kernel(x_ref, g_ref, o_ref):
    x = x_ref[...]
    ms = jnp.mean(x * x, axis=-1, keepdims=True)
    o_ref[...] = x * lax.rsqrt(ms + RMS_EPS) * g_ref[...]


def _final_norm(x, g, tile_offset):
    b, t, d = x.shape
    n_rows = t - tile_offset * ROW_TILE
    return pl.pallas_call(
        _final_norm_kernel,
        grid=(b // SAMPLES_PER_STEP, n_rows // ROW_TILE),
        in_specs=[_row_spec(d, tile_offset), _whole((1, d))],
        out_specs=_row_spec(d),
        out_shape=jax.ShapeDtypeStruct((b, n_rows, d), F32),
        compiler_params=_params("arbitrary", "arbitrary"),
        name="final_norm",
    )(x, g.reshape(1, d))


def kernel(x, c, ctx, c_ctx, norm1_g, norm2_g, mod_w, mod_b, w_in, w_out, na_rpb, rw_mu_prev, rw_mu_next, rw_w0, rw_w2, rw_a0, rw_a2, rw_g2, rw_k_k, rw_k_a, rw_r_k, rw_gn_g, rw_gn_b, cv_dw_w, cv_dw_b, cv_ln_g, cv_ln_b, ffn_w_gate, ffn_w_up, ffn_w_down, moe_router, moe_w_gate, moe_w_up, moe_w_down, final_g):
    b, n, _ = x.shape
    n_ctx = ctx.shape[1]
    depth = mod_w.shape[0]
    assert n_ctx == ROW_TILE and n % ROW_TILE == 0 and b == SUBLANES
    ctx_tiles = n_ctx // ROW_TILE
    c_rows = 2 * SUBLANES
    n_mod = b + SAMPLES_PER_STEP
    c_all = jnp.concatenate([c, jnp.broadcast_to(c_ctx[None, :], (SAMPLES_PER_STEP, D_MODEL)),
                             jnp.zeros((c_rows - n_mod, D_MODEL), F32)], axis=0)
    mod = _modulation(c_all, mod_w, mod_b)

    xa = jnp.concatenate([ctx, x], axis=1)
    out = None
    for layer in range(depth):
        last = layer == depth - 1
        m = mod[layer, :n_mod].reshape(n_mod, 6, 1, D_MODEL)
        sh1, sc1, g1, sh2, sc2, g2 = (m[:, k] for k in range(6))
        w_in_b = w_in[layer].astype(BF16)
        w_out_b = w_out[layer].astype(BF16)
        qkv, rw_in, cv_in = _in_projection(xa, norm1_g[layer], sh1, sc1, w_in_b, ctx_tiles)
        na_l = _neighbourhood_attention(qkv, _na_bias_table(na_rpb[layer]), n_ctx)
        rw_p = dict(mu_prev=rw_mu_prev[layer], mu_next=rw_mu_next[layer], w0=rw_w0[layer], a0=rw_a0[layer],
                    wa=_rw_lora_weights(rw_w2[layer], rw_a2[layer]), g2=rw_g2[layer].astype(BF16),
                    k_k=rw_k_k[layer], k_a=rw_k_a[layer], r_k=rw_r_k[layer].reshape(-1),
                    gn_g=rw_gn_g[layer], gn_b=rw_gn_b[layer])
        rw_o = _bi_rwkv7(rw_in, rw_p, n_ctx)
        cv_o = _conformer_conv(cv_in, cv_dw_w[layer], cv_dw_b[layer], cv_ln_g[layer], cv_ln_b[layer], n_ctx)
        na_c = None if last else _context_attention(qkv, n_ctx)
        xa = _out_projection(xa, na_l, na_c, rw_o, cv_o, w_out_b, g1, n_ctx, with_ctx=not last)
        j = layer // 2
        if layer % 2 == 0:
            ffn_w = (ffn_w_gate[j].astype(BF16), ffn_w_up[j].astype(BF16), ffn_w_down[j].astype(BF16))
            xa = _dense_ffn(xa, norm2_g[layer], sh2, sc2, g2, *ffn_w, 0 if last else ctx_tiles)
            if last:
                out = _final_norm(xa, final_g, 0)
        else:
            if not last:
                raise NotImplementedError("context tokens through a mixture-of-experts layer")
            moe_w = (moe_w_gate[j].astype(BF16), moe_w_up[j].astype(BF16), moe_w_down[j].astype(BF16))
            out = _moe_layer_final(xa, norm2_g[layer], sh2, sc2, g2, moe_router[j], *moe_w, final_g)
    return out
```

```python
import functools

import jax
import jax.numpy as jnp
import numpy as np
from jax import lax
from jax.experimental import pallas as pl
from jax.experimental.pallas import tpu as pltpu

F32 = jnp.float32
BF16 = jnp.bfloat16

D_MODEL = 1024
GRID_W = 64
HEAD_DIM = 64
NA_WIDTH = 512
NA_HEADS = 8
NA_WIN_ROWS = 8
NA_WIN_COLS = 16
RW_WIDTH = 256
RW_HEADS = 4
RW_DECAY_RANK = 64
RW_LORA = 256
RW_GATE_RANK = 128
RW_IN = 1280
RW_GN_EPS = 64e-5
CV_WIDTH = 256
CV_CONV_LEN = 31
QKV_WIDTH = 3 * NA_WIDTH
IN_WIDTH = QKV_WIDTH + RW_IN + 2 * CV_WIDTH
N_EXPERTS = 8
TOP_K = 2
RMS_EPS = 1e-6
LN_EPS = 1e-5

LANES = 128
SUBLANES = 8
VMEM_LIMIT_BYTES = 56 * 1024 * 1024

ROW_TILE = 256
SAMPLES_PER_STEP = 4
MASK_VALUE = -1e30
NA_ROWS_PER_STEP = 8
MOE_ROWS = 1024
MOE_FF_TILE = 512
ROUTER_PAD = LANES
SCAN_STEPS = 64
RW_TILE = 256


def _params(*sem):
    return pltpu.CompilerParams(dimension_semantics=sem, vmem_limit_bytes=VMEM_LIMIT_BYTES)


def _col_chunks(width, step=512):
    out, c = [], 0
    while c < width:
        w = min(step, width - c)
        out.append((c, w))
        c += w
    return out


def _sigmoid(x):
    return 1.0 / (1.0 + jnp.exp(-x))


def _norm_mod(x, g, shift, scale):
    ms = jnp.mean(x * x, axis=-1, keepdims=True)
    h = x * lax.rsqrt(ms + RMS_EPS) * g
    return h * (1.0 + scale) + shift


def _dot_exact_rhs(x, m):
    hi = x.astype(BF16)
    r1 = x - hi.astype(F32)
    mid = r1.astype(BF16)
    lo = (r1 - mid.astype(F32)).astype(BF16)
    dot = functools.partial(jnp.dot, preferred_element_type=F32)
    return dot(hi, m) + dot(mid, m) + dot(lo, m)


def _mod_spec(n_batch, ctx_tiles):
    g = SAMPLES_PER_STEP
    return pl.BlockSpec((g, 1, D_MODEL), lambda b, i: (jnp.where(i < ctx_tiles, n_batch // g, b), 0, 0))


def _row_spec(width, tile_offset=0):
    return pl.BlockSpec((SAMPLES_PER_STEP, ROW_TILE, width), lambda b, i: (b, i + tile_offset, 0))


def _stack(x):
    return x.reshape(x.shape[0] * x.shape[1], x.shape[2])


def _unstack(x):
    return x.reshape(SAMPLES_PER_STEP, x.shape[0] // SAMPLES_PER_STEP, x.shape[1])


def _whole(shape):
    return pl.BlockSpec(shape, lambda *_: (0,) * len(shape), pipeline_mode=pl.Buffered(1))


def _mod_kernel(c_ref, w_ref, b_ref, o_ref):
    c = c_ref[...]
    cs = c * _sigmoid(c)
    o_ref[0] = jnp.dot(cs, w_ref[0], precision=lax.Precision.HIGHEST,
                       preferred_element_type=F32) + b_ref[0]


def _modulation(c_all, mod_w, mod_b):
    depth, d, n = mod_w.shape
    rows = c_all.shape[0]
    tn = 768
    return pl.pallas_call(
        _mod_kernel,
        grid=(depth, n // tn),
        in_specs=[
            pl.BlockSpec((rows, d), lambda l, j: (0, 0)),
            pl.BlockSpec((1, d, tn), lambda l, j: (l, 0, j)),
            pl.BlockSpec((1, 1, tn), lambda l, j: (l, 0, j)),
        ],
        out_specs=pl.BlockSpec((1, rows, tn), lambda l, j: (l, 0, j)),
        out_shape=jax.ShapeDtypeStruct((depth, rows, n), F32),
        compiler_params=_params("arbitrary", "arbitrary"),
        name="modulation",
    )(c_all, mod_w, mod_b.reshape(depth, 1, n))


def _proj_kernel(x_ref, g_ref, sh_ref, sc_ref, w_ref, qkv_ref, rw_ref, cv_ref):
    hb = _stack(_norm_mod(x_ref[...], g_ref[...], sh_ref[...], sc_ref[...])).astype(BF16)
    for ref, base in ((qkv_ref, 0), (rw_ref, QKV_WIDTH), (cv_ref, QKV_WIDTH + RW_IN)):
        for c0, w in _col_chunks(ref.shape[-1]):
            y = jnp.dot(hb, w_ref[:, base + c0:base + c0 + w], preferred_element_type=F32)
            ref[:, :, c0:c0 + w] = _unstack(y.astype(ref.dtype))


def _in_projection(x, g, shift, scale, w_in_bf16, ctx_tiles):
    b, t, d = x.shape
    ms = _mod_spec(b, ctx_tiles)
    return pl.pallas_call(
        _proj_kernel,
        grid=(b // SAMPLES_PER_STEP, t // ROW_TILE),
        in_specs=[_row_spec(d), _whole((1, d)), ms, ms, _whole((d, IN_WIDTH))],
        out_specs=[_row_spec(QKV_WIDTH), _row_spec(RW_IN), _row_spec(2 * CV_WIDTH)],
        out_shape=[
            jax.ShapeDtypeStruct((b, t, QKV_WIDTH), BF16),
            jax.ShapeDtypeStruct((b, t, RW_IN), F32),
            jax.ShapeDtypeStruct((b, t, 2 * CV_WIDTH), F32),
        ],
        compiler_params=_params("arbitrary", "arbitrary"),
        name="in_projection",
    )(x, g.reshape(1, d), shift, scale, w_in_bf16)


def _head_pair_queries(q):
    lane = lax.broadcasted_iota(jnp.int32, q.shape, 1)
    qs = q * jnp.asarray(HEAD_DIM ** -0.5, q.dtype)
    zero = jnp.zeros_like(qs)
    return jnp.concatenate([jnp.where(lane < HEAD_DIM, qs, zero), jnp.where(lane >= HEAD_DIM, qs, zero)], axis=0)


def _head_pair_merge(o, rows):
    lane = lax.broadcasted_iota(jnp.int32, (rows, LANES), 1)
    return jnp.where(lane < HEAD_DIM, o[:rows], o[rows:])


_NT = (((1,), (1,)), ((), ()))


def _na_kernel(q_ref, k_ref, v_ref, bias_ref, o_ref, *, rows, n_ctx):
    n_win = NA_WIN_ROWS * GRID_W
    nq = 2 * GRID_W
    first = pl.program_id(2) * NA_ROWS_PER_STEP
    q0 = pl.multiple_of(n_ctx + first * GRID_W, GRID_W)
    qb_all = jnp.concatenate(
        [_head_pair_queries(q_ref[0, pl.ds(q0 + j * GRID_W, GRID_W), :]) for j in range(NA_ROWS_PER_STEP)], axis=0)
    s_ctx_all = lax.dot_general(qb_all, k_ref[0, 0:n_ctx, :], _NT, preferred_element_type=F32)
    o_win, p_ctx, denom = [], [], []
    for j in range(NA_ROWS_PER_STEP):
        i = first + j
        rs = jnp.clip(i - NA_WIN_ROWS // 2, 0, rows - NA_WIN_ROWS)
        start = pl.multiple_of(n_ctx + rs * GRID_W, GRID_W)
        bias = bias_ref[0, rs - i + NA_WIN_ROWS - 1]
        s_win = lax.dot_general(qb_all[j * nq:(j + 1) * nq], k_ref[0, pl.ds(start, n_win), :], _NT,
                                preferred_element_type=F32) + bias
        s_ctx = s_ctx_all[j * nq:(j + 1) * nq]
        m = jnp.maximum(jnp.max(s_win, axis=-1, keepdims=True), jnp.max(s_ctx, axis=-1, keepdims=True))
        p_win = jnp.exp(s_win - m)
        p_ctx.append(jnp.exp(s_ctx - m))
        denom.append(jnp.sum(p_win, axis=-1, keepdims=True) + jnp.sum(p_ctx[j], axis=-1, keepdims=True))
        o_win.append(jnp.dot(p_win.astype(BF16), v_ref[0, pl.ds(start, n_win), :], preferred_element_type=F32))
    o_ctx_all = jnp.dot(jnp.concatenate(p_ctx, axis=0).astype(BF16), v_ref[0, 0:n_ctx, :],
                        preferred_element_type=F32)
    for j in range(NA_ROWS_PER_STEP):
        o = (o_win[j] + o_ctx_all[j * nq:(j + 1) * nq]) / denom[j]
        o_ref[0, j * GRID_W:(j + 1) * GRID_W, :] = _head_pair_merge(o, GRID_W).astype(o_ref.dtype)


def _na_bias_table(rpb):
    qc = np.arange(GRID_W)[:, None]
    kc = np.arange(GRID_W)[None, :]
    ws = np.clip(qc - NA_WIN_COLS // 2, 0, GRID_W - NA_WIN_COLS)
    mask = (kc >= ws) & (kc < ws + NA_WIN_COLS)
    rel = np.clip(kc - qc + NA_WIN_COLS - 1, 0, 2 * NA_WIN_COLS - 2)
    full = jnp.where(mask[None, None], rpb[:, :, rel].astype(F32), MASK_VALUE)
    dr = np.arange(NA_WIN_ROWS)[:, None] + np.arange(NA_WIN_ROWS)[None, :]
    t = full[:, dr]
    t = t.transpose(0, 1, 3, 2, 4).reshape(NA_HEADS // 2, 2, NA_WIN_ROWS, GRID_W, NA_WIN_ROWS * GRID_W)
    return t.transpose(0, 2, 1, 3, 4).reshape(NA_HEADS // 2, NA_WIN_ROWS, 2 * GRID_W, NA_WIN_ROWS * GRID_W)


def _neighbourhood_attention(qkv, bias_table, n_ctx):
    b, t, _ = qkv.shape
    n = t - n_ctx
    rows = n // GRID_W
    hp = NA_HEADS // 2
    koff, voff = NA_WIDTH // LANES, 2 * NA_WIDTH // LANES
    qrows = NA_ROWS_PER_STEP * GRID_W
    assert rows % NA_ROWS_PER_STEP == 0
    return pl.pallas_call(
        functools.partial(_na_kernel, rows=rows, n_ctx=n_ctx),
        grid=(b, hp, rows // NA_ROWS_PER_STEP),
        in_specs=[
            pl.BlockSpec((1, t, LANES), lambda bi, h, i: (bi, 0, h)),
            pl.BlockSpec((1, t, LANES), lambda bi, h, i: (bi, 0, koff + h)),
            pl.BlockSpec((1, t, LANES), lambda bi, h, i: (bi, 0, voff + h)),
            pl.BlockSpec((1, NA_WIN_ROWS, 2 * GRID_W, NA_WIN_ROWS * GRID_W), lambda bi, h, i: (h, 0, 0, 0)),
        ],
        out_specs=pl.BlockSpec((1, qrows, LANES), lambda bi, h, i: (bi, i, h)),
        out_shape=jax.ShapeDtypeStruct((b, n, NA_WIDTH), BF16),
        compiler_params=_params("arbitrary", "arbitrary", "arbitrary"),
        name="neighbourhood_attention",
    )(qkv, qkv, qkv, bias_table)


def _ctx_attn_kernel(q_ref, k_ref, v_ref, o_ref):
    l = q_ref.shape[1]
    qb = _head_pair_queries(q_ref[0])
    s = lax.dot_general(qb, k_ref[0], _NT, preferred_element_type=F32)
    p = jnp.exp(s - jnp.max(s, axis=-1, keepdims=True))
    o = jnp.dot(p.astype(BF16), v_ref[0], preferred_element_type=F32) / jnp.sum(p, axis=-1, keepdims=True)
    o_ref[0] = _head_pair_merge(o, l).astype(o_ref.dtype)


def _context_attention(qkv, n_ctx):
    b = qkv.shape[0]
    hp = NA_HEADS // 2
    koff, voff = NA_WIDTH // LANES, 2 * NA_WIDTH // LANES
    return pl.pallas_call(
        _ctx_attn_kernel,
        grid=(b, hp),
        in_specs=[
            pl.BlockSpec((1, n_ctx, LANES), lambda bi, h: (bi, 0, h)),
            pl.BlockSpec((1, n_ctx, LANES), lambda bi, h: (bi, 0, koff + h)),
            pl.BlockSpec((1, n_ctx, LANES), lambda bi, h: (bi, 0, voff + h)),
        ],
        out_specs=pl.BlockSpec((1, n_ctx, LANES), lambda bi, h: (bi, 0, h)),
        out_shape=jax.ShapeDtypeStruct((b, n_ctx, NA_WIDTH), BF16),
        compiler_params=_params("arbitrary", "arbitrary"),
        name="context_attention",
    )(qkv, qkv, qkv)


SCAN_GROUPS = ("r", "v", "z", "w0", "k0", "b0", "w1", "k1", "b1")
POST_COLS = 4 * RW_WIDTH


def _softplus(x):
    return jnp.maximum(x, 0.0) + jnp.log(1.0 + jnp.exp(-jnp.abs(x)))


def _rw_prep_kernel(u_ref, up_ref, un_ref, mup_ref, mun_ref, ones_ref, kk_ref, ka_ref, rk_ref,
                    w0_ref, a0_ref, wa_ref, g2_ref, scan_ref, post_ref, *, n_ctx):
    i = pl.program_id(0)
    b = pl.program_id(1)
    tt = u_ref.shape[1]
    n_tok = pl.num_programs(0) * tt
    u = u_ref[0]
    row = lax.broadcasted_iota(jnp.int32, u.shape, 0)
    starts_segment = jnp.logical_or(i == 0, i * tt == n_ctx)
    ends_segment = jnp.logical_or((i + 1) * tt == n_tok, (i + 1) * tt == n_ctx)
    prev_row = jnp.where(starts_segment, 0.0, up_ref[0, SUBLANES - 1:SUBLANES, :])
    next_row = jnp.where(ends_segment, 0.0, un_ref[0, 0:1, :])
    prev = jnp.where(row == 0, prev_row, pltpu.roll(u, 1, 0))
    nxt = jnp.where(row == tt - 1, next_row, pltpu.roll(u, tt - 1, 0))
    us = u + mup_ref[...] * (prev - u) + mun_ref[...] * (nxt - u)

    def put(group, val):
        g = SCAN_GROUPS.index(group)
        for c in range(2):
            scan_ref[2 * g + c, pl.ds(b, tt, stride=SUBLANES), :] = val[:, c * LANES:(c + 1) * LANES]

    ones = ones_ref[...]
    r = us[:, 0:RW_WIDTH]
    k = us[:, RW_WIDTH:2 * RW_WIDTH]
    v = us[:, 2 * RW_WIDTH:3 * RW_WIDTH]
    kk = k * kk_ref[...]
    kk = kk * lax.rsqrt(_dot_exact_rhs(kk * kk, ones) + 1e-12)
    put("r", r)
    put("v", v)
    put("z", -kk)
    for d in range(2):
        lo = 3 * RW_WIDTH + d * RW_LORA
        wa_in = us[:, lo:lo + LANES]
        lane = lax.broadcasted_iota(jnp.int32, wa_in.shape, 1)
        wa_in = jnp.where(lane < RW_DECAY_RANK, jnp.tanh(wa_in), wa_in)
        wa = jnp.dot(wa_in.astype(BF16), wa_ref[d], preferred_element_type=F32)
        log_w = -_softplus(-(w0_ref[d] + wa[:, 0:RW_WIDTH])) - 0.5
        decay = jnp.exp(-jnp.exp(log_w))
        a = _sigmoid(a0_ref[d] + wa[:, RW_WIDTH:2 * RW_WIDTH])
        gate_in = _sigmoid(us[:, lo + LANES:lo + 2 * LANES])
        g = jnp.dot(gate_in.astype(BF16), g2_ref[d], preferred_element_type=F32)
        kd = k * (1.0 + (a - 1.0) * ka_ref[...])
        bonus = _dot_exact_rhs(r * kd * rk_ref[...], ones) * v
        put("w%d" % d, decay)
        put("k%d" % d, kd)
        put("b%d" % d, kk * a)
        post_ref[0, :, 2 * d * RW_WIDTH:(2 * d + 1) * RW_WIDTH] = g
        post_ref[0, :, (2 * d + 1) * RW_WIDTH:(2 * d + 2) * RW_WIDTH] = bonus


def _head_ones():
    h = np.arange(RW_WIDTH) // HEAD_DIM
    return jnp.asarray(h[:, None] == h[None, :], BF16)


def _rw_prep(u, p, n_ctx):
    b, t, _ = u.shape
    tt = RW_TILE
    assert b == SUBLANES and n_ctx % tt == 0 and t % tt == 0
    nt8 = t // SUBLANES
    row = lambda vec, n: vec.reshape(1, n)
    return pl.pallas_call(
        functools.partial(_rw_prep_kernel, n_ctx=n_ctx),
        grid=(t // tt, b),
        in_specs=[
            pl.BlockSpec((1, tt, RW_IN), lambda i, bi: (bi, i, 0)),
            pl.BlockSpec((1, SUBLANES, RW_IN), lambda i, bi: (bi, jnp.maximum(i * (tt // SUBLANES) - 1, 0), 0)),
            pl.BlockSpec((1, SUBLANES, RW_IN), lambda i, bi: (bi, jnp.minimum((i + 1) * (tt // SUBLANES), nt8 - 1), 0)),
            _whole((1, RW_IN)), _whole((1, RW_IN)),
            _whole((RW_WIDTH, RW_WIDTH)),
            _whole((1, RW_WIDTH)), _whole((1, RW_WIDTH)), _whole((1, RW_WIDTH)),
            _whole((2, 1, RW_WIDTH)), _whole((2, 1, RW_WIDTH)),
            _whole((2, LANES, 2 * RW_WIDTH)), _whole((2, RW_GATE_RANK, RW_WIDTH)),
        ],
        out_specs=[
            pl.BlockSpec((2 * len(SCAN_GROUPS), tt * b, LANES), lambda i, bi: (0, i, 0)),
            pl.BlockSpec((1, tt, POST_COLS), lambda i, bi: (bi, i, 0)),
        ],
        out_shape=[
            jax.ShapeDtypeStruct((2 * len(SCAN_GROUPS), t * b, LANES), F32),
            jax.ShapeDtypeStruct((b, t, POST_COLS), F32),
        ],
        compiler_params=_params("arbitrary", "arbitrary"),
        name="rwkv_prep",
    )(u, u, u, row(p["mu_prev"], RW_IN), row(p["mu_next"], RW_IN), _head_ones(),
      row(p["k_k"], RW_WIDTH), row(p["k_a"], RW_WIDTH), row(p["r_k"], RW_WIDTH),
      p["w0"].reshape(2, 1, RW_WIDTH), p["a0"].reshape(2, 1, RW_WIDTH), p["wa"], p["g2"])


def _rw_lora_weights(w2, a2):
    z = jnp.zeros_like(w2)
    top = jnp.concatenate([w2, z], axis=-1)
    bot = jnp.concatenate([z, a2], axis=-1)
    return jnp.concatenate([top, bot], axis=1).astype(BF16)


_QUARTER = LANES // 4
_VROWS = HEAD_DIM // 4


def _chain_tile(f_ref, b_ref, sf, sb):
    pieces = [f_ref[0, pl.ds(sf, SUBLANES), :], f_ref[1, pl.ds(sf, SUBLANES), :],
              b_ref[0, pl.ds(sb, SUBLANES), :], b_ref[1, pl.ds(sb, SUBLANES), :]]
    return jnp.concatenate(pieces * 4, axis=0).T


def _scan_kernel(rf, vf, zf, wf, kf, bf, rb, vb, zb, wb, kb, bb, yf_ref, yb_ref, s_ref, t_ref, v_ref, y_ref):
    steps = rf.shape[1] // SUBLANES
    zero = jnp.zeros((_VROWS, LANES), F32)

    @pl.when(pl.program_id(0) == 0)
    def _():
        s_ref[...] = jnp.zeros_like(s_ref)
        y_ref[...] = jnp.zeros_like(y_ref)

    lane_q = lax.broadcasted_iota(jnp.int32, (_VROWS, LANES), 1) // _QUARTER
    col_q = (lax.broadcasted_iota(jnp.int32, (_QUARTER, LANES), 1) % HEAD_DIM) // _VROWS

    def offsets(s):
        s = jnp.clip(s, 0, steps - 1)
        return pl.multiple_of(s * SUBLANES, SUBLANES), pl.multiple_of((steps - 1 - s) * SUBLANES, SUBLANES)

    def prepare(s, p):
        sf, sb = offsets(s)
        zf_off, zb_off = offsets(s + 1)
        t_ref[p, 0] = _chain_tile(zf, zb, zf_off, zb_off)
        for n, (f, b) in enumerate(((wf, wb), (bf, bb), (kf, kb), (rf, rb))):
            t_ref[p, n + 1] = _chain_tile(f, b, sf, sb)
        vt = _chain_tile(vf, vb, sf, sb)
        for h in range(2):
            v = zero
            for q in range(4):
                lo = h * HEAD_DIM + q * _VROWS
                v = jnp.where(lane_q == q, vt[lo:lo + _VROWS], v)
            v_ref[p, h] = v

    def flush(s, p):
        sf, sb = offsets(s)
        yt = jnp.concatenate([y_ref[p, 0]] * 4 + [y_ref[p, 1]] * 4, axis=0).T
        nat = jnp.zeros((_QUARTER, LANES), F32)
        for q in range(4):
            nat = jnp.where(col_q == q, yt[q * _QUARTER:(q + 1) * _QUARTER], nat)
        yf_ref[0, pl.ds(sf, SUBLANES), :] = nat[0:8]
        yf_ref[1, pl.ds(sf, SUBLANES), :] = nat[8:16]
        yb_ref[0, pl.ds(sb, SUBLANES), :] = nat[16:24]
        yb_ref[1, pl.ds(sb, SUBLANES), :] = nat[24:32]

    def update(p, sz):
        sz_next = []
        for h in range(2):
            base = h * HEAD_DIM
            v = v_ref[p, h]
            y = [zero, zero]
            zn = [zero, zero]
            for k in range(HEAD_DIM):
                row = pl.ds(base + k, 1)
                st = s_ref[base + k] * t_ref[p, 1, row, :] + sz[h] * t_ref[p, 2, row, :] + v * t_ref[p, 3, row, :]
                s_ref[base + k] = st
                y[k % 2] = y[k % 2] + st * t_ref[p, 4, row, :]
                zn[k % 2] = zn[k % 2] + st * t_ref[p, 0, row, :]
            y_ref[p, h] = y[0] + y[1]
            sz_next.append(zn[0] + zn[1])
        return tuple(sz_next)

    f0, b0 = offsets(0)
    t_ref[1, 0] = _chain_tile(zf, zb, f0, b0)
    sz0 = []
    for h in range(2):
        acc = [zero, zero]
        for k in range(HEAD_DIM):
            r = h * HEAD_DIM + k
            acc[k % 2] = acc[k % 2] + s_ref[r] * t_ref[1, 0, pl.ds(r, 1), :]
        sz0.append(acc[0] + acc[1])
    prepare(0, 0)

    def pair(j, sz):
        s = 2 * j
        for p in range(2):
            prepare(s + p + 1, 1 - p)
            sz = update(p, sz)
            flush(s + p - 1, 1 - p)
        return sz

    lax.fori_loop(0, steps // 2, pair, tuple(sz0))
    flush(steps - 1, 1)


def _rwkv7_scan(ops, n_batch, n_ctx):
    rows = ops.shape[1]
    blk = SCAN_STEPS * n_batch
    nb = rows // blk
    nb_c = n_ctx * n_batch // blk

    def mirror(i):
        return jnp.where(i < nb_c, nb_c - 1 - i, nb_c + nb - 1 - i)

    def spec(group, backward):
        g = SCAN_GROUPS.index(group)
        if backward:
            return pl.BlockSpec((2, blk, LANES), lambda i: (g, mirror(i), 0))
        return pl.BlockSpec((2, blk, LANES), lambda i: (g, i, 0))

    fw = [spec(g, False) for g in ("r", "v", "z", "w0", "k0", "b0")]
    bw = [spec(g, True) for g in ("r", "v", "z", "w1", "k1", "b1")]
    out_sds = jax.ShapeDtypeStruct((2, rows, LANES), F32)
    return pl.pallas_call(
        _scan_kernel,
        grid=(nb,),
        in_specs=fw + bw,
        out_specs=[pl.BlockSpec((2, blk, LANES), lambda i: (0, i, 0)),
                   pl.BlockSpec((2, blk, LANES), lambda i: (0, mirror(i), 0))],
        out_shape=[out_sds, out_sds],
        scratch_shapes=[pltpu.VMEM((2 * HEAD_DIM, _VROWS, LANES), F32), pltpu.VMEM((2, 5, LANES, LANES), F32),
                        pltpu.VMEM((2, 2, _VROWS, LANES), F32), pltpu.VMEM((2, 2, _VROWS, LANES), F32)],
        compiler_params=_params("arbitrary"),
        name="rwkv_scan",
    )(*([ops] * 12))


def _rw_post_kernel(yf_ref, yb_ref, post_ref, ones_ref, gg_ref, gb_ref, o_ref):
    b = pl.program_id(1)
    tt = o_ref.shape[1]
    ones = ones_ref[...]
    out = None
    for d, y_ref in enumerate((yf_ref, yb_ref)):
        y = jnp.concatenate([y_ref[0, pl.ds(b, tt, stride=SUBLANES), :],
                             y_ref[1, pl.ds(b, tt, stride=SUBLANES), :]], axis=1)
        mu = _dot_exact_rhs(y, ones) * (1.0 / HEAD_DIM)
        yc = y - mu
        var = _dot_exact_rhs(yc * yc, ones) * (1.0 / HEAD_DIM)
        yn = yc * lax.rsqrt(var + RW_GN_EPS) * gg_ref[...] + gb_ref[...]
        g = post_ref[0, :, 2 * d * RW_WIDTH:(2 * d + 1) * RW_WIDTH]
        bonus = post_ref[0, :, (2 * d + 1) * RW_WIDTH:(2 * d + 2) * RW_WIDTH]
        term = (yn + bonus) * g
        out = term if out is None else out + term
    o_ref[0] = out.astype(o_ref.dtype)


def _rw_post(y_f, y_b, post, gn_g, gn_b):
    b, t, _ = post.shape
    tt = RW_TILE
    yspec = pl.BlockSpec((2, tt * b, LANES), lambda i, bi: (0, i, 0))
    return pl.pallas_call(
        _rw_post_kernel,
        grid=(t // tt, b),
        in_specs=[yspec, yspec, pl.BlockSpec((1, tt, POST_COLS), lambda i, bi: (bi, i, 0)),
                  _whole((RW_WIDTH, RW_WIDTH)), _whole((1, RW_WIDTH)), _whole((1, RW_WIDTH))],
        out_specs=pl.BlockSpec((1, tt, RW_WIDTH), lambda i, bi: (bi, i, 0)),
        out_shape=jax.ShapeDtypeStruct((b, t, RW_WIDTH), BF16),
        compiler_params=_params("arbitrary", "arbitrary"),
        name="rwkv_post",
    )(y_f, y_b, post, _head_ones(), gn_g.reshape(1, RW_WIDTH), gn_b.reshape(1, RW_WIDTH))


def _bi_rwkv7(u, p, n_ctx):
    ops, post = _rw_prep(u, p, n_ctx)
    y_f, y_b = _rwkv7_scan(ops, u.shape[0], n_ctx)
    return _rw_post(y_f, y_b, post, p["gn_g"], p["gn_b"])


CONV_PAD = 16


def _conv_kernel(u_ref, w_ref, b_ref, lg_ref, lb_ref, o_ref, pad_ref, shift_ref, *, n_ctx):
    t = u_ref.shape[1]
    half = CV_CONV_LEN // 2
    chunk = ROW_TILE
    u = u_ref[0]
    h = u[:, :CV_WIDTH] * _sigmoid(u[:, CV_WIDTH:])
    zeros = jnp.zeros((CONV_PAD, CV_WIDTH), F32)
    lat0 = 2 * CONV_PAD + n_ctx
    pad_ref[0:CONV_PAD, :] = zeros
    pad_ref[CONV_PAD:CONV_PAD + n_ctx, :] = h[:n_ctx]
    pad_ref[CONV_PAD + n_ctx:lat0, :] = zeros
    pad_ref[lat0:lat0 + t - n_ctx, :] = h[n_ctx:]
    pad_ref[lat0 + t - n_ctx:lat0 + t - n_ctx + CONV_PAD, :] = zeros
    w = w_ref[...]

    def body(c, carry):
        out0 = pl.multiple_of(c * chunk, chunk)
        base = pl.multiple_of(out0 + jnp.where(c >= n_ctx // chunk, CONV_PAD, 0), CONV_PAD)
        win = pad_ref[pl.ds(base, chunk + 2 * CONV_PAD), :]
        acc = jnp.zeros((chunk, CV_WIDTH), F32)
        for r in range(SUBLANES):
            offs = [o for o in range(CONV_PAD - half, CONV_PAD + half + 1) if o % SUBLANES == r]
            if not offs:
                continue
            shift_ref[r] = win[r:r + shift_ref.shape[1], :]
            for off in offs:
                j = off - (CONV_PAD - half)
                acc = acc + shift_ref[r, off - r:off - r + chunk, :] * w[j:j + 1, :]
        hh = acc + b_ref[...]
        mu = jnp.mean(hh, axis=-1, keepdims=True)
        hc = hh - mu
        var = jnp.mean(hc * hc, axis=-1, keepdims=True)
        y = hc * lax.rsqrt(var + LN_EPS) * lg_ref[...] + lb_ref[...]
        o_ref[0, pl.ds(out0, chunk), :] = (y * _sigmoid(y)).astype(o_ref.dtype)
        return carry

    lax.fori_loop(0, t // chunk, body, 0)


def _conformer_conv(u, dw_w, dw_b, ln_g, ln_b, n_ctx):
    b, t, _ = u.shape
    assert n_ctx % ROW_TILE == 0 and t % ROW_TILE == 0
    row = lambda vec: vec.reshape(1, CV_WIDTH)
    return pl.pallas_call(
        functools.partial(_conv_kernel, n_ctx=n_ctx),
        grid=(b,),
        in_specs=[pl.BlockSpec((1, t, 2 * CV_WIDTH), lambda bi: (bi, 0, 0)),
                  _whole((CV_CONV_LEN, CV_WIDTH)), _whole((1, CV_WIDTH)), _whole((1, CV_WIDTH)),
                  _whole((1, CV_WIDTH))],
        out_specs=pl.BlockSpec((1, t, CV_WIDTH), lambda bi: (bi, 0, 0)),
        out_shape=jax.ShapeDtypeStruct((b, t, CV_WIDTH), BF16),
        scratch_shapes=[pltpu.VMEM((t + 3 * CONV_PAD, CV_WIDTH), F32),
                        pltpu.VMEM((SUBLANES, ROW_TILE + 2 * CONV_PAD - SUBLANES, CV_WIDTH), F32)],
        compiler_params=_params("arbitrary"),
        name="conformer_conv",
    )(u, dw_w, row(dw_b), row(ln_g), row(ln_b))


def _wout_kernel(x_ref, nal_ref, nac_ref, rw_ref, cv_ref, w_ref, gate_ref, o_ref, *, ctx_tiles):
    dot = functools.partial(jnp.dot, preferred_element_type=F32)
    na = nal_ref[...]
    if ctx_tiles:
        na = jnp.where(pl.program_id(1) < ctx_tiles, nac_ref[...], na)
    acc = (dot(_stack(na), w_ref[0:NA_WIDTH, :])
           + dot(_stack(rw_ref[...]), w_ref[NA_WIDTH:NA_WIDTH + RW_WIDTH, :])
           + dot(_stack(cv_ref[...]), w_ref[NA_WIDTH + RW_WIDTH:, :]))
    o_ref[...] = x_ref[...] + gate_ref[...] * _unstack(acc)


def _out_projection(x, na_l, na_c, rw, cv, w_out_bf16, gate, n_ctx, with_ctx):
    b, t, d = x.shape
    ctx_tiles = n_ctx // ROW_TILE
    assert ctx_tiles == 1 or not with_ctx
    off = 0 if with_ctx else ctx_tiles
    n_rows = t if with_ctx else t - n_ctx
    g = SAMPLES_PER_STEP
    if with_ctx:
        nal_spec = pl.BlockSpec((g, ROW_TILE, NA_WIDTH), lambda bi, i: (bi, jnp.maximum(i - ctx_tiles, 0), 0))
    else:
        nal_spec = _row_spec(NA_WIDTH)
        na_c = na_l
    return pl.pallas_call(
        functools.partial(_wout_kernel, ctx_tiles=ctx_tiles if with_ctx else 0),
        grid=(b // g, n_rows // ROW_TILE),
        in_specs=[_row_spec(d, off), nal_spec,
                  pl.BlockSpec((g, ROW_TILE, NA_WIDTH), lambda bi, i: (bi, 0, 0)),
                  _row_spec(RW_WIDTH, off), _row_spec(CV_WIDTH, off),
                  _whole((d, d)), _mod_spec(b, ctx_tiles if with_ctx else 0)],
        out_specs=_row_spec(d),
        out_shape=jax.ShapeDtypeStruct((b, n_rows, d), F32),
        compiler_params=_params("arbitrary", "arbitrary"),
        name="out_projection",
    )(x, na_l, na_c, rw, cv, w_out_bf16, gate)


def _ffn_kernel(x_ref, g_ref, sh_ref, sc_ref, gate_ref, wg_ref, wu_ref, wd_ref, o_ref, acc_ref):
    x = x_ref[...]
    hb = _stack(_norm_mod(x, g_ref[...], sh_ref[...], sc_ref[...])).astype(BF16)
    dot = functools.partial(jnp.dot, preferred_element_type=F32)
    for n, (c0, w) in enumerate(_col_chunks(wg_ref.shape[1])):
        g = dot(hb, wg_ref[:, c0:c0 + w])
        u = dot(hb, wu_ref[:, c0:c0 + w])
        a = (g * _sigmoid(g) * u).astype(BF16)
        y = dot(a, wd_ref[c0:c0 + w, :])
        if n == 0:
            acc_ref[...] = y
        else:
            acc_ref[...] += y
    o_ref[...] = x + gate_ref[...] * _unstack(acc_ref[...])


def _dense_ffn(x, g, shift, scale, gate, wg, wu, wd, ctx_tiles):
    b, t, d = x.shape
    dff = wg.shape[1]
    ms = _mod_spec(b, ctx_tiles)
    return pl.pallas_call(
        _ffn_kernel,
        grid=(b // SAMPLES_PER_STEP, t // ROW_TILE),
        in_specs=[_row_spec(d), _whole((1, d)), ms, ms, ms,
                  _whole((d, dff)), _whole((d, dff)), _whole((dff, d))],
        out_specs=_row_spec(d),
        out_shape=jax.ShapeDtypeStruct((b, t, d), F32),
        scratch_shapes=[pltpu.VMEM((SAMPLES_PER_STEP * ROW_TILE, d), F32)],
        compiler_params=_params("arbitrary", "arbitrary"),
        name="dense_ffn",
    )(x, g.reshape(1, d), shift, scale, gate, wg, wu, wd)


def _router_kernel(x_ref, g_ref, sh_ref, sc_ref, wr_ref, h_ref, logit_ref):
    h = _norm_mod(x_ref[...], g_ref[...], sh_ref[...], sc_ref[...])
    h_ref[...] = h
    logit_ref[...] = _unstack(jnp.dot(_stack(h), wr_ref[...], precision=lax.Precision.HIGHEST,
                                      preferred_element_type=F32))


def _moe_router(x, g, shift, scale, router_pad):
    b, t, d = x.shape
    ms = _mod_spec(b, 0)
    return pl.pallas_call(
        _router_kernel,
        grid=(b // SAMPLES_PER_STEP, t // ROW_TILE),
        in_specs=[_row_spec(d), _whole((1, d)), ms, ms, _whole((d, ROUTER_PAD))],
        out_specs=[_row_spec(d), _row_spec(ROUTER_PAD)],
        out_shape=[jax.ShapeDtypeStruct((b, t, d), F32), jax.ShapeDtypeStruct((b, t, ROUTER_PAD), F32)],
        compiler_params=_params("arbitrary", "arbitrary"),
        name="moe_router",
    )(x, g.reshape(1, d), shift, scale, router_pad)


def _row_copy(src_hbm, dst_vmem, sem, src_row, dst_row):
    return pltpu.make_async_copy(src_hbm.at[pl.ds(src_row, 1)], dst_vmem.at[pl.ds(dst_row, 1)], sem)


GATHER_UNROLL = 8


def _start_row_gather(src_hbm, dst_vmem, sem, index_of_row, rows):
    def body(r, c):
        _row_copy(src_hbm, dst_vmem, sem, index_of_row(r), r).start()
        return c

    lax.fori_loop(0, rows, body, 0, unroll=GATHER_UNROLL)


def _wait_row_gather(src_hbm, dst_vmem, sem, rows):
    def body(r, c):
        _row_copy(src_hbm, dst_vmem, sem, 0, r).wait()
        return c

    lax.fori_loop(0, rows, body, 0, unroll=GATHER_UNROLL)


def _expert_kernel(be_ref, nused_ref, tok_ref, tok_next_ref, h_hbm, wg_ref, wu_ref, wd_ref, o_ref,
                   xg_ref, xb_ref, acc_ref, sem):
    i = pl.program_id(0)
    j = pl.program_id(1)
    rows = xb_ref.shape[0]
    n_used = nused_ref[0]
    used = i < n_used
    slot = i % 2

    @pl.when(jnp.logical_and(i == 0, j == 0))
    def _():
        _start_row_gather(h_hbm, xg_ref.at[0], sem.at[0], lambda r: tok_ref[0, 0, r], rows)

    @pl.when(jnp.logical_and(used, j == 0))
    def _():
        _wait_row_gather(h_hbm, xg_ref.at[slot], sem.at[slot], rows)
        xb_ref[...] = xg_ref[slot].astype(BF16)

    @pl.when(jnp.logical_and(i + 1 < n_used, j == 1))
    def _():
        _start_row_gather(h_hbm, xg_ref.at[1 - slot], sem.at[1 - slot], lambda r: tok_next_ref[0, 0, r], rows)

    @pl.when(used)
    def _():
        dot = functools.partial(jnp.dot, preferred_element_type=F32)
        xb = xb_ref[...]
        g = dot(xb, wg_ref[0])
        u = dot(xb, wu_ref[0])
        a = (g * _sigmoid(g) * u).astype(BF16)
        y = dot(a, wd_ref[0])

        @pl.when(j == 0)
        def _():
            acc_ref[...] = y

        @pl.when(j > 0)
        def _():
            acc_ref[...] += y

    last = j == pl.num_programs(1) - 1

    @pl.when(jnp.logical_and(used, last))
    def _():
        o_ref[...] = acc_ref[...]

    @pl.when(jnp.logical_and(jnp.logical_not(used), last))
    def _():
        o_ref[...] = jnp.zeros_like(o_ref)


def _moe_experts(h_flat, slot_tok, block_e, n_used, wg, wu, wd):
    nb = slot_tok.shape[0]
    d = h_flat.shape[1]
    dff = wg.shape[2]
    nj = dff // MOE_FF_TILE
    assert nj >= 2
    grid_spec = pltpu.PrefetchScalarGridSpec(
        num_scalar_prefetch=2,
        grid=(nb, nj),
        in_specs=[
            pl.BlockSpec((1, 1, MOE_ROWS), lambda i, j, be, nu: (i, 0, 0), memory_space=pltpu.SMEM),
            pl.BlockSpec((1, 1, MOE_ROWS), lambda i, j, be, nu: (jnp.minimum(i + 1, nb - 1), 0, 0),
                         memory_space=pltpu.SMEM),
            pl.BlockSpec(memory_space=pl.ANY),
            pl.BlockSpec((1, d, MOE_FF_TILE), lambda i, j, be, nu: (be[i], 0, j)),
            pl.BlockSpec((1, d, MOE_FF_TILE), lambda i, j, be, nu: (be[i], 0, j)),
            pl.BlockSpec((1, MOE_FF_TILE, d), lambda i, j, be, nu: (be[i], j, 0)),
        ],
        out_specs=pl.BlockSpec((MOE_ROWS, d), lambda i, j, be, nu: (i, 0)),
        scratch_shapes=[pltpu.VMEM((2, MOE_ROWS, d), F32), pltpu.VMEM((MOE_ROWS, d), BF16),
                        pltpu.VMEM((MOE_ROWS, d), F32), pltpu.SemaphoreType.DMA((2,))],
    )
    return pl.pallas_call(
        _expert_kernel,
        grid_spec=grid_spec,
        out_shape=jax.ShapeDtypeStruct((nb * MOE_ROWS, d), F32),
        compiler_params=_params("arbitrary", "arbitrary"),
        name="moe_experts",
    )(block_e, n_used, slot_tok, slot_tok, h_flat, wg, wu, wd)


def _combine_kernel(dest_ref, dest_next_ref, x_ref, gates_ref, gate2_ref, fg_ref, yb_hbm, o_ref, y_ref, sem):
    i = pl.program_id(0)
    rows = x_ref.shape[1]
    slot = i % 2

    @pl.when(i == 0)
    def _():
        _start_row_gather(yb_hbm, y_ref.at[0], sem.at[0], lambda r: dest_ref[0, 0, r], TOP_K * rows)

    @pl.when(i + 1 < pl.num_programs(0))
    def _():
        _start_row_gather(yb_hbm, y_ref.at[1 - slot], sem.at[1 - slot], lambda r: dest_next_ref[0, 0, r],
                          TOP_K * rows)

    _wait_row_gather(yb_hbm, y_ref.at[slot], sem.at[slot], TOP_K * rows)
    gates = gates_ref[0]
    y = y_ref[slot, 0:rows, :] * gates[:, 0:1] + y_ref[slot, rows:2 * rows, :] * gates[:, 1:2]
    x = x_ref[0] + gate2_ref[0] * y
    ms = jnp.mean(x * x, axis=-1, keepdims=True)
    o_ref[0] = x * lax.rsqrt(ms + RMS_EPS) * fg_ref[...]


def _moe_combine_final(x, gates, dest, yb, gate2, final_g):
    nt, tm, d = x.shape
    tiles_per_sample = nt // gate2.shape[0]
    tile = lambda w: pl.BlockSpec((1, tm, w), lambda i: (i, 0, 0))
    return pl.pallas_call(
        _combine_kernel,
        grid=(nt,),
        in_specs=[
            pl.BlockSpec((1, 1, TOP_K * tm), lambda i: (i, 0, 0), memory_space=pltpu.SMEM),
            pl.BlockSpec((1, 1, TOP_K * tm), lambda i: (jnp.minimum(i + 1, nt - 1), 0, 0),
                         memory_space=pltpu.SMEM),
            tile(d), tile(TOP_K),
            pl.BlockSpec((1, 1, d), lambda i: (i // tiles_per_sample, 0, 0)),
            pl.BlockSpec((1, d), lambda i: (0, 0)),
            pl.BlockSpec(memory_space=pl.ANY),
        ],
        out_specs=tile(d),
        out_shape=jax.ShapeDtypeStruct((nt, tm, d), F32),
        scratch_shapes=[pltpu.VMEM((2, TOP_K * tm, d), F32), pltpu.SemaphoreType.DMA((2,))],
        compiler_params=_params("arbitrary"),
        name="moe_combine_final",
    )(dest, dest, x, gates, gate2, final_g.reshape(1, d), yb)


def _moe_layer_final(x, g, shift, scale, gate2, router, wg, wu, wd, final_g):
    b, t, d = x.shape
    n_tok = b * t
    n_asg = n_tok * TOP_K
    router_pad = jnp.pad(router, ((0, 0), (0, ROUTER_PAD - N_EXPERTS)))
    h, logits = _moe_router(x, g, shift, scale, router_pad)
    logits = logits.reshape(n_tok, ROUTER_PAD)[:, :N_EXPERTS]
    top_logit, top_e = lax.top_k(logits, TOP_K)
    gates = jax.nn.softmax(top_logit, axis=-1)
    flat_e = top_e.reshape(-1)
    onehot = (flat_e[:, None] == jnp.arange(N_EXPERTS, dtype=flat_e.dtype)[None, :]).astype(jnp.int32)
    rank = jnp.cumsum(onehot, axis=0) - onehot
    counts = jnp.sum(onehot, axis=0)
    padded = (counts + MOE_ROWS - 1) // MOE_ROWS * MOE_ROWS
    pad_end = jnp.cumsum(padded)
    pad_start = pad_end - padded
    dest = (pad_start[flat_e] + jnp.sum(rank * onehot, axis=1)).astype(jnp.int32)
    n_blocks = (n_asg + MOE_ROWS - 1) // MOE_ROWS + N_EXPERTS
    flat_tok = jnp.arange(n_asg, dtype=jnp.int32) // TOP_K
    slot_tok = jnp.zeros((n_blocks * MOE_ROWS,), jnp.int32).at[dest].set(flat_tok)
    block_e = jnp.minimum(jnp.searchsorted(pad_end, jnp.arange(n_blocks) * MOE_ROWS, side="right"),
                          N_EXPERTS - 1).astype(jnp.int32)
    n_used = (pad_end[-1:] // MOE_ROWS).astype(jnp.int32)
    yb = _moe_experts(h.reshape(n_tok, d), slot_tok.reshape(n_blocks, 1, MOE_ROWS), block_e, n_used, wg, wu, wd)
    nt = n_tok // ROW_TILE
    tile_dest = dest.reshape(nt, ROW_TILE, TOP_K).transpose(0, 2, 1).reshape(nt, 1, TOP_K * ROW_TILE)
    out = _moe_combine_final(x.reshape(nt, ROW_TILE, d), gates.reshape(nt, ROW_TILE, TOP_K), tile_dest, yb,
                             gate2[:b], final_g)
    return out.reshape(b, t, d)


def _final_norm_kernel(x_ref, g_ref, o_ref):
    x = x_ref[...]
    ms = jnp.mean(x * x, axis=-1, keepdims=True)
    o_ref[...] = x * lax.rsqrt(ms + RMS_EPS) * g_ref[...]


def _final_norm(x, g, tile_offset):
    b, t, d = x.shape
    n_rows = t - tile_offset * ROW_TILE
    return pl.pallas_call(
        _final_norm_kernel,
        grid=(b // SAMPLES_PER_STEP, n_rows // ROW_TILE),
        in_specs=[_row_spec(d, tile_offset), _whole((1, d))],
        out_specs=_row_spec(d),
        out_shape=jax.ShapeDtypeStruct((b, n_rows, d), F32),
        compiler_params=_params("arbitrary", "arbitrary"),
        name="final_norm",
    )(x, g.reshape(1, d))


def kernel(x, c, ctx, c_ctx, norm1_g, norm2_g, mod_w, mod_b, w_in, w_out, na_rpb, rw_mu_prev, rw_mu_next, rw_w0, rw_w2, rw_a0, rw_a2, rw_g2, rw_k_k, rw_k_a, rw_r_k, rw_gn_g, rw_gn_b, cv_dw_w, cv_dw_b, cv_ln_g, cv_ln_b, ffn_w_gate, ffn_w_up, ffn_w_down, moe_router, moe_w_gate, moe_w_up, moe_w_down, final_g):
    b, n, _ = x.shape
    n_ctx = ctx.shape[1]
    depth = mod_w.shape[0]
    assert n_ctx == ROW_TILE and n % ROW_TILE == 0 and b == SUBLANES
    ctx_tiles = n_ctx // ROW_TILE
    c_rows = 2 * SUBLANES
    n_mod = b + SAMPLES_PER_STEP
    c_all = jnp.concatenate([c, jnp.broadcast_to(c_ctx[None, :], (SAMPLES_PER_STEP, D_MODEL)),
                             jnp.zeros((c_rows - n_mod, D_MODEL), F32)], axis=0)
    mod = _modulation(c_all, mod_w, mod_b)

    xa = jnp.concatenate([ctx, x], axis=1)
    out = None
    for layer in range(depth):
        last = layer == depth - 1
        m = mod[layer, :n_mod].reshape(n_mod, 6, 1, D_MODEL)
        sh1, sc1, g1, sh2, sc2, g2 = (m[:, k] for k in range(6))
        w_in_b = w_in[layer].astype(BF16)
        w_out_b = w_out[layer].astype(BF16)
        qkv, rw_in, cv_in = _in_projection(xa, norm1_g[layer], sh1, sc1, w_in_b, ctx_tiles)
        na_l = _neighbourhood_attention(qkv, _na_bias_table(na_rpb[layer]), n_ctx)
        rw_p = dict(mu_prev=rw_mu_prev[layer], mu_next=rw_mu_next[layer], w0=rw_w0[layer], a0=rw_a0[layer],
                    wa=_rw_lora_weights(rw_w2[layer], rw_a2[layer]), g2=rw_g2[layer].astype(BF16),
                    k_k=rw_k_k[layer], k_a=rw_k_a[layer], r_k=rw_r_k[layer].reshape(-1),
                    gn_g=rw_gn_g[layer], gn_b=rw_gn_b[layer])
        rw_o = _bi_rwkv7(rw_in, rw_p, n_ctx)
        cv_o = _conformer_conv(cv_in, cv_dw_w[layer], cv_dw_b[layer], cv_ln_g[layer], cv_ln_b[layer], n_ctx)
        na_c = None if last else _context_attention(qkv, n_ctx)
        xa = _out_projection(xa, na_l, na_c, rw_o, cv_o, w_out_b, g1, n_ctx, with_ctx=not last)
        j = layer // 2
        if layer % 2 == 0:
            ffn_w = (ffn_w_gate[j].astype(BF16), ffn_w_up[j].astype(BF16), ffn_w_down[j].astype(BF16))
            xa = _dense_ffn(xa, norm2_g[layer], sh2, sc2, g2, *ffn_w, 0 if last else ctx_tiles)
            if last:
                out = _final_norm(xa, final_g, 0)
        else:
            if not last:
                raise NotImplementedError("context tokens through a mixture-of-experts layer")
            moe_w = (moe_w_gate[j].astype(BF16), moe_w_up[j].astype(BF16), moe_w_down[j].astype(BF16))
            out = _moe_layer_final(xa, norm2_g[layer], sh2, sc2, g2, moe_router[j], *moe_w, final_g)
    return out
```

```python
import functools

import jax
import jax.numpy as jnp
import numpy as np
from jax import lax
from jax.experimental import pallas as pl
from jax.experimental.pallas import tpu as pltpu

F32 = jnp.float32
BF16 = jnp.bfloat16

D_MODEL = 1024
GRID_W = 64
HEAD_DIM = 64
NA_WIDTH = 512
NA_HEADS = 8
NA_WIN_ROWS = 8
NA_WIN_COLS = 16
RW_WIDTH = 256
RW_HEADS = 4
RW_DECAY_RANK = 64
RW_LORA = 256
RW_GATE_RANK = 128
RW_IN = 1280
RW_GN_EPS = 64e-5
CV_WIDTH = 256
CV_CONV_LEN = 31
QKV_WIDTH = 3 * NA_WIDTH
IN_WIDTH = QKV_WIDTH + RW_IN + 2 * CV_WIDTH
N_EXPERTS = 8
TOP_K = 2
RMS_EPS = 1e-6
LN_EPS = 1e-5

LANES = 128
SUBLANES = 8
VMEM_LIMIT_BYTES = 56 * 1024 * 1024

ROW_TILE = 256
SAMPLES_PER_STEP = 4
MASK_VALUE = -1e30
NA_ROWS_PER_STEP = 8
MOE_ROWS = 1024
MOE_FF_TILE = 512
ROUTER_PAD = LANES
SCAN_STEPS = 64
RW_TILE = 256


def _params(*sem):
    return pltpu.CompilerParams(dimension_semantics=sem, vmem_limit_bytes=VMEM_LIMIT_BYTES)


def _col_chunks(width, step=512):
    out, c = [], 0
    while c < width:
        w = min(step, width - c)
        out.append((c, w))
        c += w
    return out


def _sigmoid(x):
    return 1.0 / (1.0 + jnp.exp(-x))


def _norm_mod(x, g, shift, scale):
    ms = jnp.mean(x * x, axis=-1, keepdims=True)
    h = x * lax.rsqrt(ms + RMS_EPS) * g
    return h * (1.0 + scale) + shift


def _dot_exact_rhs(x, m):
    hi = x.astype(BF16)
    r1 = x - hi.astype(F32)
    mid = r1.astype(BF16)
    lo = (r1 - mid.astype(F32)).astype(BF16)
    dot = functools.partial(jnp.dot, preferred_element_type=F32)
    return dot(hi, m) + dot(mid, m) + dot(lo, m)


def _mod_spec(n_batch, ctx_tiles):
    g = SAMPLES_PER_STEP
    return pl.BlockSpec((g, 1, D_MODEL), lambda b, i: (jnp.where(i < ctx_tiles, n_batch // g, b), 0, 0))


def _row_spec(width, tile_offset=0):
    return pl.BlockSpec((SAMPLES_PER_STEP, ROW_TILE, width), lambda b, i: (b, i + tile_offset, 0))


def _stack(x):
    return x.reshape(x.shape[0] * x.shape[1], x.shape[2])


def _unstack(x):
    return x.reshape(SAMPLES_PER_STEP, x.shape[0] // SAMPLES_PER_STEP, x.shape[1])


def _whole(shape):
    return pl.BlockSpec(shape, lambda *_: (0,) * len(shape), pipeline_mode=pl.Buffered(1))


def _mod_kernel(c_ref, w_ref, b_ref, o_ref):
    c = c_ref[...]
    cs = c * _sigmoid(c)
    o_ref[0] = jnp.dot(cs, w_ref[0], precision=lax.Precision.HIGHEST,
                       preferred_element_type=F32) + b_ref[0]


def _modulation(c_all, mod_w, mod_b):
    depth, d, n = mod_w.shape
    rows = c_all.shape[0]
    tn = 768
    return pl.pallas_call(
        _mod_kernel,
        grid=(depth, n // tn),
        in_specs=[
            pl.BlockSpec((rows, d), lambda l, j: (0, 0)),
            pl.BlockSpec((1, d, tn), lambda l, j: (l, 0, j)),
            pl.BlockSpec((1, 1, tn), lambda l, j: (l, 0, j)),
        ],
        out_specs=pl.BlockSpec((1, rows, tn), lambda l, j: (l, 0, j)),
        out_shape=jax.ShapeDtypeStruct((depth, rows, n), F32),
        compiler_params=_params("arbitrary", "arbitrary"),
        name="modulation",
    )(c_all, mod_w, mod_b.reshape(depth, 1, n))


def _proj_kernel(x_ref, g_ref, sh_ref, sc_ref, w_ref, qkv_ref, rw_ref, cv_ref):
    hb = _stack(_norm_mod(x_ref[...], g_ref[...], sh_ref[...], sc_ref[...])).astype(BF16)
    for ref, base in ((qkv_ref, 0), (rw_ref, QKV_WIDTH), (cv_ref, QKV_WIDTH + RW_IN)):
        for c0, w in _col_chunks(ref.shape[-1]):
            y = jnp.dot(hb, w_ref[:, base + c0:base + c0 + w], preferred_element_type=F32)
            ref[:, :, c0:c0 + w] = _unstack(y.astype(ref.dtype))


def _in_projection(x, g, shift, scale, w_in_bf16, ctx_tiles):
    b, t, d = x.shape
    ms = _mod_spec(b, ctx_tiles)
    return pl.pallas_call(
        _proj_kernel,
        grid=(b // SAMPLES_PER_STEP, t // ROW_TILE),
        in_specs=[_row_spec(d), _whole((1, d)), ms, ms, _whole((d, IN_WIDTH))],
        out_specs=[_row_spec(QKV_WIDTH), _row_spec(RW_IN), _row_spec(2 * CV_WIDTH)],
        out_shape=[
            jax.ShapeDtypeStruct((b, t, QKV_WIDTH), BF16),
            jax.ShapeDtypeStruct((b, t, RW_IN), F32),
            jax.ShapeDtypeStruct((b, t, 2 * CV_WIDTH), F32),
        ],
        compiler_params=_params("arbitrary", "arbitrary"),
        name="in_projection",
    )(x, g.reshape(1, d), shift, scale, w_in_bf16)


def _head_pair_queries(q):
    lane = lax.broadcasted_iota(jnp.int32, q.shape, 1)
    qs = q * jnp.asarray(HEAD_DIM ** -0.5, q.dtype)
    zero = jnp.zeros_like(qs)
    return jnp.concatenate([jnp.where(lane < HEAD_DIM, qs, zero), jnp.where(lane >= HEAD_DIM, qs, zero)], axis=0)


def _head_pair_merge(o, rows):
    lane = lax.broadcasted_iota(jnp.int32, (rows, LANES), 1)
    return jnp.where(lane < HEAD_DIM, o[:rows], o[rows:])


_NT = (((1,), (1,)), ((), ()))


def _na_kernel(q_ref, k_ref, v_ref, bias_ref, o_ref, *, rows, n_ctx):
    n_win = NA_WIN_ROWS * GRID_W
    nq = 2 * GRID_W
    first = pl.program_id(2) * NA_ROWS_PER_STEP
    q0 = pl.multiple_of(n_ctx + first * GRID_W, GRID_W)
    qb_all = jnp.concatenate(
        [_head_pair_queries(q_ref[0, pl.ds(q0 + j * GRID_W, GRID_W), :]) for j in range(NA_ROWS_PER_STEP)], axis=0)
    s_ctx_all = lax.dot_general(qb_all, k_ref[0, 0:n_ctx, :], _NT, preferred_element_type=F32)
    o_win, p_ctx, denom = [], [], []
    for j in range(NA_ROWS_PER_STEP):
        i = first + j
        rs = jnp.clip(i - NA_WIN_ROWS // 2, 0, rows - NA_WIN_ROWS)
        start = pl.multiple_of(n_ctx + rs * GRID_W, GRID_W)
        bias = bias_ref[0, rs - i + NA_WIN_ROWS - 1]
        s_win = lax.dot_general(qb_all[j * nq:(j + 1) * nq], k_ref[0, pl.ds(start, n_win), :], _NT,
                                preferred_element_type=F32) + bias
        s_ctx = s_ctx_all[j * nq:(j + 1) * nq]
        m = jnp.maximum(jnp.max(s_win, axis=-1, keepdims=True), jnp.max(s_ctx, axis=-1, keepdims=True))
        p_win = jnp.exp(s_win - m)
        p_ctx.append(jnp.exp(s_ctx - m))
        denom.append(jnp.sum(p_win, axis=-1, keepdims=True) + jnp.sum(p_ctx[j], axis=-1, keepdims=True))
        o_win.append(jnp.dot(p_win.astype(BF16), v_ref[0, pl.ds(start, n_win), :], preferred_element_type=F32))
    o_ctx_all = jnp.dot(jnp.concatenate(p_ctx, axis=0).astype(BF16), v_ref[0, 0:n_ctx, :],
                        preferred_element_type=F32)
    for j in range(NA_ROWS_PER_STEP):
        o = (o_win[j] + o_ctx_all[j * nq:(j + 1) * nq]) / denom[j]
        o_ref[0, j * GRID_W:(j + 1) * GRID_W, :] = _head_pair_merge(o, GRID_W).astype(o_ref.dtype)


def _na_bias_table(rpb):
    qc = np.arange(GRID_W)[:, None]
    kc = np.arange(GRID_W)[None, :]
    ws = np.clip(qc - NA_WIN_COLS // 2, 0, GRID_W - NA_WIN_COLS)
    mask = (kc >= ws) & (kc < ws + NA_WIN_COLS)
    rel = np.clip(kc - qc + NA_WIN_COLS - 1, 0, 2 * NA_WIN_COLS - 2)
    full = jnp.where(mask[None, None], rpb[:, :, rel].astype(F32), MASK_VALUE)
    dr = np.arange(NA_WIN_ROWS)[:, None] + np.arange(NA_WIN_ROWS)[None, :]
    t = full[:, dr]
    t = t.transpose(0, 1, 3, 2, 4).reshape(NA_HEADS // 2, 2, NA_WIN_ROWS, GRID_W, NA_WIN_ROWS * GRID_W)
    return t.transpose(0, 2, 1, 3, 4).reshape(NA_HEADS // 2, NA_WIN_ROWS, 2 * GRID_W, NA_WIN_ROWS * GRID_W)


def _neighbourhood_attention(qkv, bias_table, n_ctx):
    b, t, _ = qkv.shape
    n = t - n_ctx
    rows = n // GRID_W
    hp = NA_HEADS // 2
    koff, voff = NA_WIDTH // LANES, 2 * NA_WIDTH // LANES
    qrows = NA_ROWS_PER_STEP * GRID_W
    assert rows % NA_ROWS_PER_STEP == 0
    return pl.pallas_call(
        functools.partial(_na_kernel, rows=rows, n_ctx=n_ctx),
        grid=(b, hp, rows // NA_ROWS_PER_STEP),
        in_specs=[
            pl.BlockSpec((1, t, LANES), lambda bi, h, i: (bi, 0, h)),
            pl.BlockSpec((1, t, LANES), lambda bi, h, i: (bi, 0, koff + h)),
            pl.BlockSpec((1, t, LANES), lambda bi, h, i: (bi, 0, voff + h)),
            pl.BlockSpec((1, NA_WIN_ROWS, 2 * GRID_W, NA_WIN_ROWS * GRID_W), lambda bi, h, i: (h, 0, 0, 0)),
        ],
        out_specs=pl.BlockSpec((1, qrows, LANES), lambda bi, h, i: (bi, i, h)),
        out_shape=jax.ShapeDtypeStruct((b, n, NA_WIDTH), BF16),
        compiler_params=_params("arbitrary", "arbitrary", "arbitrary"),
        name="neighbourhood_attention",
    )(qkv, qkv, qkv, bias_table)


def _ctx_attn_kernel(q_ref, k_ref, v_ref, o_ref):
    l = q_ref.shape[1]
    qb = _head_pair_queries(q_ref[0])
    s = lax.dot_general(qb, k_ref[0], _NT, preferred_element_type=F32)
    p = jnp.exp(s - jnp.max(s, axis=-1, keepdims=True))
    o = jnp.dot(p.astype(BF16), v_ref[0], preferred_element_type=F32) / jnp.sum(p, axis=-1, keepdims=True)
    o_ref[0] = _head_pair_merge(o, l).astype(o_ref.dtype)


def _context_attention(qkv, n_ctx):
    b = qkv.shape[0]
    hp = NA_HEADS // 2
    koff, voff = NA_WIDTH // LANES, 2 * NA_WIDTH // LANES
    return pl.pallas_call(
        _ctx_attn_kernel,
        grid=(b, hp),
        in_specs=[
            pl.BlockSpec((1, n_ctx, LANES), lambda bi, h: (bi, 0, h)),
            pl.BlockSpec((1, n_ctx, LANES), lambda bi, h: (bi, 0, koff + h)),
            pl.BlockSpec((1, n_ctx, LANES), lambda bi, h: (bi, 0, voff + h)),
        ],
        out_specs=pl.BlockSpec((1, n_ctx, LANES), lambda bi, h: (bi, 0, h)),
        out_shape=jax.ShapeDtypeStruct((b, n_ctx, NA_WIDTH), BF16),
        compiler_params=_params("arbitrary", "arbitrary"),
        name="context_attention",
    )(qkv, qkv, qkv)


SCAN_GROUPS = ("r", "v", "z", "w0", "k0", "b0", "w1", "k1", "b1")
POST_COLS = 4 * RW_WIDTH


def _softplus(x):
    return jnp.maximum(x, 0.0) + jnp.log(1.0 + jnp.exp(-jnp.abs(x)))


def _rw_prep_kernel(u_ref, up_ref, un_ref, mup_ref, mun_ref, ones_ref, kk_ref, ka_ref, rk_ref,
                    w0_ref, a0_ref, wa_ref, g2_ref, scan_ref, post_ref, *, n_ctx):
    i = pl.program_id(0)
    b = pl.program_id(1)
    tt = u_ref.shape[1]
    n_tok = pl.num_programs(0) * tt
    u = u_ref[0]
    row = lax.broadcasted_iota(jnp.int32, u.shape, 0)
    starts_segment = jnp.logical_or(i == 0, i * tt == n_ctx)
    ends_segment = jnp.logical_or((i + 1) * tt == n_tok, (i + 1) * tt == n_ctx)
    prev_row = jnp.where(starts_segment, 0.0, up_ref[0, SUBLANES - 1:SUBLANES, :])
    next_row = jnp.where(ends_segment, 0.0, un_ref[0, 0:1, :])
    prev = jnp.where(row == 0, prev_row, pltpu.roll(u, 1, 0))
    nxt = jnp.where(row == tt - 1, next_row, pltpu.roll(u, tt - 1, 0))
    us = u + mup_ref[...] * (prev - u) + mun_ref[...] * (nxt - u)

    def put(group, val):
        g = SCAN_GROUPS.index(group)
        for c in range(2):
            scan_ref[2 * g + c, pl.ds(b, tt, stride=SUBLANES), :] = val[:, c * LANES:(c + 1) * LANES]

    ones = ones_ref[...]
    r = us[:, 0:RW_WIDTH]
    k = us[:, RW_WIDTH:2 * RW_WIDTH]
    v = us[:, 2 * RW_WIDTH:3 * RW_WIDTH]
    kk = k * kk_ref[...]
    kk = kk * lax.rsqrt(_dot_exact_rhs(kk * kk, ones) + 1e-12)
    put("r", r)
    put("v", v)
    put("z", -kk)
    for d in range(2):
        lo = 3 * RW_WIDTH + d * RW_LORA
        wa_in = us[:, lo:lo + LANES]
        lane = lax.broadcasted_iota(jnp.int32, wa_in.shape, 1)
        wa_in = jnp.where(lane < RW_DECAY_RANK, jnp.tanh(wa_in), wa_in)
        wa = jnp.dot(wa_in.astype(BF16), wa_ref[d], preferred_element_type=F32)
        log_w = -_softplus(-(w0_ref[d] + wa[:, 0:RW_WIDTH])) - 0.5
        decay = jnp.exp(-jnp.exp(log_w))
        a = _sigmoid(a0_ref[d] + wa[:, RW_WIDTH:2 * RW_WIDTH])
        gate_in = _sigmoid(us[:, lo + LANES:lo + 2 * LANES])
        g = jnp.dot(gate_in.astype(BF16), g2_ref[d], preferred_element_type=F32)
        kd = k * (1.0 + (a - 1.0) * ka_ref[...])
        bonus = _dot_exact_rhs(r * kd * rk_ref[...], ones) * v
        put("w%d" % d, decay)
        put("k%d" % d, kd)
        put("b%d" % d, kk * a)
        post_ref[0, :, 2 * d * RW_WIDTH:(2 * d + 1) * RW_WIDTH] = g
        post_ref[0, :, (2 * d + 1) * RW_WIDTH:(2 * d + 2) * RW_WIDTH] = bonus


def _head_ones():
    h = np.arange(RW_WIDTH) // HEAD_DIM
    return jnp.asarray(h[:, None] == h[None, :], BF16)


def _rw_prep(u, p, n_ctx):
    b, t, _ = u.shape
    tt = RW_TILE
    assert b == SUBLANES and n_ctx % tt == 0 and t % tt == 0
    nt8 = t // SUBLANES
    row = lambda vec, n: vec.reshape(1, n)
    return pl.pallas_call(
        functools.partial(_rw_prep_kernel, n_ctx=n_ctx),
        grid=(t // tt, b),
        in_specs=[
            pl.BlockSpec((1, tt, RW_IN), lambda i, bi: (bi, i, 0)),
            pl.BlockSpec((1, SUBLANES, RW_IN), lambda i, bi: (bi, jnp.maximum(i * (tt // SUBLANES) - 1, 0), 0)),
            pl.BlockSpec((1, SUBLANES, RW_IN), lambda i, bi: (bi, jnp.minimum((i + 1) * (tt // SUBLANES), nt8 - 1), 0)),
            _whole((1, RW_IN)), _whole((1, RW_IN)),
            _whole((RW_WIDTH, RW_WIDTH)),
            _whole((1, RW_WIDTH)), _whole((1, RW_WIDTH)), _whole((1, RW_WIDTH)),
            _whole((2, 1, RW_WIDTH)), _whole((2, 1, RW_WIDTH)),
            _whole((2, LANES, 2 * RW_WIDTH)), _whole((2, RW_GATE_RANK, RW_WIDTH)),
        ],
        out_specs=[
            pl.BlockSpec((2 * len(SCAN_GROUPS), tt * b, LANES), lambda i, bi: (0, i, 0)),
            pl.BlockSpec((1, tt, POST_COLS), lambda i, bi: (bi, i, 0)),
        ],
        out_shape=[
            jax.ShapeDtypeStruct((2 * len(SCAN_GROUPS), t * b, LANES), F32),
            jax.ShapeDtypeStruct((b, t, POST_COLS), F32),
        ],
        compiler_params=_params("arbitrary", "arbitrary"),
        name="rwkv_prep",
    )(u, u, u, row(p["mu_prev"], RW_IN), row(p["mu_next"], RW_IN), _head_ones(),
      row(p["k_k"], RW_WIDTH), row(p["k_a"], RW_WIDTH), row(p["r_k"], RW_WIDTH),
      p["w0"].reshape(2, 1, RW_WIDTH), p["a0"].reshape(2, 1, RW_WIDTH), p["wa"], p["g2"])


def _rw_lora_weights(w2, a2):
    z = jnp.zeros_like(w2)
    top = jnp.concatenate([w2, z], axis=-1)
    bot = jnp.concatenate([z, a2], axis=-1)
    return jnp.concatenate([top, bot], axis=1).astype(BF16)


_QUARTER = LANES // 4
_VROWS = HEAD_DIM // 4


def _chain_tile(f_ref, b_ref, sf, sb):
    pieces = [f_ref[0, pl.ds(sf, SUBLANES), :], f_ref[1, pl.ds(sf, SUBLANES), :],
              b_ref[0, pl.ds(sb, SUBLANES), :], b_ref[1, pl.ds(sb, SUBLANES), :]]
    return jnp.concatenate(pieces * 4, axis=0).T


def _scan_kernel(rf, vf, zf, wf, kf, bf, rb, vb, zb, wb, kb, bb, yf_ref, yb_ref, s_ref, t_ref, v_ref, y_ref):
    steps = rf.shape[1] // SUBLANES
    zero = jnp.zeros((_VROWS, LANES), F32)

    @pl.when(pl.program_id(0) == 0)
    def _():
        s_ref[...] = jnp.zeros_like(s_ref)
        y_ref[...] = jnp.zeros_like(y_ref)

    lane_q = lax.broadcasted_iota(jnp.int32, (_VROWS, LANES), 1) // _QUARTER
    col_q = (lax.broadcasted_iota(jnp.int32, (_QUARTER, LANES), 1) % HEAD_DIM) // _VROWS

    def offsets(s):
        s = jnp.clip(s, 0, steps - 1)
        return pl.multiple_of(s * SUBLANES, SUBLANES), pl.multiple_of((steps - 1 - s) * SUBLANES, SUBLANES)

    def prepare(s, p):
        sf, sb = offsets(s)
        zf_off, zb_off = offsets(s + 1)
        t_ref[p, 0] = _chain_tile(zf, zb, zf_off, zb_off)
        for n, (f, b) in enumerate(((wf, wb), (bf, bb), (kf, kb), (rf, rb))):
            t_ref[p, n + 1] = _chain_tile(f, b, sf, sb)
        vt = _chain_tile(vf, vb, sf, sb)
        for h in range(2):
            v = zero
            for q in range(4):
                lo = h * HEAD_DIM + q * _VROWS
                v = jnp.where(lane_q == q, vt[lo:lo + _VROWS], v)
            v_ref[p, h] = v

    def flush(s, p):
        sf, sb = offsets(s)
        yt = jnp.concatenate([y_ref[p, 0]] * 4 + [y_ref[p, 1]] * 4, axis=0).T
        nat = jnp.zeros((_QUARTER, LANES), F32)
        for q in range(4):
            nat = jnp.where(col_q == q, yt[q * _QUARTER:(q + 1) * _QUARTER], nat)
        yf_ref[0, pl.ds(sf, SUBLANES), :] = nat[0:8]
        yf_ref[1, pl.ds(sf, SUBLANES), :] = nat[8:16]
        yb_ref[0, pl.ds(sb, SUBLANES), :] = nat[16:24]
        yb_ref[1, pl.ds(sb, SUBLANES), :] = nat[24:32]

    def update(p, sz):
        sz_next = []
        for h in range(2):
            base = h * HEAD_DIM
            v = v_ref[p, h]
            y = [zero, zero]
            zn = [zero, zero]
            for k in range(HEAD_DIM):
                row = pl.ds(base + k, 1)
                st = s_ref[base + k] * t_ref[p, 1, row, :] + sz[h] * t_ref[p, 2, row, :] + v * t_ref[p, 3, row, :]
                s_ref[base + k] = st
                y[k % 2] = y[k % 2] + st * t_ref[p, 4, row, :]
                zn[k % 2] = zn[k % 2] + st * t_ref[p, 0, row, :]
            y_ref[p, h] = y[0] + y[1]
            sz_next.append(zn[0] + zn[1])
        return tuple(sz_next)

    f0, b0 = offsets(0)
    t_ref[1, 0] = _chain_tile(zf, zb, f0, b0)
    sz0 = []
    for h in range(2):
        acc = [zero, zero]
        for k in range(HEAD_DIM):
            r = h * HEAD_DIM + k
            acc[k % 2] = acc[k % 2] + s_ref[r] * t_ref[1, 0, pl.ds(r, 1), :]
        sz0.append(acc[0] + acc[1])
    prepare(0, 0)

    def pair(j, sz):
        s = 2 * j
        for p in range(2):
            prepare(s + p + 1, 1 - p)
            sz = update(p, sz)
            flush(s + p - 1, 1 - p)
        return sz

    lax.fori_loop(0, steps // 2, pair, tuple(sz0))
    flush(steps - 1, 1)


def _rwkv7_scan(ops, n_batch, n_ctx):
    rows = ops.shape[1]
    blk = SCAN_STEPS * n_batch
    nb = rows // blk
    nb_c = n_ctx * n_batch // blk

    def mirror(i):
        return jnp.where(i < nb_c, nb_c - 1 - i, nb_c + nb - 1 - i)

    def spec(group, backward):
        g = SCAN_GROUPS.index(group)
        if backward:
            return pl.BlockSpec((2, blk, LANES), lambda i: (g, mirror(i), 0))
        return pl.BlockSpec((2, blk, LANES), lambda i: (g, i, 0))

    fw = [spec(g, False) for g in ("r", "v", "z", "w0", "k0", "b0")]
    bw = [spec(g, True) for g in ("r", "v", "z", "w1", "k1", "b1")]
    out_sds = jax.ShapeDtypeStruct((2, rows, LANES), F32)
    return pl.pallas_call(
        _scan_kernel,
        grid=(nb,),
        in_specs=fw + bw,
        out_specs=[pl.BlockSpec((2, blk, LANES), lambda i: (0, i, 0)),
                   pl.BlockSpec((2, blk, LANES), lambda i: (0, mirror(i), 0))],
        out_shape=[out_sds, out_sds],
        scratch_shapes=[pltpu.VMEM((2 * HEAD_DIM, _VROWS, LANES), F32), pltpu.VMEM((2, 5, LANES, LANES), F32),
                        pltpu.VMEM((2, 2, _VROWS, LANES), F32), pltpu.VMEM((2, 2, _VROWS, LANES), F32)],
        compiler_params=_params("arbitrary"),
        name="rwkv_scan",
    )(*([ops] * 12))


def _rw_post_kernel(yf_ref, yb_ref, post_ref, ones_ref, gg_ref, gb_ref, o_ref):
    b = pl.program_id(1)
    tt = o_ref.shape[1]
    ones = ones_ref[...]
    out = None
    for d, y_ref in enumerate((yf_ref, yb_ref)):
        y = jnp.concatenate([y_ref[0, pl.ds(b, tt, stride=SUBLANES), :],
                             y_ref[1, pl.ds(b, tt, stride=SUBLANES), :]], axis=1)
        mu = _dot_exact_rhs(y, ones) * (1.0 / HEAD_DIM)
        yc = y - mu
        var = _dot_exact_rhs(yc * yc, ones) * (1.0 / HEAD_DIM)
        yn = yc * lax.rsqrt(var + RW_GN_EPS) * gg_ref[...] + gb_ref[...]
        g = post_ref[0, :, 2 * d * RW_WIDTH:(2 * d + 1) * RW_WIDTH]
        bonus = post_ref[0, :, (2 * d + 1) * RW_WIDTH:(2 * d + 2) * RW_WIDTH]
        term = (yn + bonus) * g
        out = term if out is None else out + term
    o_ref[0] = out.astype(o_ref.dtype)


def _rw_post(y_f, y_b, post, gn_g, gn_b):
    b, t, _ = post.shape
    tt = RW_TILE
    yspec = pl.BlockSpec((2, tt * b, LANES), lambda i, bi: (0, i, 0))
    return pl.pallas_call(
        _rw_post_kernel,
        grid=(t // tt, b),
        in_specs=[yspec, yspec, pl.BlockSpec((1, tt, POST_COLS), lambda i, bi: (bi, i, 0)),
                  _whole((RW_WIDTH, RW_WIDTH)), _whole((1, RW_WIDTH)), _whole((1, RW_WIDTH))],
        out_specs=pl.BlockSpec((1, tt, RW_WIDTH), lambda i, bi: (bi, i, 0)),
        out_shape=jax.ShapeDtypeStruct((b, t, RW_WIDTH), BF16),
        compiler_params=_params("arbitrary", "arbitrary"),
        name="rwkv_post",
    )(y_f, y_b, post, _head_ones(), gn_g.reshape(1, RW_WIDTH), gn_b.reshape(1, RW_WIDTH))


def _bi_rwkv7(u, p, n_ctx):
    ops, post = _rw_prep(u, p, n_ctx)
    y_f, y_b = _rwkv7_scan(ops, u.shape[0], n_ctx)
    return _rw_post(y_f, y_b, post, p["gn_g"], p["gn_b"])


CONV_PAD = 16


def _conv_kernel(u_ref, w_ref, b_ref, lg_ref, lb_ref, o_ref, pad_ref, shift_ref, *, n_ctx):
    t = u_ref.shape[1]
    half = CV_CONV_LEN // 2
    chunk = ROW_TILE
    u = u_ref[0]
    h = u[:, :CV_WIDTH] * _sigmoid(u[:, CV_WIDTH:])
    zeros = jnp.zeros((CONV_PAD, CV_WIDTH), F32)
    lat0 = 2 * CONV_PAD + n_ctx
    pad_ref[0:CONV_PAD, :] = zeros
    pad_ref[CONV_PAD:CONV_PAD + n_ctx, :] = h[:n_ctx]
    pad_ref[CONV_PAD + n_ctx:lat0, :] = zeros
    pad_ref[lat0:lat0 + t - n_ctx, :] = h[n_ctx:]
    pad_ref[lat0 + t - n_ctx:lat0 + t - n_ctx + CONV_PAD, :] = zeros
    w = w_ref[...]

    def body(c, carry):
        out0 = pl.multiple_of(c * chunk, chunk)
        base = pl.multiple_of(out0 + jnp.where(c >= n_ctx // chunk, CONV_PAD, 0), CONV_PAD)
        win = pad_ref[pl.ds(base, chunk + 2 * CONV_PAD), :]
        acc = jnp.zeros((chunk, CV_WIDTH), F32)
        for r in range(SUBLANES):
            offs = [o for o in range(CONV_PAD - half, CONV_PAD + half + 1) if o % SUBLANES == r]
            if not offs:
                continue
            shift_ref[r] = win[r:r + shift_ref.shape[1], :]
            for off in offs:
                j = off - (CONV_PAD - half)
                acc = acc + shift_ref[r, off - r:off - r + chunk, :] * w[j:j + 1, :]
        hh = acc + b_ref[...]
        mu = jnp.mean(hh, axis=-1, keepdims=True)
        hc = hh - mu
        var = jnp.mean(hc * hc, axis=-1, keepdims=True)
        y = hc * lax.rsqrt(var + LN_EPS) * lg_ref[...] + lb_ref[...]
        o_ref[0, pl.ds(out0, chunk), :] = (y * _sigmoid(y)).astype(o_ref.dtype)
        return carry

    lax.fori_loop(0, t // chunk, body, 0)


def _conformer_conv(u, dw_w, dw_b, ln_g, ln_b, n_ctx):
    b, t, _ = u.shape
    assert n_ctx % ROW_TILE == 0 and t % ROW_TILE == 0
    row = lambda vec: vec.reshape(1, CV_WIDTH)
    return pl.pallas_call(
        functools.partial(_conv_kernel, n_ctx=n_ctx),
        grid=(b,),
        in_specs=[pl.BlockSpec((1, t, 2 * CV_WIDTH), lambda bi: (bi, 0, 0)),
                  _whole((CV_CONV_LEN, CV_WIDTH)), _whole((1, CV_WIDTH)), _whole((1, CV_WIDTH)),
                  _whole((1, CV_WIDTH))],
        out_specs=pl.BlockSpec((1, t, CV_WIDTH), lambda bi: (bi, 0, 0)),
        out_shape=jax.ShapeDtypeStruct((b, t, CV_WIDTH), BF16),
        scratch_shapes=[pltpu.VMEM((t + 3 * CONV_PAD, CV_WIDTH), F32),
                        pltpu.VMEM((SUBLANES, ROW_TILE + 2 * CONV_PAD - SUBLANES, CV_WIDTH), F32)],
        compiler_params=_params("arbitrary"),
        name="conformer_conv",
    )(u, dw_w, row(dw_b), row(ln_g), row(ln_b))


def _wout_kernel(x_ref, nal_ref, nac_ref, rw_ref, cv_ref, w_ref, gate_ref, o_ref, *, ctx_tiles):
    dot = functools.partial(jnp.dot, preferred_element_type=F32)
    na = nal_ref[...]
    if ctx_tiles:
        na = jnp.where(pl.program_id(1) < ctx_tiles, nac_ref[...], na)
    acc = (dot(_stack(na), w_ref[0:NA_WIDTH, :])
           + dot(_stack(rw_ref[...]), w_ref[NA_WIDTH:NA_WIDTH + RW_WIDTH, :])
           + dot(_stack(cv_ref[...]), w_ref[NA_WIDTH + RW_WIDTH:, :]))
    o_ref[...] = x_ref[...] + gate_ref[...] * _unstack(acc)


def _out_projection(x, na_l, na_c, rw, cv, w_out_bf16, gate, n_ctx, with_ctx):
    b, t, d = x.shape
    ctx_tiles = n_ctx // ROW_TILE
    assert ctx_tiles == 1 or not with_ctx
    off = 0 if with_ctx else ctx_tiles
    n_rows = t if with_ctx else t - n_ctx
    g = SAMPLES_PER_STEP
    if with_ctx:
        nal_spec = pl.BlockSpec((g, ROW_TILE, NA_WIDTH), lambda bi, i: (bi, jnp.maximum(i - ctx_tiles, 0), 0))
    else:
        nal_spec = _row_spec(NA_WIDTH)
        na_c = na_l
    return pl.pallas_call(
        functools.partial(_wout_kernel, ctx_tiles=ctx_tiles if with_ctx else 0),
        grid=(b // g, n_rows // ROW_TILE),
        in_specs=[_row_spec(d, off), nal_spec,
                  pl.BlockSpec((g, ROW_TILE, NA_WIDTH), lambda bi, i: (bi, 0, 0)),
                  _row_spec(RW_WIDTH, off), _row_spec(CV_WIDTH, off),
                  _whole((d, d)), _mod_spec(b, ctx_tiles if with_ctx else 0)],
        out_specs=_row_spec(d),
        out_shape=jax.ShapeDtypeStruct((b, n_rows, d), F32),
        compiler_params=_params("arbitrary", "arbitrary"),
        name="out_projection",
    )(x, na_l, na_c, rw, cv, w_out_bf16, gate)


def _ffn_kernel(x_ref, g_ref, sh_ref, sc_ref, gate_ref, wg_ref, wu_ref, wd_ref, o_ref, acc_ref):
    x = x_ref[...]
    hb = _stack(_norm_mod(x, g_ref[...], sh_ref[...], sc_ref[...])).astype(BF16)
    dot = functools.partial(jnp.dot, preferred_element_type=F32)
    for n, (c0, w) in enumerate(_col_chunks(wg_ref.shape[1])):
        g = dot(hb, wg_ref[:, c0:c0 + w])
        u = dot(hb, wu_ref[:, c0:c0 + w])
        a = (g * _sigmoid(g) * u).astype(BF16)
        y = dot(a, wd_ref[c0:c0 + w, :])
        if n == 0:
            acc_ref[...] = y
        else:
            acc_ref[...] += y
    o_ref[...] = x + gate_ref[...] * _unstack(acc_ref[...])


def _dense_ffn(x, g, shift, scale, gate, wg, wu, wd, ctx_tiles):
    b, t, d = x.shape
    dff = wg.shape[1]
    ms = _mod_spec(b, ctx_tiles)
    return pl.pallas_call(
        _ffn_kernel,
        grid=(b // SAMPLES_PER_STEP, t // ROW_TILE),
        in_specs=[_row_spec(d), _whole((1, d)), ms, ms, ms,
                  _whole((d, dff)), _whole((d, dff)), _whole((dff, d))],
        out_specs=_row_spec(d),
        out_shape=jax.ShapeDtypeStruct((b, t, d), F32),
        scratch_shapes=[pltpu.VMEM((SAMPLES_PER_STEP * ROW_TILE, d), F32)],
        compiler_params=_params("arbitrary", "arbitrary"),
        name="dense_ffn",
    )(x, g.reshape(1, d), shift, scale, gate, wg, wu, wd)


def _router_kernel(x_ref, g_ref, sh_ref, sc_ref, wr_ref, h_ref, logit_ref):
    h = _norm_mod(x_ref[...], g_ref[...], sh_ref[...], sc_ref[...])
    h_ref[...] = h
    logit_ref[...] = _unstack(jnp.dot(_stack(h), wr_ref[...], precision=lax.Precision.HIGHEST,
                                      preferred_element_type=F32))


def _moe_router(x, g, shift, scale, router_pad):
    b, t, d = x.shape
    ms = _mod_spec(b, 0)
    return pl.pallas_call(
        _router_kernel,
        grid=(b // SAMPLES_PER_STEP, t // ROW_TILE),
        in_specs=[_row_spec(d), _whole((1, d)), ms, ms, _whole((d, ROUTER_PAD))],
        out_specs=[_row_spec(d), _row_spec(ROUTER_PAD)],
        out_shape=[jax.ShapeDtypeStruct((b, t, d), F32), jax.ShapeDtypeStruct((b, t, ROUTER_PAD), F32)],
        compiler_params=_params("arbitrary", "arbitrary"),
        name="moe_router",
    )(x, g.reshape(1, d), shift, scale, router_pad)


def _row_copy(src_hbm, dst_vmem, sem, src_row, dst_row):
    return pltpu.make_async_copy(src_hbm.at[pl.ds(src_row, 1)], dst_vmem.at[pl.ds(dst_row, 1)], sem)


GATHER_UNROLL = 8


def _start_row_gather(src_hbm, dst_vmem, sem, index_of_row, rows):
    def body(r, c):
        _row_copy(src_hbm, dst_vmem, sem, index_of_row(r), r).start()
        return c

    lax.fori_loop(0, rows, body, 0, unroll=GATHER_UNROLL)


def _wait_row_gather(src_hbm, dst_vmem, sem, rows):
    def body(r, c):
        _row_copy(src_hbm, dst_vmem, sem, 0, r).wait()
        return c

    lax.fori_loop(0, rows, body, 0, unroll=GATHER_UNROLL)


def _expert_kernel(be_ref, nused_ref, tok_ref, tok_next_ref, h_hbm, wg_ref, wu_ref, wd_ref, o_ref, xg_ref, sem):
    i = pl.program_id(0)
    rows = o_ref.shape[0]
    n_used = nused_ref[0]
    used = i < n_used
    slot = i % 2

    @pl.when(i == 0)
    def _():
        _start_row_gather(h_hbm, xg_ref.at[0], sem.at[0], lambda r: tok_ref[0, 0, r], rows)

    @pl.when(i + 1 < n_used)
    def _():
        _start_row_gather(h_hbm, xg_ref.at[1 - slot], sem.at[1 - slot], lambda r: tok_next_ref[0, 0, r], rows)

    @pl.when(used)
    def _():
        _wait_row_gather(h_hbm, xg_ref.at[slot], sem.at[slot], rows)
        dot = functools.partial(jnp.dot, preferred_element_type=F32)
        xb = xg_ref[slot].astype(BF16)
        for n, (c0, w) in enumerate(_col_chunks(wg_ref.shape[2], MOE_FF_TILE)):
            g = dot(xb, wg_ref[0, :, c0:c0 + w])
            u = dot(xb, wu_ref[0, :, c0:c0 + w])
            a = (g * _sigmoid(g) * u).astype(BF16)
            y = dot(a, wd_ref[0, c0:c0 + w, :])
            if n == 0:
                o_ref[...] = y
            else:
                o_ref[...] += y

    @pl.when(jnp.logical_not(used))
    def _():
        o_ref[...] = jnp.zeros_like(o_ref)


def _moe_experts(h_flat, slot_tok, block_e, n_used, wg, wu, wd):
    nb = slot_tok.shape[0]
    d = h_flat.shape[1]
    dff = wg.shape[2]
    once = pl.Buffered(1)
    grid_spec = pltpu.PrefetchScalarGridSpec(
        num_scalar_prefetch=2,
        grid=(nb,),
        in_specs=[
            pl.BlockSpec((1, 1, MOE_ROWS), lambda i, be, nu: (i, 0, 0), memory_space=pltpu.SMEM),
            pl.BlockSpec((1, 1, MOE_ROWS), lambda i, be, nu: (jnp.minimum(i + 1, nb - 1), 0, 0),
                         memory_space=pltpu.SMEM),
            pl.BlockSpec(memory_space=pl.ANY),
            pl.BlockSpec((1, d, dff), lambda i, be, nu: (be[i], 0, 0), pipeline_mode=once),
            pl.BlockSpec((1, d, dff), lambda i, be, nu: (be[i], 0, 0), pipeline_mode=once),
            pl.BlockSpec((1, dff, d), lambda i, be, nu: (be[i], 0, 0), pipeline_mode=once),
        ],
        out_specs=pl.BlockSpec((MOE_ROWS, d), lambda i, be, nu: (i, 0)),
        scratch_shapes=[pltpu.VMEM((2, MOE_ROWS, d), F32), pltpu.SemaphoreType.DMA((2,))],
    )
    return pl.pallas_call(
        _expert_kernel,
        grid_spec=grid_spec,
        out_shape=jax.ShapeDtypeStruct((nb * MOE_ROWS, d), F32),
        compiler_params=_params("arbitrary"),
        name="moe_experts",
    )(block_e, n_used, slot_tok, slot_tok, h_flat, wg, wu, wd)


def _combine_kernel(dest_ref, dest_next_ref, x_ref, gates_ref, gate2_ref, fg_ref, yb_hbm, o_ref, y_ref, sem):
    i = pl.program_id(0)
    rows = x_ref.shape[1]
    slot = i % 2

    @pl.when(i == 0)
    def _():
        _start_row_gather(yb_hbm, y_ref.at[0], sem.at[0], lambda r: dest_ref[0, 0, r], TOP_K * rows)

    @pl.when(i + 1 < pl.num_programs(0))
    def _():
        _start_row_gather(yb_hbm, y_ref.at[1 - slot], sem.at[1 - slot], lambda r: dest_next_ref[0, 0, r],
                          TOP_K * rows)

    _wait_row_gather(yb_hbm, y_ref.at[slot], sem.at[slot], TOP_K * rows)
    gates = gates_ref[0]
    y = y_ref[slot, 0:rows, :] * gates[:, 0:1] + y_ref[slot, rows:2 * rows, :] * gates[:, 1:2]
    x = x_ref[0] + gate2_ref[0] * y
    ms = jnp.mean(x * x, axis=-1, keepdims=True)
    o_ref[0] = x * lax.rsqrt(ms + RMS_EPS) * fg_ref[...]


def _moe_combine_final(x, gates, dest, yb, gate2, final_g):
    nt, tm, d = x.shape
    tiles_per_sample = nt // gate2.shape[0]
    tile = lambda w: pl.BlockSpec((1, tm, w), lambda i: (i, 0, 0))
    return pl.pallas_call(
        _combine_kernel,
        grid=(nt,),
        in_specs=[
            pl.BlockSpec((1, 1, TOP_K * tm), lambda i: (i, 0, 0), memory_space=pltpu.SMEM),
            pl.BlockSpec((1, 1, TOP_K * tm), lambda i: (jnp.minimum(i + 1, nt - 1), 0, 0),
                         memory_space=pltpu.SMEM),
            tile(d), tile(TOP_K),
            pl.BlockSpec((1, 1, d), lambda i: (i // tiles_per_sample, 0, 0)),
            pl.BlockSpec((1, d), lambda i: (0, 0)),
            pl.BlockSpec(memory_space=pl.ANY),
        ],
        out_specs=tile(d),
        out_shape=jax.ShapeDtypeStruct((nt, tm, d), F32),
        scratch_shapes=[pltpu.VMEM((2, TOP_K * tm, d), F32), pltpu.SemaphoreType.DMA((2,))],
        compiler_params=_params("arbitrary"),
        name="moe_combine_final",
    )(dest, dest, x, gates, gate2, final_g.reshape(1, d), yb)


def _moe_layer_final(x, g, shift, scale, gate2, router, wg, wu, wd, final_g):
    b, t, d = x.shape
    n_tok = b * t
    n_asg = n_tok * TOP_K
    router_pad = jnp.pad(router, ((0, 0), (0, ROUTER_PAD - N_EXPERTS)))
    h, logits = _moe_router(x, g, shift, scale, router_pad)
    logits = logits.reshape(n_tok, ROUTER_PAD)[:, :N_EXPERTS]
    top_logit, top_e = lax.top_k(logits, TOP_K)
    gates = jax.nn.softmax(top_logit, axis=-1)
    flat_e = top_e.reshape(-1)
    onehot = (flat_e[:, None] == jnp.arange(N_EXPERTS, dtype=flat_e.dtype)[None, :]).astype(jnp.int32)
    rank = jnp.cumsum(onehot, axis=0) - onehot
    counts = jnp.sum(onehot, axis=0)
    padded = (counts + MOE_ROWS - 1) // MOE_ROWS * MOE_ROWS
    pad_end = jnp.cumsum(padded)
    pad_start = pad_end - padded
    dest = (pad_start[flat_e] + jnp.sum(rank * onehot, axis=1)).astype(jnp.int32)
    n_blocks = (n_asg + MOE_ROWS - 1) // MOE_ROWS + N_EXPERTS
    block_e = jnp.minimum(jnp.searchsorted(pad_end, jnp.arange(n_blocks) * MOE_ROWS, side="right"),
                          N_EXPERTS - 1).astype(jnp.int32)
    order = jnp.argsort(flat_e, stable=True).astype(jnp.int32)
    slot = jnp.arange(n_blocks * MOE_ROWS, dtype=jnp.int32)
    slot_e = jnp.repeat(block_e, MOE_ROWS)
    slot_rank = slot - pad_start[slot_e].astype(jnp.int32)
    first = (jnp.cumsum(counts) - counts).astype(jnp.int32)
    src = jnp.clip(first[slot_e] + slot_rank, 0, n_asg - 1)
    slot_tok = jnp.where(slot_rank < counts[slot_e], order[src] // TOP_K, 0).astype(jnp.int32)
    n_used = (pad_end[-1:] // MOE_ROWS).astype(jnp.int32)
    yb = _moe_experts(h.reshape(n_tok, d), slot_tok.reshape(n_blocks, 1, MOE_ROWS), block_e, n_used, wg, wu, wd)
    nt = n_tok // ROW_TILE
    tile_dest = dest.reshape(nt, ROW_TILE, TOP_K).transpose(0, 2, 1).reshape(nt, 1, TOP_K * ROW_TILE)
    out = _moe_combine_final(x.reshape(nt, ROW_TILE, d), gates.reshape(nt, ROW_TILE, TOP_K), tile_dest, yb,
                             gate2[:b], final_g)
    return out.reshape(b, t, d)


def _final_norm_kernel(x_ref, g_ref, o_ref):
    x = x_ref[...]
    ms = jnp.mean(x * x, axis=-1, keepdims=True)
    o_ref[...] = x * lax.rsqrt(ms + RMS_EPS) * g_ref[...]


def _final_norm(x, g, tile_offset):
    b, t, d = x.shape
    n_rows = t - tile_offset * ROW_TILE
    return pl.pallas_call(
        _final_norm_kernel,
        grid=(b // SAMPLES_PER_STEP, n_rows // ROW_TILE),
        in_specs=[_row_spec(d, tile_offset), _whole((1, d))],
        out_specs=_row_spec(d),
        out_shape=jax.ShapeDtypeStruct((b, n_rows, d), F32),
        compiler_params=_params("arbitrary", "arbitrary"),
        name="final_norm",
    )(x, g.reshape(1, d))


def kernel(x, c, ctx, c_ctx, norm1_g, norm2_g, mod_w, mod_b, w_in, w_out, na_rpb, rw_mu_prev, rw_mu_next, rw_w0, rw_w2, rw_a0, rw_a2, rw_g2, rw_k_k, rw_k_a, rw_r_k, rw_gn_g, rw_gn_b, cv_dw_w, cv_dw_b, cv_ln_g, cv_ln_b, ffn_w_gate, ffn_w_up, ffn_w_down, moe_router, moe_w_gate, moe_w_up, moe_w_down, final_g):
    b, n, _ = x.shape
    n_ctx = ctx.shape[1]
    depth = mod_w.shape[0]
    assert n_ctx == ROW_TILE and n % ROW_TILE == 0 and b == SUBLANES
    ctx_tiles = n_ctx // ROW_TILE
    c_rows = 2 * SUBLANES
    n_mod = b + SAMPLES_PER_STEP
    c_all = jnp.concatenate([c, jnp.broadcast_to(c_ctx[None, :], (SAMPLES_PER_STEP, D_MODEL)),
                             jnp.zeros((c_rows - n_mod, D_MODEL), F32)], axis=0)
    mod = _modulation(c_all, mod_w, mod_b)

    xa = jnp.concatenate([ctx, x], axis=1)
    out = None
    for layer in range(depth):
        last = layer == depth - 1
        m = mod[layer, :n_mod].reshape(n_mod, 6, 1, D_MODEL)
        sh1, sc1, g1, sh2, sc2, g2 = (m[:, k] for k in range(6))
        w_in_b = w_in[layer].astype(BF16)
        w_out_b = w_out[layer].astype(BF16)
        qkv, rw_in, cv_in = _in_projection(xa, norm1_g[layer], sh1, sc1, w_in_b, ctx_tiles)
        na_l = _neighbourhood_attention(qkv, _na_bias_table(na_rpb[layer]), n_ctx)
        rw_p = dict(mu_prev=rw_mu_prev[layer], mu_next=rw_mu_next[layer], w0=rw_w0[layer], a0=rw_a0[layer],
                    wa=_rw_lora_weights(rw_w2[layer], rw_a2[layer]), g2=rw_g2[layer].astype(BF16),
                    k_k=rw_k_k[layer], k_a=rw_k_a[layer], r_k=rw_r_k[layer].reshape(-1),
                    gn_g=rw_gn_g[layer], gn_b=rw_gn_b[layer])
        rw_o = _bi_rwkv7(rw_in, rw_p, n_ctx)
        cv_o = _conformer_conv(cv_in, cv_dw_w[layer], cv_dw_b[layer], cv_ln_g[layer], cv_ln_b[layer], n_ctx)
        na_c = None if last else _context_attention(qkv, n_ctx)
        xa = _out_projection(xa, na_l, na_c, rw_o, cv_o, w_out_b, g1, n_ctx, with_ctx=not last)
        j = layer // 2
        if layer % 2 == 0:
            ffn_w = (ffn_w_gate[j].astype(BF16), ffn_w_up[j].astype(BF16), ffn_w_down[j].astype(BF16))
            xa = _dense_ffn(xa, norm2_g[layer], sh2, sc2, g2, *ffn_w, 0 if last else ctx_tiles)
            if last:
                out = _final_norm(xa, final_g, 0)
        else:
            if not last:
                raise NotImplementedError("context tokens through a mixture-of-experts layer")
            moe_w = (moe_w_gate[j].astype(BF16), moe_w_up[j].astype(BF16), moe_w_down[j].astype(BF16))
            out = _moe_layer_final(xa, norm2_g[layer], sh2, sc2, g2, moe_router[j], *moe_w, final_g)
    return out
```

```python
import functools

import jax
import jax.numpy as jnp
import numpy as np
from jax import lax
from jax.experimental import pallas as pl
from jax.experimental.pallas import tpu as pltpu

F32 = jnp.float32
BF16 = jnp.bfloat16

D_MODEL = 1024
GRID_W = 64
HEAD_DIM = 64
NA_WIDTH = 512
NA_HEADS = 8
NA_WIN_ROWS = 8
NA_WIN_COLS = 16
RW_WIDTH = 256
RW_HEADS = 4
RW_DECAY_RANK = 64
RW_LORA = 256
RW_GATE_RANK = 128
RW_IN = 1280
RW_GN_EPS = 64e-5
CV_WIDTH = 256
CV_CONV_LEN = 31
QKV_WIDTH = 3 * NA_WIDTH
IN_WIDTH = QKV_WIDTH + RW_IN + 2 * CV_WIDTH
N_EXPERTS = 8
TOP_K = 2
RMS_EPS = 1e-6
LN_EPS = 1e-5

LANES = 128
SUBLANES = 8
VMEM_LIMIT_BYTES = 56 * 1024 * 1024

ROW_TILE = 256
SAMPLES_PER_STEP = 4
MASK_VALUE = -1e30
NA_ROWS_PER_STEP = 8
MOE_ROWS = 1024
MOE_FF_TILE = 512
ROUTER_PAD = LANES
SCAN_STEPS = 64
RW_TILE = 256


def _params(*sem):
    return pltpu.CompilerParams(dimension_semantics=sem, vmem_limit_bytes=VMEM_LIMIT_BYTES)


def _col_chunks(width, step=512):
    out, c = [], 0
    while c < width:
        w = min(step, width - c)
        out.append((c, w))
        c += w
    return out


def _sigmoid(x):
    return 1.0 / (1.0 + jnp.exp(-x))


def _norm_mod(x, g, shift, scale):
    ms = jnp.mean(x * x, axis=-1, keepdims=True)
    h = x * lax.rsqrt(ms + RMS_EPS) * g
    return h * (1.0 + scale) + shift


def _dot_exact_rhs(x, m):
    hi = x.astype(BF16)
    r1 = x - hi.astype(F32)
    mid = r1.astype(BF16)
    lo = (r1 - mid.astype(F32)).astype(BF16)
    dot = functools.partial(jnp.dot, preferred_element_type=F32)
    return dot(hi, m) + dot(mid, m) + dot(lo, m)


def _mod_spec(n_batch, ctx_tiles):
    g = SAMPLES_PER_STEP
    return pl.BlockSpec((g, 1, D_MODEL), lambda b, i: (jnp.where(i < ctx_tiles, n_batch // g, b), 0, 0))


def _row_spec(width, tile_offset=0):
    return pl.BlockSpec((SAMPLES_PER_STEP, ROW_TILE, width), lambda b, i: (b, i + tile_offset, 0))


def _stack(x):
    return x.reshape(x.shape[0] * x.shape[1], x.shape[2])


def _unstack(x):
    return x.reshape(SAMPLES_PER_STEP, x.shape[0] // SAMPLES_PER_STEP, x.shape[1])


def _whole(shape):
    return pl.BlockSpec(shape, lambda *_: (0,) * len(shape), pipeline_mode=pl.Buffered(1))


def _mod_kernel(c_ref, w_ref, b_ref, o_ref):
    c = c_ref[...]
    cs = c * _sigmoid(c)
    o_ref[0] = jnp.dot(cs, w_ref[0], precision=lax.Precision.HIGHEST,
                       preferred_element_type=F32) + b_ref[0]


def _modulation(c_all, mod_w, mod_b):
    depth, d, n = mod_w.shape
    rows = c_all.shape[0]
    tn = 768
    return pl.pallas_call(
        _mod_kernel,
        grid=(depth, n // tn),
        in_specs=[
            pl.BlockSpec((rows, d), lambda l, j: (0, 0)),
            pl.BlockSpec((1, d, tn), lambda l, j: (l, 0, j)),
            pl.BlockSpec((1, 1, tn), lambda l, j: (l, 0, j)),
        ],
        out_specs=pl.BlockSpec((1, rows, tn), lambda l, j: (l, 0, j)),
        out_shape=jax.ShapeDtypeStruct((depth, rows, n), F32),
        compiler_params=_params("arbitrary", "arbitrary"),
        name="modulation",
    )(c_all, mod_w, mod_b.reshape(depth, 1, n))


def _proj_kernel(x_ref, g_ref, sh_ref, sc_ref, w_ref, qkv_ref, rw_ref, cv_ref):
    hb = _stack(_norm_mod(x_ref[...], g_ref[...], sh_ref[...], sc_ref[...])).astype(BF16)
    for ref, base in ((qkv_ref, 0), (rw_ref, QKV_WIDTH), (cv_ref, QKV_WIDTH + RW_IN)):
        for c0, w in _col_chunks(ref.shape[-1]):
            y = jnp.dot(hb, w_ref[:, base + c0:base + c0 + w], preferred_element_type=F32)
            ref[:, :, c0:c0 + w] = _unstack(y.astype(ref.dtype))


def _in_projection(x, g, shift, scale, w_in_bf16, ctx_tiles):
    b, t, d = x.shape
    ms = _mod_spec(b, ctx_tiles)
    return pl.pallas_call(
        _proj_kernel,
        grid=(b // SAMPLES_PER_STEP, t // ROW_TILE),
        in_specs=[_row_spec(d), _whole((1, d)), ms, ms, _whole((d, IN_WIDTH))],
        out_specs=[_row_spec(QKV_WIDTH), _row_spec(RW_IN), _row_spec(2 * CV_WIDTH)],
        out_shape=[
            jax.ShapeDtypeStruct((b, t, QKV_WIDTH), BF16),
            jax.ShapeDtypeStruct((b, t, RW_IN), F32),
            jax.ShapeDtypeStruct((b, t, 2 * CV_WIDTH), F32),
        ],
        compiler_params=_params("arbitrary", "arbitrary"),
        name="in_projection",
    )(x, g.reshape(1, d), shift, scale, w_in_bf16)


def _head_pair_queries(q):
    lane = lax.broadcasted_iota(jnp.int32, q.shape, 1)
    qs = q * jnp.asarray(HEAD_DIM ** -0.5, q.dtype)
    zero = jnp.zeros_like(qs)
    return jnp.concatenate([jnp.where(lane < HEAD_DIM, qs, zero), jnp.where(lane >= HEAD_DIM, qs, zero)], axis=0)


def _head_pair_merge(o, rows):
    lane = lax.broadcasted_iota(jnp.int32, (rows, LANES), 1)
    return jnp.where(lane < HEAD_DIM, o[:rows], o[rows:])


_NT = (((1,), (1,)), ((), ()))


def _na_kernel(q_ref, k_ref, v_ref, bias_ref, o_ref, *, rows, n_ctx):
    n_win = NA_WIN_ROWS * GRID_W
    nq = 2 * GRID_W
    first = pl.program_id(2) * NA_ROWS_PER_STEP
    q0 = pl.multiple_of(n_ctx + first * GRID_W, GRID_W)
    qb_all = jnp.concatenate(
        [_head_pair_queries(q_ref[0, pl.ds(q0 + j * GRID_W, GRID_W), :]) for j in range(NA_ROWS_PER_STEP)], axis=0)
    s_ctx_all = lax.dot_general(qb_all, k_ref[0, 0:n_ctx, :], _NT, preferred_element_type=F32)
    o_win, p_ctx, denom = [], [], []
    for j in range(NA_ROWS_PER_STEP):
        i = first + j
        rs = jnp.clip(i - NA_WIN_ROWS // 2, 0, rows - NA_WIN_ROWS)
        start = pl.multiple_of(n_ctx + rs * GRID_W, GRID_W)
        bias = bias_ref[0, rs - i + NA_WIN_ROWS - 1]
        s_win = lax.dot_general(qb_all[j * nq:(j + 1) * nq], k_ref[0, pl.ds(start, n_win), :], _NT,
                                preferred_element_type=F32) + bias
        s_ctx = s_ctx_all[j * nq:(j + 1) * nq]
        m = jnp.maximum(jnp.max(s_win, axis=-1, keepdims=True), jnp.max(s_ctx, axis=-1, keepdims=True))
        p_win = jnp.exp(s_win - m)
        p_ctx.append(jnp.exp(s_ctx - m))
        denom.append(jnp.sum(p_win, axis=-1, keepdims=True) + jnp.sum(p_ctx[j], axis=-1, keepdims=True))
        o_win.append(jnp.dot(p_win.astype(BF16), v_ref[0, pl.ds(start, n_win), :], preferred_element_type=F32))
    o_ctx_all = jnp.dot(jnp.concatenate(p_ctx, axis=0).astype(BF16), v_ref[0, 0:n_ctx, :],
                        preferred_element_type=F32)
    for j in range(NA_ROWS_PER_STEP):
        o = (o_win[j] + o_ctx_all[j * nq:(j + 1) * nq]) / denom[j]
        o_ref[0, j * GRID_W:(j + 1) * GRID_W, :] = _head_pair_merge(o, GRID_W).astype(o_ref.dtype)


def _na_bias_table(rpb):
    qc = np.arange(GRID_W)[:, None]
    kc = np.arange(GRID_W)[None, :]
    ws = np.clip(qc - NA_WIN_COLS // 2, 0, GRID_W - NA_WIN_COLS)
    mask = (kc >= ws) & (kc < ws + NA_WIN_COLS)
    rel = np.clip(kc - qc + NA_WIN_COLS - 1, 0, 2 * NA_WIN_COLS - 2)
    full = jnp.where(mask[None, None], rpb[:, :, rel].astype(F32), MASK_VALUE)
    dr = np.arange(NA_WIN_ROWS)[:, None] + np.arange(NA_WIN_ROWS)[None, :]
    t = full[:, dr]
    t = t.transpose(0, 1, 3, 2, 4).reshape(NA_HEADS // 2, 2, NA_WIN_ROWS, GRID_W, NA_WIN_ROWS * GRID_W)
    return t.transpose(0, 2, 1, 3, 4).reshape(NA_HEADS // 2, NA_WIN_ROWS, 2 * GRID_W, NA_WIN_ROWS * GRID_W)


def _neighbourhood_attention(qkv, bias_table, n_ctx):
    b, t, _ = qkv.shape
    n = t - n_ctx
    rows = n // GRID_W
    hp = NA_HEADS // 2
    koff, voff = NA_WIDTH // LANES, 2 * NA_WIDTH // LANES
    qrows = NA_ROWS_PER_STEP * GRID_W
    assert rows % NA_ROWS_PER_STEP == 0
    return pl.pallas_call(
        functools.partial(_na_kernel, rows=rows, n_ctx=n_ctx),
        grid=(b, hp, rows // NA_ROWS_PER_STEP),
        in_specs=[
            pl.BlockSpec((1, t, LANES), lambda bi, h, i: (bi, 0, h)),
            pl.BlockSpec((1, t, LANES), lambda bi, h, i: (bi, 0, koff + h)),
            pl.BlockSpec((1, t, LANES), lambda bi, h, i: (bi, 0, voff + h)),
            pl.BlockSpec((1, NA_WIN_ROWS, 2 * GRID_W, NA_WIN_ROWS * GRID_W), lambda bi, h, i: (h, 0, 0, 0)),
        ],
        out_specs=pl.BlockSpec((1, qrows, LANES), lambda bi, h, i: (bi, i, h)),
        out_shape=jax.ShapeDtypeStruct((b, n, NA_WIDTH), BF16),
        compiler_params=_params("arbitrary", "arbitrary", "arbitrary"),
        name="neighbourhood_attention",
    )(qkv, qkv, qkv, bias_table)


def _ctx_attn_kernel(q_ref, k_ref, v_ref, o_ref):
    l = q_ref.shape[1]
    qb = _head_pair_queries(q_ref[0])
    s = lax.dot_general(qb, k_ref[0], _NT, preferred_element_type=F32)
    p = jnp.exp(s - jnp.max(s, axis=-1, keepdims=True))
    o = jnp.dot(p.astype(BF16), v_ref[0], preferred_element_type=F32) / jnp.sum(p, axis=-1, keepdims=True)
    o_ref[0] = _head_pair_merge(o, l).astype(o_ref.dtype)


def _context_attention(qkv, n_ctx):
    b = qkv.shape[0]
    hp = NA_HEADS // 2
    koff, voff = NA_WIDTH // LANES, 2 * NA_WIDTH // LANES
    return pl.pallas_call(
        _ctx_attn_kernel,
        grid=(b, hp),
        in_specs=[
            pl.BlockSpec((1, n_ctx, LANES), lambda bi, h: (bi, 0, h)),
            pl.BlockSpec((1, n_ctx, LANES), lambda bi, h: (bi, 0, koff + h)),
            pl.BlockSpec((1, n_ctx, LANES), lambda bi, h: (bi, 0, voff + h)),
        ],
        out_specs=pl.BlockSpec((1, n_ctx, LANES), lambda bi, h: (bi, 0, h)),
        out_shape=jax.ShapeDtypeStruct((b, n_ctx, NA_WIDTH), BF16),
        compiler_params=_params("arbitrary", "arbitrary"),
        name="context_attention",
    )(qkv, qkv, qkv)


SCAN_GROUPS = ("r", "v", "z", "w0", "k0", "b0", "w1", "k1", "b1")
POST_COLS = 4 * RW_WIDTH


def _softplus(x):
    return jnp.maximum(x, 0.0) + jnp.log(1.0 + jnp.exp(-jnp.abs(x)))


def _rw_prep_kernel(u_ref, up_ref, un_ref, mup_ref, mun_ref, ones_ref, kk_ref, ka_ref, rk_ref,
                    w0_ref, a0_ref, wa_ref, g2_ref, scan_ref, post_ref, *, n_ctx):
    i = pl.program_id(0)
    b = pl.program_id(1)
    tt = u_ref.shape[1]
    n_tok = pl.num_programs(0) * tt
    u = u_ref[0]
    row = lax.broadcasted_iota(jnp.int32, u.shape, 0)
    starts_segment = jnp.logical_or(i == 0, i * tt == n_ctx)
    ends_segment = jnp.logical_or((i + 1) * tt == n_tok, (i + 1) * tt == n_ctx)
    prev_row = jnp.where(starts_segment, 0.0, up_ref[0, SUBLANES - 1:SUBLANES, :])
    next_row = jnp.where(ends_segment, 0.0, un_ref[0, 0:1, :])
    prev = jnp.where(row == 0, prev_row, pltpu.roll(u, 1, 0))
    nxt = jnp.where(row == tt - 1, next_row, pltpu.roll(u, tt - 1, 0))
    us = u + mup_ref[...] * (prev - u) + mun_ref[...] * (nxt - u)

    def put(group, val):
        g = SCAN_GROUPS.index(group)
        for c in range(2):
            scan_ref[2 * g + c, pl.ds(b, tt, stride=SUBLANES), :] = val[:, c * LANES:(c + 1) * LANES]

    ones = ones_ref[...]
    r = us[:, 0:RW_WIDTH]
    k = us[:, RW_WIDTH:2 * RW_WIDTH]
    v = us[:, 2 * RW_WIDTH:3 * RW_WIDTH]
    kk = k * kk_ref[...]
    kk = kk * lax.rsqrt(_dot_exact_rhs(kk * kk, ones) + 1e-12)
    put("r", r)
    put("v", v)
    put("z", -kk)
    for d in range(2):
        lo = 3 * RW_WIDTH + d * RW_LORA
        wa_in = us[:, lo:lo + LANES]
        lane = lax.broadcasted_iota(jnp.int32, wa_in.shape, 1)
        wa_in = jnp.where(lane < RW_DECAY_RANK, jnp.tanh(wa_in), wa_in)
        wa = jnp.dot(wa_in.astype(BF16), wa_ref[d], preferred_element_type=F32)
        log_w = -_softplus(-(w0_ref[d] + wa[:, 0:RW_WIDTH])) - 0.5
        decay = jnp.exp(-jnp.exp(log_w))
        a = _sigmoid(a0_ref[d] + wa[:, RW_WIDTH:2 * RW_WIDTH])
        gate_in = _sigmoid(us[:, lo + LANES:lo + 2 * LANES])
        g = jnp.dot(gate_in.astype(BF16), g2_ref[d], preferred_element_type=F32)
        kd = k * (1.0 + (a - 1.0) * ka_ref[...])
        bonus = _dot_exact_rhs(r * kd * rk_ref[...], ones) * v
        put("w%d" % d, decay)
        put("k%d" % d, kd)
        put("b%d" % d, kk * a)
        post_ref[0, :, 2 * d * RW_WIDTH:(2 * d + 1) * RW_WIDTH] = g
        post_ref[0, :, (2 * d + 1) * RW_WIDTH:(2 * d + 2) * RW_WIDTH] = bonus


def _head_ones():
    h = np.arange(RW_WIDTH) // HEAD_DIM
    return jnp.asarray(h[:, None] == h[None, :], BF16)


def _rw_prep(u, p, n_ctx):
    b, t, _ = u.shape
    tt = RW_TILE
    assert b == SUBLANES and n_ctx % tt == 0 and t % tt == 0
    nt8 = t // SUBLANES
    row = lambda vec, n: vec.reshape(1, n)
    return pl.pallas_call(
        functools.partial(_rw_prep_kernel, n_ctx=n_ctx),
        grid=(t // tt, b),
        in_specs=[
            pl.BlockSpec((1, tt, RW_IN), lambda i, bi: (bi, i, 0)),
            pl.BlockSpec((1, SUBLANES, RW_IN), lambda i, bi: (bi, jnp.maximum(i * (tt // SUBLANES) - 1, 0), 0)),
            pl.BlockSpec((1, SUBLANES, RW_IN), lambda i, bi: (bi, jnp.minimum((i + 1) * (tt // SUBLANES), nt8 - 1), 0)),
            _whole((1, RW_IN)), _whole((1, RW_IN)),
            _whole((RW_WIDTH, RW_WIDTH)),
            _whole((1, RW_WIDTH)), _whole((1, RW_WIDTH)), _whole((1, RW_WIDTH)),
            _whole((2, 1, RW_WIDTH)), _whole((2, 1, RW_WIDTH)),
            _whole((2, LANES, 2 * RW_WIDTH)), _whole((2, RW_GATE_RANK, RW_WIDTH)),
        ],
        out_specs=[
            pl.BlockSpec((2 * len(SCAN_GROUPS), tt * b, LANES), lambda i, bi: (0, i, 0)),
            pl.BlockSpec((1, tt, POST_COLS), lambda i, bi: (bi, i, 0)),
        ],
        out_shape=[
            jax.ShapeDtypeStruct((2 * len(SCAN_GROUPS), t * b, LANES), F32),
            jax.ShapeDtypeStruct((b, t, POST_COLS), F32),
        ],
        compiler_params=_params("arbitrary", "arbitrary"),
        name="rwkv_prep",
    )(u, u, u, row(p["mu_prev"], RW_IN), row(p["mu_next"], RW_IN), _head_ones(),
      row(p["k_k"], RW_WIDTH), row(p["k_a"], RW_WIDTH), row(p["r_k"], RW_WIDTH),
      p["w0"].reshape(2, 1, RW_WIDTH), p["a0"].reshape(2, 1, RW_WIDTH), p["wa"], p["g2"])


def _rw_lora_weights(w2, a2):
    z = jnp.zeros_like(w2)
    top = jnp.concatenate([w2, z], axis=-1)
    bot = jnp.concatenate([z, a2], axis=-1)
    return jnp.concatenate([top, bot], axis=1).astype(BF16)


_QUARTER = LANES // 4
_VROWS = HEAD_DIM // 4


def _chain_tile(f_ref, b_ref, sf, sb):
    pieces = [f_ref[0, pl.ds(sf, SUBLANES), :], f_ref[1, pl.ds(sf, SUBLANES), :],
              b_ref[0, pl.ds(sb, SUBLANES), :], b_ref[1, pl.ds(sb, SUBLANES), :]]
    return jnp.concatenate(pieces * 4, axis=0).T


def _scan_kernel(rf, vf, zf, wf, kf, bf, rb, vb, zb, wb, kb, bb, yf_ref, yb_ref, s_ref, t_ref, v_ref, y_ref):
    steps = rf.shape[1] // SUBLANES
    zero = jnp.zeros((_VROWS, LANES), F32)

    @pl.when(pl.program_id(0) == 0)
    def _():
        s_ref[...] = jnp.zeros_like(s_ref)
        y_ref[...] = jnp.zeros_like(y_ref)

    lane_q = lax.broadcasted_iota(jnp.int32, (_VROWS, LANES), 1) // _QUARTER
    col_q = (lax.broadcasted_iota(jnp.int32, (_QUARTER, LANES), 1) % HEAD_DIM) // _VROWS

    def offsets(s):
        s = jnp.clip(s, 0, steps - 1)
        return pl.multiple_of(s * SUBLANES, SUBLANES), pl.multiple_of((steps - 1 - s) * SUBLANES, SUBLANES)

    def prepare(s, p):
        sf, sb = offsets(s)
        zf_off, zb_off = offsets(s + 1)
        t_ref[p, 0] = _chain_tile(zf, zb, zf_off, zb_off)
        for n, (f, b) in enumerate(((wf, wb), (bf, bb), (kf, kb), (rf, rb))):
            t_ref[p, n + 1] = _chain_tile(f, b, sf, sb)
        vt = _chain_tile(vf, vb, sf, sb)
        for h in range(2):
            v = zero
            for q in range(4):
                lo = h * HEAD_DIM + q * _VROWS
                v = jnp.where(lane_q == q, vt[lo:lo + _VROWS], v)
            v_ref[p, h] = v

    def flush(s, p):
        sf, sb = offsets(s)
        yt = jnp.concatenate([y_ref[p, 0]] * 4 + [y_ref[p, 1]] * 4, axis=0).T
        nat = jnp.zeros((_QUARTER, LANES), F32)
        for q in range(4):
            nat = jnp.where(col_q == q, yt[q * _QUARTER:(q + 1) * _QUARTER], nat)
        yf_ref[0, pl.ds(sf, SUBLANES), :] = nat[0:8]
        yf_ref[1, pl.ds(sf, SUBLANES), :] = nat[8:16]
        yb_ref[0, pl.ds(sb, SUBLANES), :] = nat[16:24]
        yb_ref[1, pl.ds(sb, SUBLANES), :] = nat[24:32]

    def update(p, sz):
        sz_next = []
        for h in range(2):
            base = h * HEAD_DIM
            v = v_ref[p, h]
            y = [zero, zero]
            zn = [zero, zero]
            for k in range(HEAD_DIM):
                row = pl.ds(base + k, 1)
                st = s_ref[base + k] * t_ref[p, 1, row, :] + sz[h] * t_ref[p, 2, row, :] + v * t_ref[p, 3, row, :]
                s_ref[base + k] = st
                y[k % 2] = y[k % 2] + st * t_ref[p, 4, row, :]
                zn[k % 2] = zn[k % 2] + st * t_ref[p, 0, row, :]
            y_ref[p, h] = y[0] + y[1]
            sz_next.append(zn[0] + zn[1])
        return tuple(sz_next)

    f0, b0 = offsets(0)
    t_ref[1, 0] = _chain_tile(zf, zb, f0, b0)
    sz0 = []
    for h in range(2):
        acc = [zero, zero]
        for k in range(HEAD_DIM):
            r = h * HEAD_DIM + k
            acc[k % 2] = acc[k % 2] + s_ref[r] * t_ref[1, 0, pl.ds(r, 1), :]
        sz0.append(acc[0] + acc[1])
    prepare(0, 0)

    def pair(j, sz):
        s = 2 * j
        for p in range(2):
            prepare(s + p + 1, 1 - p)
            sz = update(p, sz)
            flush(s + p - 1, 1 - p)
        return sz

    lax.fori_loop(0, steps // 2, pair, tuple(sz0))
    flush(steps - 1, 1)


def _rwkv7_scan(ops, n_batch, n_ctx):
    rows = ops.shape[1]
    blk = SCAN_STEPS * n_batch
    nb = rows // blk
    nb_c = n_ctx * n_batch // blk

    def mirror(i):
        return jnp.where(i < nb_c, nb_c - 1 - i, nb_c + nb - 1 - i)

    def spec(group, backward):
        g = SCAN_GROUPS.index(group)
        if backward:
            return pl.BlockSpec((2, blk, LANES), lambda i: (g, mirror(i), 0))
        return pl.BlockSpec((2, blk, LANES), lambda i: (g, i, 0))

    fw = [spec(g, False) for g in ("r", "v", "z", "w0", "k0", "b0")]
    bw = [spec(g, True) for g in ("r", "v", "z", "w1", "k1", "b1")]
    out_sds = jax.ShapeDtypeStruct((2, rows, LANES), F32)
    return pl.pallas_call(
        _scan_kernel,
        grid=(nb,),
        in_specs=fw + bw,
        out_specs=[pl.BlockSpec((2, blk, LANES), lambda i: (0, i, 0)),
                   pl.BlockSpec((2, blk, LANES), lambda i: (0, mirror(i), 0))],
        out_shape=[out_sds, out_sds],
        scratch_shapes=[pltpu.VMEM((2 * HEAD_DIM, _VROWS, LANES), F32), pltpu.VMEM((2, 5, LANES, LANES), F32),
                        pltpu.VMEM((2, 2, _VROWS, LANES), F32), pltpu.VMEM((2, 2, _VROWS, LANES), F32)],
        compiler_params=_params("arbitrary"),
        name="rwkv_scan",
    )(*([ops] * 12))


def _rw_post_kernel(yf_ref, yb_ref, post_ref, ones_ref, gg_ref, gb_ref, o_ref):
    b = pl.program_id(1)
    tt = o_ref.shape[1]
    ones = ones_ref[...]
    out = None
    for d, y_ref in enumerate((yf_ref, yb_ref)):
        y = jnp.concatenate([y_ref[0, pl.ds(b, tt, stride=SUBLANES), :],
                             y_ref[1, pl.ds(b, tt, stride=SUBLANES), :]], axis=1)
        mu = _dot_exact_rhs(y, ones) * (1.0 / HEAD_DIM)
        yc = y - mu
        var = _dot_exact_rhs(yc * yc, ones) * (1.0 / HEAD_DIM)
        yn = yc * lax.rsqrt(var + RW_GN_EPS) * gg_ref[...] + gb_ref[...]
        g = post_ref[0, :, 2 * d * RW_WIDTH:(2 * d + 1) * RW_WIDTH]
        bonus = post_ref[0, :, (2 * d + 1) * RW_WIDTH:(2 * d + 2) * RW_WIDTH]
        term = (yn + bonus) * g
        out = term if out is None else out + term
    o_ref[0] = out.astype(o_ref.dtype)


def _rw_post(y_f, y_b, post, gn_g, gn_b):
    b, t, _ = post.shape
    tt = RW_TILE
    yspec = pl.BlockSpec((2, tt * b, LANES), lambda i, bi: (0, i, 0))
    return pl.pallas_call(
        _rw_post_kernel,
        grid=(t // tt, b),
        in_specs=[yspec, yspec, pl.BlockSpec((1, tt, POST_COLS), lambda i, bi: (bi, i, 0)),
                  _whole((RW_WIDTH, RW_WIDTH)), _whole((1, RW_WIDTH)), _whole((1, RW_WIDTH))],
        out_specs=pl.BlockSpec((1, tt, RW_WIDTH), lambda i, bi: (bi, i, 0)),
        out_shape=jax.ShapeDtypeStruct((b, t, RW_WIDTH), BF16),
        compiler_params=_params("arbitrary", "arbitrary"),
        name="rwkv_post",
    )(y_f, y_b, post, _head_ones(), gn_g.reshape(1, RW_WIDTH), gn_b.reshape(1, RW_WIDTH))


def _bi_rwkv7(u, p, n_ctx):
    ops, post = _rw_prep(u, p, n_ctx)
    y_f, y_b = _rwkv7_scan(ops, u.shape[0], n_ctx)
    return _rw_post(y_f, y_b, post, p["gn_g"], p["gn_b"])


CONV_PAD = 16


def _conv_kernel(u_ref, w_ref, b_ref, lg_ref, lb_ref, o_ref, pad_ref, shift_ref, *, n_ctx):
    t = u_ref.shape[1]
    half = CV_CONV_LEN // 2
    chunk = ROW_TILE
    u = u_ref[0]
    h = u[:, :CV_WIDTH] * _sigmoid(u[:, CV_WIDTH:])
    zeros = jnp.zeros((CONV_PAD, CV_WIDTH), F32)
    lat0 = 2 * CONV_PAD + n_ctx
    pad_ref[0:CONV_PAD, :] = zeros
    pad_ref[CONV_PAD:CONV_PAD + n_ctx, :] = h[:n_ctx]
    pad_ref[CONV_PAD + n_ctx:lat0, :] = zeros
    pad_ref[lat0:lat0 + t - n_ctx, :] = h[n_ctx:]
    pad_ref[lat0 + t - n_ctx:lat0 + t - n_ctx + CONV_PAD, :] = zeros
    w = w_ref[...]

    def body(c, carry):
        out0 = pl.multiple_of(c * chunk, chunk)
        base = pl.multiple_of(out0 + jnp.where(c >= n_ctx // chunk, CONV_PAD, 0), CONV_PAD)
        win = pad_ref[pl.ds(base, chunk + 2 * CONV_PAD), :]
        acc = jnp.zeros((chunk, CV_WIDTH), F32)
        for r in range(SUBLANES):
            offs = [o for o in range(CONV_PAD - half, CONV_PAD + half + 1) if o % SUBLANES == r]
            if not offs:
                continue
            shift_ref[r] = win[r:r + shift_ref.shape[1], :]
            for off in offs:
                j = off - (CONV_PAD - half)
                acc = acc + shift_ref[r, off - r:off - r + chunk, :] * w[j:j + 1, :]
        hh = acc + b_ref[...]
        mu = jnp.mean(hh, axis=-1, keepdims=True)
        hc = hh - mu
        var = jnp.mean(hc * hc, axis=-1, keepdims=True)
        y = hc * lax.rsqrt(var + LN_EPS) * lg_ref[...] + lb_ref[...]
        o_ref[0, pl.ds(out0, chunk), :] = (y * _sigmoid(y)).astype(o_ref.dtype)
        return carry

    lax.fori_loop(0, t // chunk, body, 0)


def _conformer_conv(u, dw_w, dw_b, ln_g, ln_b, n_ctx):
    b, t, _ = u.shape
    assert n_ctx % ROW_TILE == 0 and t % ROW_TILE == 0
    row = lambda vec: vec.reshape(1, CV_WIDTH)
    return pl.pallas_call(
        functools.partial(_conv_kernel, n_ctx=n_ctx),
        grid=(b,),
        in_specs=[pl.BlockSpec((1, t, 2 * CV_WIDTH), lambda bi: (bi, 0, 0)),
                  _whole((CV_CONV_LEN, CV_WIDTH)), _whole((1, CV_WIDTH)), _whole((1, CV_WIDTH)),
                  _whole((1, CV_WIDTH))],
        out_specs=pl.BlockSpec((1, t, CV_WIDTH), lambda bi: (bi, 0, 0)),
        out_shape=jax.ShapeDtypeStruct((b, t, CV_WIDTH), BF16),
        scratch_shapes=[pltpu.VMEM((t + 3 * CONV_PAD, CV_WIDTH), F32),
                        pltpu.VMEM((SUBLANES, ROW_TILE + 2 * CONV_PAD - SUBLANES, CV_WIDTH), F32)],
        compiler_params=_params("arbitrary"),
        name="conformer_conv",
    )(u, dw_w, row(dw_b), row(ln_g), row(ln_b))


def _wout_kernel(x_ref, nal_ref, nac_ref, rw_ref, cv_ref, w_ref, gate_ref, o_ref, *, ctx_tiles):
    dot = functools.partial(jnp.dot, preferred_element_type=F32)
    na = nal_ref[...]
    if ctx_tiles:
        na = jnp.where(pl.program_id(1) < ctx_tiles, nac_ref[...], na)
    acc = (dot(_stack(na), w_ref[0:NA_WIDTH, :])
           + dot(_stack(rw_ref[...]), w_ref[NA_WIDTH:NA_WIDTH + RW_WIDTH, :])
           + dot(_stack(cv_ref[...]), w_ref[NA_WIDTH + RW_WIDTH:, :]))
    o_ref[...] = x_ref[...] + gate_ref[...] * _unstack(acc)


def _out_projection(x, na_l, na_c, rw, cv, w_out_bf16, gate, n_ctx, with_ctx):
    b, t, d = x.shape
    ctx_tiles = n_ctx // ROW_TILE
    assert ctx_tiles == 1 or not with_ctx
    off = 0 if with_ctx else ctx_tiles
    n_rows = t if with_ctx else t - n_ctx
    g = SAMPLES_PER_STEP
    if with_ctx:
        nal_spec = pl.BlockSpec((g, ROW_TILE, NA_WIDTH), lambda bi, i: (bi, jnp.maximum(i - ctx_tiles, 0), 0))
    else:
        nal_spec = _row_spec(NA_WIDTH)
        na_c = na_l
    return pl.pallas_call(
        functools.partial(_wout_kernel, ctx_tiles=ctx_tiles if with_ctx else 0),
        grid=(b // g, n_rows // ROW_TILE),
        in_specs=[_row_spec(d, off), nal_spec,
                  pl.BlockSpec((g, ROW_TILE, NA_WIDTH), lambda bi, i: (bi, 0, 0)),
                  _row_spec(RW_WIDTH, off), _row_spec(CV_WIDTH, off),
                  _whole((d, d)), _mod_spec(b, ctx_tiles if with_ctx else 0)],
        out_specs=_row_spec(d),
        out_shape=jax.ShapeDtypeStruct((b, n_rows, d), F32),
        compiler_params=_params("arbitrary", "arbitrary"),
        name="out_projection",
    )(x, na_l, na_c, rw, cv, w_out_bf16, gate)


def _ffn_kernel(x_ref, g_ref, sh_ref, sc_ref, gate_ref, wg_ref, wu_ref, wd_ref, o_ref, acc_ref):
    x = x_ref[...]
    hb = _stack(_norm_mod(x, g_ref[...], sh_ref[...], sc_ref[...])).astype(BF16)
    dot = functools.partial(jnp.dot, preferred_element_type=F32)
    for n, (c0, w) in enumerate(_col_chunks(wg_ref.shape[1])):
        g = dot(hb, wg_ref[:, c0:c0 + w])
        u = dot(hb, wu_ref[:, c0:c0 + w])
        a = (g * _sigmoid(g) * u).astype(BF16)
        y = dot(a, wd_ref[c0:c0 + w, :])
        if n == 0:
            acc_ref[...] = y
        else:
            acc_ref[...] += y
    o_ref[...] = x + gate_ref[...] * _unstack(acc_ref[...])


def _dense_ffn(x, g, shift, scale, gate, wg, wu, wd, ctx_tiles):
    b, t, d = x.shape
    dff = wg.shape[1]
    ms = _mod_spec(b, ctx_tiles)
    return pl.pallas_call(
        _ffn_kernel,
        grid=(b // SAMPLES_PER_STEP, t // ROW_TILE),
        in_specs=[_row_spec(d), _whole((1, d)), ms, ms, ms,
                  _whole((d, dff)), _whole((d, dff)), _whole((dff, d))],
        out_specs=_row_spec(d),
        out_shape=jax.ShapeDtypeStruct((b, t, d), F32),
        scratch_shapes=[pltpu.VMEM((SAMPLES_PER_STEP * ROW_TILE, d), F32)],
        compiler_params=_params("arbitrary", "arbitrary"),
        name="dense_ffn",
    )(x, g.reshape(1, d), shift, scale, gate, wg, wu, wd)


def _rows_to_tiles(ref2d, val):
    rows = val.shape[0]
    for s in range(val.shape[1] // LANES):
        ref2d[pl.ds(s, rows, stride=SUBLANES), :] = val[:, s * LANES:(s + 1) * LANES]


def _rows_from_tiles(ref, lead, first_row, rows):
    return jnp.concatenate([ref[lead, pl.ds(first_row * SUBLANES + s, rows, stride=SUBLANES), :]
                            for s in range(SUBLANES)], axis=1)


def _router_kernel(x_ref, g_ref, sh_ref, sc_ref, wr_ref, h_ref, logit_ref):
    h = _norm_mod(x_ref[...], g_ref[...], sh_ref[...], sc_ref[...])
    for s in range(h.shape[0]):
        _rows_to_tiles(h_ref.at[s], h[s])
    logit_ref[...] = _unstack(jnp.dot(_stack(h), wr_ref[...], precision=lax.Precision.HIGHEST,
                                      preferred_element_type=F32))


def _moe_router(x, g, shift, scale, router_pad):
    b, t, d = x.shape
    assert d == SUBLANES * LANES
    ms = _mod_spec(b, 0)
    return pl.pallas_call(
        _router_kernel,
        grid=(b // SAMPLES_PER_STEP, t // ROW_TILE),
        in_specs=[_row_spec(d), _whole((1, d)), ms, ms, _whole((d, ROUTER_PAD))],
        out_specs=[pl.BlockSpec((SAMPLES_PER_STEP, ROW_TILE * SUBLANES, LANES), lambda bi, i: (bi, i, 0)),
                   _row_spec(ROUTER_PAD)],
        out_shape=[jax.ShapeDtypeStruct((b, t * SUBLANES, LANES), F32),
                   jax.ShapeDtypeStruct((b, t, ROUTER_PAD), F32)],
        compiler_params=_params("arbitrary", "arbitrary"),
        name="moe_router",
    )(x, g.reshape(1, d), shift, scale, router_pad)


def _tile_copy(src_hbm, dst_vmem, sem, src_row, dst_row):
    dst = dst_vmem.at[pl.ds(pl.multiple_of(dst_row * SUBLANES, SUBLANES), SUBLANES)]
    return pltpu.make_async_copy(src_hbm.at[src_row], dst, sem)


GATHER_UNROLL = 8


def _start_row_gather(src_hbm, dst_vmem, sem, index_of_row, rows):
    def body(r, c):
        _tile_copy(src_hbm, dst_vmem, sem, index_of_row(r), r).start()
        return c

    lax.fori_loop(0, rows, body, 0, unroll=GATHER_UNROLL)


def _wait_row_gather(src_hbm, dst_vmem, sem, rows):
    def body(r, c):
        _tile_copy(src_hbm, dst_vmem, sem, 0, r).wait()
        return c

    lax.fori_loop(0, rows, body, 0, unroll=GATHER_UNROLL)


def _expert_kernel(be_ref, nused_ref, tok_ref, tok_next_ref, h_hbm, wg_ref, wu_ref, wd_ref, o_ref,
                   xg_ref, acc_ref, sem):
    i = pl.program_id(0)
    rows = o_ref.shape[0] // SUBLANES
    n_used = nused_ref[0]
    used = i < n_used
    slot = i % 2

    @pl.when(i == 0)
    def _():
        _start_row_gather(h_hbm, xg_ref.at[0], sem.at[0], lambda r: tok_ref[0, 0, r], rows)

    @pl.when(i + 1 < n_used)
    def _():
        _start_row_gather(h_hbm, xg_ref.at[1 - slot], sem.at[1 - slot], lambda r: tok_next_ref[0, 0, r], rows)

    @pl.when(used)
    def _():
        _wait_row_gather(h_hbm, xg_ref.at[slot], sem.at[slot], rows)
        dot = functools.partial(jnp.dot, preferred_element_type=F32)
        xb = _rows_from_tiles(xg_ref, slot, 0, rows).astype(BF16)
        for n, (c0, w) in enumerate(_col_chunks(wg_ref.shape[2], MOE_FF_TILE)):
            g = dot(xb, wg_ref[0, :, c0:c0 + w])
            u = dot(xb, wu_ref[0, :, c0:c0 + w])
            a = (g * _sigmoid(g) * u).astype(BF16)
            y = dot(a, wd_ref[0, c0:c0 + w, :])
            if n == 0:
                acc_ref[...] = y
            else:
                acc_ref[...] += y
        _rows_to_tiles(o_ref, acc_ref[...])

    @pl.when(jnp.logical_not(used))
    def _():
        o_ref[...] = jnp.zeros_like(o_ref)


def _moe_experts(h_flat, slot_tok, block_e, n_used, wg, wu, wd):
    nb = slot_tok.shape[0]
    d = wg.shape[1]
    dff = wg.shape[2]
    once = pl.Buffered(1)
    grid_spec = pltpu.PrefetchScalarGridSpec(
        num_scalar_prefetch=2,
        grid=(nb,),
        in_specs=[
            pl.BlockSpec((1, 1, MOE_ROWS), lambda i, be, nu: (i, 0, 0), memory_space=pltpu.SMEM),
            pl.BlockSpec((1, 1, MOE_ROWS), lambda i, be, nu: (jnp.minimum(i + 1, nb - 1), 0, 0),
                         memory_space=pltpu.SMEM),
            pl.BlockSpec(memory_space=pl.ANY),
            pl.BlockSpec((1, d, dff), lambda i, be, nu: (be[i], 0, 0), pipeline_mode=once),
            pl.BlockSpec((1, d, dff), lambda i, be, nu: (be[i], 0, 0), pipeline_mode=once),
            pl.BlockSpec((1, dff, d), lambda i, be, nu: (be[i], 0, 0), pipeline_mode=once),
        ],
        out_specs=pl.BlockSpec((MOE_ROWS * SUBLANES, LANES), lambda i, be, nu: (i, 0)),
        scratch_shapes=[pltpu.VMEM((2, MOE_ROWS * SUBLANES, LANES), F32), pltpu.VMEM((MOE_ROWS, d), F32),
                        pltpu.SemaphoreType.DMA((2,))],
    )
    yb = pl.pallas_call(
        _expert_kernel,
        grid_spec=grid_spec,
        out_shape=jax.ShapeDtypeStruct((nb * MOE_ROWS * SUBLANES, LANES), F32),
        compiler_params=_params("arbitrary"),
        name="moe_experts",
    )(block_e, n_used, slot_tok, slot_tok, h_flat, wg, wu, wd)
    return yb.reshape(nb * MOE_ROWS, SUBLANES, LANES)


def _combine_kernel(dest_ref, dest_next_ref, x_ref, gates_ref, gate2_ref, fg_ref, yb_hbm, o_ref, y_ref, sem):
    i = pl.program_id(0)
    rows = x_ref.shape[1]
    slot = i % 2

    @pl.when(i == 0)
    def _():
        _start_row_gather(yb_hbm, y_ref.at[0], sem.at[0], lambda r: dest_ref[0, 0, r], TOP_K * rows)

    @pl.when(i + 1 < pl.num_programs(0))
    def _():
        _start_row_gather(yb_hbm, y_ref.at[1 - slot], sem.at[1 - slot], lambda r: dest_next_ref[0, 0, r],
                          TOP_K * rows)

    _wait_row_gather(yb_hbm, y_ref.at[slot], sem.at[slot], TOP_K * rows)
    gates = gates_ref[0]
    y = (_rows_from_tiles(y_ref, slot, 0, rows) * gates[:, 0:1]
         + _rows_from_tiles(y_ref, slot, rows, rows) * gates[:, 1:2])
    x = x_ref[0] + gate2_ref[0] * y
    ms = jnp.mean(x * x, axis=-1, keepdims=True)
    o_ref[0] = x * lax.rsqrt(ms + RMS_EPS) * fg_ref[...]


def _moe_combine_final(x, gates, dest, yb, gate2, final_g):
    nt, tm, d = x.shape
    tiles_per_sample = nt // gate2.shape[0]
    tile = lambda w: pl.BlockSpec((1, tm, w), lambda i: (i, 0, 0))
    return pl.pallas_call(
        _combine_kernel,
        grid=(nt,),
        in_specs=[
            pl.BlockSpec((1, 1, TOP_K * tm), lambda i: (i, 0, 0), memory_space=pltpu.SMEM),
            pl.BlockSpec((1, 1, TOP_K * tm), lambda i: (jnp.minimum(i + 1, nt - 1), 0, 0),
                         memory_space=pltpu.SMEM),
            tile(d), tile(TOP_K),
            pl.BlockSpec((1, 1, d), lambda i: (i // tiles_per_sample, 0, 0)),
            pl.BlockSpec((1, d), lambda i: (0, 0)),
            pl.BlockSpec(memory_space=pl.ANY),
        ],
        out_specs=tile(d),
        out_shape=jax.ShapeDtypeStruct((nt, tm, d), F32),
        scratch_shapes=[pltpu.VMEM((2, TOP_K * tm * SUBLANES, LANES), F32), pltpu.SemaphoreType.DMA((2,))],
        compiler_params=_params("arbitrary"),
        name="moe_combine_final",
    )(dest, dest, x, gates, gate2, final_g.reshape(1, d), yb)


def _moe_layer_final(x, g, shift, scale, gate2, router, wg, wu, wd, final_g):
    b, t, d = x.shape
    n_tok = b * t
    n_asg = n_tok * TOP_K
    router_pad = jnp.pad(router, ((0, 0), (0, ROUTER_PAD - N_EXPERTS)))
    h, logits = _moe_router(x, g, shift, scale, router_pad)
    logits = logits.reshape(n_tok, ROUTER_PAD)[:, :N_EXPERTS]
    top_logit, top_e = lax.top_k(logits, TOP_K)
    gates = jax.nn.softmax(top_logit, axis=-1)
    flat_e = top_e.reshape(-1)
    onehot = (flat_e[:, None] == jnp.arange(N_EXPERTS, dtype=flat_e.dtype)[None, :]).astype(jnp.int32)
    rank = jnp.cumsum(onehot, axis=0) - onehot
    counts = jnp.sum(onehot, axis=0)
    padded = (counts + MOE_ROWS - 1) // MOE_ROWS * MOE_ROWS
    pad_end = jnp.cumsum(padded)
    pad_start = pad_end - padded
    dest = (pad_start[flat_e] + jnp.sum(rank * onehot, axis=1)).astype(jnp.int32)
    n_blocks = (n_asg + MOE_ROWS - 1) // MOE_ROWS + N_EXPERTS
    block_e = jnp.minimum(jnp.searchsorted(pad_end, jnp.arange(n_blocks) * MOE_ROWS, side="right"),
                          N_EXPERTS - 1).astype(jnp.int32)
    order = jnp.argsort(flat_e, stable=True).astype(jnp.int32)
    slot = jnp.arange(n_blocks * MOE_ROWS, dtype=jnp.int32)
    slot_e = jnp.repeat(block_e, MOE_ROWS)
    slot_rank = slot - pad_start[slot_e].astype(jnp.int32)
    first = (jnp.cumsum(counts) - counts).astype(jnp.int32)
    src = jnp.clip(first[slot_e] + slot_rank, 0, n_asg - 1)
    slot_tok = jnp.where(slot_rank < counts[slot_e], order[src] // TOP_K, 0).astype(jnp.int32)
    n_used = (pad_end[-1:] // MOE_ROWS).astype(jnp.int32)
    yb = _moe_experts(h.reshape(n_tok, SUBLANES, LANES), slot_tok.reshape(n_blocks, 1, MOE_ROWS), block_e, n_used,
                      wg, wu, wd)
    nt = n_tok // ROW_TILE
    tile_dest = dest.reshape(nt, ROW_TILE, TOP_K).transpose(0, 2, 1).reshape(nt, 1, TOP_K * ROW_TILE)
    out = _moe_combine_final(x.reshape(nt, ROW_TILE, d), gates.reshape(nt, ROW_TILE, TOP_K), tile_dest, yb,
                             gate2[:b], final_g)
    return out.reshape(b, t, d)


def _final_norm_kernel(x_ref, g_ref, o_ref):
    x = x_ref[...]
    ms = jnp.mean(x * x, axis=-1, keepdims=True)
    o_ref[...] = x * lax.rsqrt(ms + RMS_EPS) * g_ref[...]


def _final_norm(x, g, tile_offset):
    b, t, d = x.shape
    n_rows = t - tile_offset * ROW_TILE
    return pl.pallas_call(
        _final_norm_kernel,
        grid=(b // SAMPLES_PER_STEP, n_rows // ROW_TILE),
        in_specs=[_row_spec(d, tile_offset), _whole((1, d))],
        out_specs=_row_spec(d),
        out_shape=jax.ShapeDtypeStruct((b, n_rows, d), F32),
        compiler_params=_params("arbitrary", "arbitrary"),
        name="final_norm",
    )(x, g.reshape(1, d))


def kernel(x, c, ctx, c_ctx, norm1_g, norm2_g, mod_w, mod_b, w_in, w_out, na_rpb, rw_mu_prev, rw_mu_next, rw_w0, rw_w2, rw_a0, rw_a2, rw_g2, rw_k_k, rw_k_a, rw_r_k, rw_gn_g, rw_gn_b, cv_dw_w, cv_dw_b, cv_ln_g, cv_ln_b, ffn_w_gate, ffn_w_up, ffn_w_down, moe_router, moe_w_gate, moe_w_up, moe_w_down, final_g):
    b, n, _ = x.shape
    n_ctx = ctx.shape[1]
    depth = mod_w.shape[0]
    assert n_ctx == ROW_TILE and n % ROW_TILE == 0 and b == SUBLANES
    ctx_tiles = n_ctx // ROW_TILE
    c_rows = 2 * SUBLANES
    n_mod = b + SAMPLES_PER_STEP
    c_all = jnp.concatenate([c, jnp.broadcast_to(c_ctx[None, :], (SAMPLES_PER_STEP, D_MODEL)),
                             jnp.zeros((c_rows - n_mod, D_MODEL), F32)], axis=0)
    mod = _modulation(c_all, mod_w, mod_b)

    xa = jnp.concatenate([ctx, x], axis=1)
    out = None
    for layer in range(depth):
        last = layer == depth - 1
        m = mod[layer, :n_mod].reshape(n_mod, 6, 1, D_MODEL)
        sh1, sc1, g1, sh2, sc2, g2 = (m[:, k] for k in range(6))
        w_in_b = w_in[layer].astype(BF16)
        w_out_b = w_out[layer].astype(BF16)
        qkv, rw_in, cv_in = _in_projection(xa, norm1_g[layer], sh1, sc1, w_in_b, ctx_tiles)
        na_l = _neighbourhood_attention(qkv, _na_bias_table(na_rpb[layer]), n_ctx)
        rw_p = dict(mu_prev=rw_mu_prev[layer], mu_next=rw_mu_next[layer], w0=rw_w0[layer], a0=rw_a0[layer],
                    wa=_rw_lora_weights(rw_w2[layer], rw_a2[layer]), g2=rw_g2[layer].astype(BF16),
                    k_k=rw_k_k[layer], k_a=rw_k_a[layer], r_k=rw_r_k[layer].reshape(-1),
                    gn_g=rw_gn_g[layer], gn_b=rw_gn_b[layer])
        rw_o = _bi_rwkv7(rw_in, rw_p, n_ctx)
        cv_o = _conformer_conv(cv_in, cv_dw_w[layer], cv_dw_b[layer], cv_ln_g[layer], cv_ln_b[layer], n_ctx)
        na_c = None if last else _context_attention(qkv, n_ctx)
        xa = _out_projection(xa, na_l, na_c, rw_o, cv_o, w_out_b, g1, n_ctx, with_ctx=not last)
        j = layer // 2
        if layer % 2 == 0:
            ffn_w = (ffn_w_gate[j].astype(BF16), ffn_w_up[j].astype(BF16), ffn_w_down[j].astype(BF16))
            xa = _dense_ffn(xa, norm2_g[layer], sh2, sc2, g2, *ffn_w, 0 if last else ctx_tiles)
            if last:
                out = _final_norm(xa, final_g, 0)
        else:
            if not last:
                raise NotImplementedError("context tokens through a mixture-of-experts layer")
            moe_w = (moe_w_gate[j].astype(BF16), moe_w_up[j].astype(BF16), moe_w_down[j].astype(BF16))
            out = _moe_layer_final(xa, norm2_g[layer], sh2, sc2, g2, moe_router[j], *moe_w, final_g)
    return out
```

```python
import functools

import jax
import jax.numpy as jnp
import numpy as np
from jax import lax
from jax.experimental import pallas as pl
from jax.experimental.pallas import tpu as pltpu

F32 = jnp.float32
BF16 = jnp.bfloat16

D_MODEL = 1024
GRID_W = 64
HEAD_DIM = 64
NA_WIDTH = 512
NA_HEADS = 8
NA_WIN_ROWS = 8
NA_WIN_COLS = 16
RW_WIDTH = 256
RW_HEADS = 4
RW_DECAY_RANK = 64
RW_LORA = 256
RW_GATE_RANK = 128
RW_IN = 1280
RW_GN_EPS = 64e-5
CV_WIDTH = 256
CV_CONV_LEN = 31
QKV_WIDTH = 3 * NA_WIDTH
IN_WIDTH = QKV_WIDTH + RW_IN + 2 * CV_WIDTH
N_EXPERTS = 8
TOP_K = 2
RMS_EPS = 1e-6
LN_EPS = 1e-5

LANES = 128
SUBLANES = 8
VMEM_LIMIT_BYTES = 56 * 1024 * 1024

ROW_TILE = 256
SAMPLES_PER_STEP = 4
MASK_VALUE = -1e30
NA_ROWS_PER_STEP = 8
MOE_ROWS = 1024
MOE_FF_TILE = 512
ROUTER_PAD = LANES
SCAN_STEPS = 64
RW_TILE = 256


def _params(*sem):
    return pltpu.CompilerParams(dimension_semantics=sem, vmem_limit_bytes=VMEM_LIMIT_BYTES)


def _col_chunks(width, step=512):
    out, c = [], 0
    while c < width:
        w = min(step, width - c)
        out.append((c, w))
        c += w
    return out


def _sigmoid(x):
    return 1.0 / (1.0 + jnp.exp(-x))


def _norm_mod(x, g, shift, scale):
    ms = jnp.mean(x * x, axis=-1, keepdims=True)
    h = x * lax.rsqrt(ms + RMS_EPS) * g
    return h * (1.0 + scale) + shift


def _dot_exact_rhs(x, m):
    hi = x.astype(BF16)
    r1 = x - hi.astype(F32)
    mid = r1.astype(BF16)
    lo = (r1 - mid.astype(F32)).astype(BF16)
    dot = functools.partial(jnp.dot, preferred_element_type=F32)
    return dot(hi, m) + dot(mid, m) + dot(lo, m)


def _mod_spec(n_batch, ctx_tiles):
    g = SAMPLES_PER_STEP
    return pl.BlockSpec((g, 1, D_MODEL), lambda b, i: (jnp.where(i < ctx_tiles, n_batch // g, b), 0, 0))


def _row_spec(width, tile_offset=0):
    return pl.BlockSpec((SAMPLES_PER_STEP, ROW_TILE, width), lambda b, i: (b, i + tile_offset, 0))


def _stack(x):
    return x.reshape(x.shape[0] * x.shape[1], x.shape[2])


def _unstack(x):
    return x.reshape(SAMPLES_PER_STEP, x.shape[0] // SAMPLES_PER_STEP, x.shape[1])


def _whole(shape):
    return pl.BlockSpec(shape, lambda *_: (0,) * len(shape), pipeline_mode=pl.Buffered(1))


def _mod_kernel(c_ref, w_ref, b_ref, o_ref):
    c = c_ref[...]
    cs = c * _sigmoid(c)
    o_ref[0] = jnp.dot(cs, w_ref[0], precision=lax.Precision.HIGHEST,
                       preferred_element_type=F32) + b_ref[0]


def _modulation(c_all, mod_w, mod_b):
    depth, d, n = mod_w.shape
    rows = c_all.shape[0]
    tn = 768
    return pl.pallas_call(
        _mod_kernel,
        grid=(depth, n // tn),
        in_specs=[
            pl.BlockSpec((rows, d), lambda l, j: (0, 0)),
            pl.BlockSpec((1, d, tn), lambda l, j: (l, 0, j)),
            pl.BlockSpec((1, 1, tn), lambda l, j: (l, 0, j)),
        ],
        out_specs=pl.BlockSpec((1, rows, tn), lambda l, j: (l, 0, j)),
        out_shape=jax.ShapeDtypeStruct((depth, rows, n), F32),
        compiler_params=_params("arbitrary", "arbitrary"),
        name="modulation",
    )(c_all, mod_w, mod_b.reshape(depth, 1, n))


def _proj_kernel(x_ref, g_ref, sh_ref, sc_ref, w_ref, qkv_ref, rw_ref, cv_ref):
    hb = _stack(_norm_mod(x_ref[...], g_ref[...], sh_ref[...], sc_ref[...])).astype(BF16)
    for ref, base in ((qkv_ref, 0), (rw_ref, QKV_WIDTH), (cv_ref, QKV_WIDTH + RW_IN)):
        for c0, w in _col_chunks(ref.shape[-1]):
            y = jnp.dot(hb, w_ref[:, base + c0:base + c0 + w], preferred_element_type=F32)
            ref[:, :, c0:c0 + w] = _unstack(y.astype(ref.dtype))


def _in_projection(x, g, shift, scale, w_in_bf16, ctx_tiles):
    b, t, d = x.shape
    ms = _mod_spec(b, ctx_tiles)
    return pl.pallas_call(
        _proj_kernel,
        grid=(b // SAMPLES_PER_STEP, t // ROW_TILE),
        in_specs=[_row_spec(d), _whole((1, d)), ms, ms, _whole((d, IN_WIDTH))],
        out_specs=[_row_spec(QKV_WIDTH), _row_spec(RW_IN), _row_spec(2 * CV_WIDTH)],
        out_shape=[
            jax.ShapeDtypeStruct((b, t, QKV_WIDTH), BF16),
            jax.ShapeDtypeStruct((b, t, RW_IN), F32),
            jax.ShapeDtypeStruct((b, t, 2 * CV_WIDTH), F32),
        ],
        compiler_params=_params("arbitrary", "arbitrary"),
        name="in_projection",
    )(x, g.reshape(1, d), shift, scale, w_in_bf16)


def _head_pair_queries(q):
    lane = lax.broadcasted_iota(jnp.int32, q.shape, 1)
    qs = q * jnp.asarray(HEAD_DIM ** -0.5, q.dtype)
    zero = jnp.zeros_like(qs)
    return jnp.concatenate([jnp.where(lane < HEAD_DIM, qs, zero), jnp.where(lane >= HEAD_DIM, qs, zero)], axis=0)


def _head_pair_merge(o, rows):
    lane = lax.broadcasted_iota(jnp.int32, (rows, LANES), 1)
    return jnp.where(lane < HEAD_DIM, o[:rows], o[rows:])


_NT = (((1,), (1,)), ((), ()))


def _na_kernel(q_ref, k_ref, v_ref, bias_ref, o_ref, *, rows, n_ctx):
    n_win = NA_WIN_ROWS * GRID_W
    nq = 2 * GRID_W
    first = pl.program_id(2) * NA_ROWS_PER_STEP
    q0 = pl.multiple_of(n_ctx + first * GRID_W, GRID_W)
    qb_all = jnp.concatenate(
        [_head_pair_queries(q_ref[0, pl.ds(q0 + j * GRID_W, GRID_W), :]) for j in range(NA_ROWS_PER_STEP)], axis=0)
    s_ctx_all = lax.dot_general(qb_all, k_ref[0, 0:n_ctx, :], _NT, preferred_element_type=F32)
    o_win, p_ctx, denom = [], [], []
    for j in range(NA_ROWS_PER_STEP):
        i = first + j
        rs = jnp.clip(i - NA_WIN_ROWS // 2, 0, rows - NA_WIN_ROWS)
        start = pl.multiple_of(n_ctx + rs * GRID_W, GRID_W)
        bias = bias_ref[0, rs - i + NA_WIN_ROWS - 1]
        s_win = lax.dot_general(qb_all[j * nq:(j + 1) * nq], k_ref[0, pl.ds(start, n_win), :], _NT,
                                preferred_element_type=F32) + bias
        s_ctx = s_ctx_all[j * nq:(j + 1) * nq]
        m = jnp.maximum(jnp.max(s_win, axis=-1, keepdims=True), jnp.max(s_ctx, axis=-1, keepdims=True))
        p_win = jnp.exp(s_win - m)
        p_ctx.append(jnp.exp(s_ctx - m))
        denom.append(jnp.sum(p_win, axis=-1, keepdims=True) + jnp.sum(p_ctx[j], axis=-1, keepdims=True))
        o_win.append(jnp.dot(p_win.astype(BF16), v_ref[0, pl.ds(start, n_win), :], preferred_element_type=F32))
    o_ctx_all = jnp.dot(jnp.concatenate(p_ctx, axis=0).astype(BF16), v_ref[0, 0:n_ctx, :],
                        preferred_element_type=F32)
    for j in range(NA_ROWS_PER_STEP):
        o = (o_win[j] + o_ctx_all[j * nq:(j + 1) * nq]) / denom[j]
        o_ref[0, j * GRID_W:(j + 1) * GRID_W, :] = _head_pair_merge(o, GRID_W).astype(o_ref.dtype)


def _na_bias_table(rpb):
    qc = np.arange(GRID_W)[:, None]
    kc = np.arange(GRID_W)[None, :]
    ws = np.clip(qc - NA_WIN_COLS // 2, 0, GRID_W - NA_WIN_COLS)
    mask = (kc >= ws) & (kc < ws + NA_WIN_COLS)
    rel = np.clip(kc - qc + NA_WIN_COLS - 1, 0, 2 * NA_WIN_COLS - 2)
    full = jnp.where(mask[None, None], rpb[:, :, rel].astype(F32), MASK_VALUE)
    dr = np.arange(NA_WIN_ROWS)[:, None] + np.arange(NA_WIN_ROWS)[None, :]
    t = full[:, dr]
    t = t.transpose(0, 1, 3, 2, 4).reshape(NA_HEADS // 2, 2, NA_WIN_ROWS, GRID_W, NA_WIN_ROWS * GRID_W)
    return t.transpose(0, 2, 1, 3, 4).reshape(NA_HEADS // 2, NA_WIN_ROWS, 2 * GRID_W, NA_WIN_ROWS * GRID_W)


def _neighbourhood_attention(qkv, bias_table, n_ctx):
    b, t, _ = qkv.shape
    n = t - n_ctx
    rows = n // GRID_W
    hp = NA_HEADS // 2
    koff, voff = NA_WIDTH // LANES, 2 * NA_WIDTH // LANES
    qrows = NA_ROWS_PER_STEP * GRID_W
    assert rows % NA_ROWS_PER_STEP == 0
    return pl.pallas_call(
        functools.partial(_na_kernel, rows=rows, n_ctx=n_ctx),
        grid=(b, hp, rows // NA_ROWS_PER_STEP),
        in_specs=[
            pl.BlockSpec((1, t, LANES), lambda bi, h, i: (bi, 0, h)),
            pl.BlockSpec((1, t, LANES), lambda bi, h, i: (bi, 0, koff + h)),
            pl.BlockSpec((1, t, LANES), lambda bi, h, i: (bi, 0, voff + h)),
            pl.BlockSpec((1, NA_WIN_ROWS, 2 * GRID_W, NA_WIN_ROWS * GRID_W), lambda bi, h, i: (h, 0, 0, 0)),
        ],
        out_specs=pl.BlockSpec((1, qrows, LANES), lambda bi, h, i: (bi, i, h)),
        out_shape=jax.ShapeDtypeStruct((b, n, NA_WIDTH), BF16),
        compiler_params=_params("arbitrary", "arbitrary", "arbitrary"),
        name="neighbourhood_attention",
    )(qkv, qkv, qkv, bias_table)


def _ctx_attn_kernel(q_ref, k_ref, v_ref, o_ref):
    l = q_ref.shape[1]
    qb = _head_pair_queries(q_ref[0])
    s = lax.dot_general(qb, k_ref[0], _NT, preferred_element_type=F32)
    p = jnp.exp(s - jnp.max(s, axis=-1, keepdims=True))
    o = jnp.dot(p.astype(BF16), v_ref[0], preferred_element_type=F32) / jnp.sum(p, axis=-1, keepdims=True)
    o_ref[0] = _head_pair_merge(o, l).astype(o_ref.dtype)


def _context_attention(qkv, n_ctx):
    b = qkv.shape[0]
    hp = NA_HEADS // 2
    koff, voff = NA_WIDTH // LANES, 2 * NA_WIDTH // LANES
    return pl.pallas_call(
        _ctx_attn_kernel,
        grid=(b, hp),
        in_specs=[
            pl.BlockSpec((1, n_ctx, LANES), lambda bi, h: (bi, 0, h)),
            pl.BlockSpec((1, n_ctx, LANES), lambda bi, h: (bi, 0, koff + h)),
            pl.BlockSpec((1, n_ctx, LANES), lambda bi, h: (bi, 0, voff + h)),
        ],
        out_specs=pl.BlockSpec((1, n_ctx, LANES), lambda bi, h: (bi, 0, h)),
        out_shape=jax.ShapeDtypeStruct((b, n_ctx, NA_WIDTH), BF16),
        compiler_params=_params("arbitrary", "arbitrary"),
        name="context_attention",
    )(qkv, qkv, qkv)


SCAN_GROUPS = ("r", "v", "z", "w0", "k0", "b0", "w1", "k1", "b1")
POST_COLS = 4 * RW_WIDTH


def _softplus(x):
    return jnp.maximum(x, 0.0) + jnp.log(1.0 + jnp.exp(-jnp.abs(x)))


def _rw_prep_kernel(u_ref, up_ref, un_ref, mup_ref, mun_ref, ones_ref, kk_ref, ka_ref, rk_ref,
                    w0_ref, a0_ref, wa_ref, g2_ref, scan_ref, post_ref, *, n_ctx):
    i = pl.program_id(0)
    b = pl.program_id(1)
    tt = u_ref.shape[1]
    n_tok = pl.num_programs(0) * tt
    u = u_ref[0]
    row = lax.broadcasted_iota(jnp.int32, u.shape, 0)
    starts_segment = jnp.logical_or(i == 0, i * tt == n_ctx)
    ends_segment = jnp.logical_or((i + 1) * tt == n_tok, (i + 1) * tt == n_ctx)
    prev_row = jnp.where(starts_segment, 0.0, up_ref[0, SUBLANES - 1:SUBLANES, :])
    next_row = jnp.where(ends_segment, 0.0, un_ref[0, 0:1, :])
    prev = jnp.where(row == 0, prev_row, pltpu.roll(u, 1, 0))
    nxt = jnp.where(row == tt - 1, next_row, pltpu.roll(u, tt - 1, 0))
    us = u + mup_ref[...] * (prev - u) + mun_ref[...] * (nxt - u)

    def put(group, val):
        g = SCAN_GROUPS.index(group)
        for c in range(2):
            scan_ref[2 * g + c, pl.ds(b, tt, stride=SUBLANES), :] = val[:, c * LANES:(c + 1) * LANES]

    ones = ones_ref[...]
    r = us[:, 0:RW_WIDTH]
    k = us[:, RW_WIDTH:2 * RW_WIDTH]
    v = us[:, 2 * RW_WIDTH:3 * RW_WIDTH]
    kk = k * kk_ref[...]
    kk = kk * lax.rsqrt(_dot_exact_rhs(kk * kk, ones) + 1e-12)
    put("r", r)
    put("v", v)
    put("z", -kk)
    for d in range(2):
        lo = 3 * RW_WIDTH + d * RW_LORA
        wa_in = us[:, lo:lo + LANES]
        lane = lax.broadcasted_iota(jnp.int32, wa_in.shape, 1)
        wa_in = jnp.where(lane < RW_DECAY_RANK, jnp.tanh(wa_in), wa_in)
        wa = jnp.dot(wa_in.astype(BF16), wa_ref[d], preferred_element_type=F32)
        log_w = -_softplus(-(w0_ref[d] + wa[:, 0:RW_WIDTH])) - 0.5
        decay = jnp.exp(-jnp.exp(log_w))
        a = _sigmoid(a0_ref[d] + wa[:, RW_WIDTH:2 * RW_WIDTH])
        gate_in = _sigmoid(us[:, lo + LANES:lo + 2 * LANES])
        g = jnp.dot(gate_in.astype(BF16), g2_ref[d], preferred_element_type=F32)
        kd = k * (1.0 + (a - 1.0) * ka_ref[...])
        bonus = _dot_exact_rhs(r * kd * rk_ref[...], ones) * v
        put("w%d" % d, decay)
        put("k%d" % d, kd)
        put("b%d" % d, kk * a)
        post_ref[0, :, 2 * d * RW_WIDTH:(2 * d + 1) * RW_WIDTH] = g
        post_ref[0, :, (2 * d + 1) * RW_WIDTH:(2 * d + 2) * RW_WIDTH] = bonus


def _head_ones():
    h = np.arange(RW_WIDTH) // HEAD_DIM
    return jnp.asarray(h[:, None] == h[None, :], BF16)


def _rw_prep(u, p, n_ctx):
    b, t, _ = u.shape
    tt = RW_TILE
    assert b == SUBLANES and n_ctx % tt == 0 and t % tt == 0
    nt8 = t // SUBLANES
    row = lambda vec, n: vec.reshape(1, n)
    return pl.pallas_call(
        functools.partial(_rw_prep_kernel, n_ctx=n_ctx),
        grid=(t // tt, b),
        in_specs=[
            pl.BlockSpec((1, tt, RW_IN), lambda i, bi: (bi, i, 0)),
            pl.BlockSpec((1, SUBLANES, RW_IN), lambda i, bi: (bi, jnp.maximum(i * (tt // SUBLANES) - 1, 0), 0)),
            pl.BlockSpec((1, SUBLANES, RW_IN), lambda i, bi: (bi, jnp.minimum((i + 1) * (tt // SUBLANES), nt8 - 1), 0)),
            _whole((1, RW_IN)), _whole((1, RW_IN)),
            _whole((RW_WIDTH, RW_WIDTH)),
            _whole((1, RW_WIDTH)), _whole((1, RW_WIDTH)), _whole((1, RW_WIDTH)),
            _whole((2, 1, RW_WIDTH)), _whole((2, 1, RW_WIDTH)),
            _whole((2, LANES, 2 * RW_WIDTH)), _whole((2, RW_GATE_RANK, RW_WIDTH)),
        ],
        out_specs=[
            pl.BlockSpec((2 * len(SCAN_GROUPS), tt * b, LANES), lambda i, bi: (0, i, 0)),
            pl.BlockSpec((1, tt, POST_COLS), lambda i, bi: (bi, i, 0)),
        ],
        out_shape=[
            jax.ShapeDtypeStruct((2 * len(SCAN_GROUPS), t * b, LANES), F32),
            jax.ShapeDtypeStruct((b, t, POST_COLS), F32),
        ],
        compiler_params=_params("arbitrary", "arbitrary"),
        name="rwkv_prep",
    )(u, u, u, row(p["mu_prev"], RW_IN), row(p["mu_next"], RW_IN), _head_ones(),
      row(p["k_k"], RW_WIDTH), row(p["k_a"], RW_WIDTH), row(p["r_k"], RW_WIDTH),
      p["w0"].reshape(2, 1, RW_WIDTH), p["a0"].reshape(2, 1, RW_WIDTH), p["wa"], p["g2"])


def _rw_lora_weights(w2, a2):
    z = jnp.zeros_like(w2)
    top = jnp.concatenate([w2, z], axis=-1)
    bot = jnp.concatenate([z, a2], axis=-1)
    return jnp.concatenate([top, bot], axis=1).astype(BF16)


_QUARTER = LANES // 4
_VROWS = HEAD_DIM // 4


def _chain_tile(f_ref, b_ref, sf, sb):
    pieces = [f_ref[0, pl.ds(sf, SUBLANES), :], f_ref[1, pl.ds(sf, SUBLANES), :],
              b_ref[0, pl.ds(sb, SUBLANES), :], b_ref[1, pl.ds(sb, SUBLANES), :]]
    return jnp.concatenate(pieces * 4, axis=0).T


def _scan_kernel(rf, vf, zf, wf, kf, bf, rb, vb, zb, wb, kb, bb, yf_ref, yb_ref, s_ref, t_ref, v_ref, y_ref):
    steps = rf.shape[1] // SUBLANES
    zero = jnp.zeros((_VROWS, LANES), F32)

    @pl.when(pl.program_id(0) == 0)
    def _():
        s_ref[...] = jnp.zeros_like(s_ref)
        y_ref[...] = jnp.zeros_like(y_ref)

    lane_q = lax.broadcasted_iota(jnp.int32, (_VROWS, LANES), 1) // _QUARTER
    col_q = (lax.broadcasted_iota(jnp.int32, (_QUARTER, LANES), 1) % HEAD_DIM) // _VROWS

    def offsets(s):
        s = jnp.clip(s, 0, steps - 1)
        return pl.multiple_of(s * SUBLANES, SUBLANES), pl.multiple_of((steps - 1 - s) * SUBLANES, SUBLANES)

    def prepare(s, p):
        sf, sb = offsets(s)
        zf_off, zb_off = offsets(s + 1)
        t_ref[p, 0] = _chain_tile(zf, zb, zf_off, zb_off)
        for n, (f, b) in enumerate(((wf, wb), (bf, bb), (kf, kb), (rf, rb))):
            t_ref[p, n + 1] = _chain_tile(f, b, sf, sb)
        vt = _chain_tile(vf, vb, sf, sb)
        for h in range(2):
            v = zero
            for q in range(4):
                lo = h * HEAD_DIM + q * _VROWS
                v = jnp.where(lane_q == q, vt[lo:lo + _VROWS], v)
            v_ref[p, h] = v

    def flush(s, p):
        sf, sb = offsets(s)
        yt = jnp.concatenate([y_ref[p, 0]] * 4 + [y_ref[p, 1]] * 4, axis=0).T
        nat = jnp.zeros((_QUARTER, LANES), F32)
        for q in range(4):
            nat = jnp.where(col_q == q, yt[q * _QUARTER:(q + 1) * _QUARTER], nat)
        yf_ref[0, pl.ds(sf, SUBLANES), :] = nat[0:8]
        yf_ref[1, pl.ds(sf, SUBLANES), :] = nat[8:16]
        yb_ref[0, pl.ds(sb, SUBLANES), :] = nat[16:24]
        yb_ref[1, pl.ds(sb, SUBLANES), :] = nat[24:32]

    def update(p, sz):
        sz_next = []
        for h in range(2):
            base = h * HEAD_DIM
            v = v_ref[p, h]
            y = [zero, zero]
            zn = [zero, zero]
            for k in range(HEAD_DIM):
                row = pl.ds(base + k, 1)
                st = s_ref[base + k] * t_ref[p, 1, row, :] + sz[h] * t_ref[p, 2, row, :] + v * t_ref[p, 3, row, :]
                s_ref[base + k] = st
                y[k % 2] = y[k % 2] + st * t_ref[p, 4, row, :]
                zn[k % 2] = zn[k % 2] + st * t_ref[p, 0, row, :]
            y_ref[p, h] = y[0] + y[1]
            sz_next.append(zn[0] + zn[1])
        return tuple(sz_next)

    f0, b0 = offsets(0)
    t_ref[1, 0] = _chain_tile(zf, zb, f0, b0)
    sz0 = []
    for h in range(2):
        acc = [zero, zero]
        for k in range(HEAD_DIM):
            r = h * HEAD_DIM + k
            acc[k % 2] = acc[k % 2] + s_ref[r] * t_ref[1, 0, pl.ds(r, 1), :]
        sz0.append(acc[0] + acc[1])
    prepare(0, 0)

    def pair(j, sz):
        s = 2 * j
        for p in range(2):
            prepare(s + p + 1, 1 - p)
            sz = update(p, sz)
            flush(s + p - 1, 1 - p)
        return sz

    lax.fori_loop(0, steps // 2, pair, tuple(sz0))
    flush(steps - 1, 1)


def _rwkv7_scan(ops, n_batch, n_ctx):
    rows = ops.shape[1]
    blk = SCAN_STEPS * n_batch
    nb = rows // blk
    nb_c = n_ctx * n_batch // blk

    def mirror(i):
        return jnp.where(i < nb_c, nb_c - 1 - i, nb_c + nb - 1 - i)

    def spec(group, backward):
        g = SCAN_GROUPS.index(group)
        if backward:
            return pl.BlockSpec((2, blk, LANES), lambda i: (g, mirror(i), 0))
        return pl.BlockSpec((2, blk, LANES), lambda i: (g, i, 0))

    fw = [spec(g, False) for g in ("r", "v", "z", "w0", "k0", "b0")]
    bw = [spec(g, True) for g in ("r", "v", "z", "w1", "k1", "b1")]
    out_sds = jax.ShapeDtypeStruct((2, rows, LANES), F32)
    return pl.pallas_call(
        _scan_kernel,
        grid=(nb,),
        in_specs=fw + bw,
        out_specs=[pl.BlockSpec((2, blk, LANES), lambda i: (0, i, 0)),
                   pl.BlockSpec((2, blk, LANES), lambda i: (0, mirror(i), 0))],
        out_shape=[out_sds, out_sds],
        scratch_shapes=[pltpu.VMEM((2 * HEAD_DIM, _VROWS, LANES), F32), pltpu.VMEM((2, 5, LANES, LANES), F32),
                        pltpu.VMEM((2, 2, _VROWS, LANES), F32), pltpu.VMEM((2, 2, _VROWS, LANES), F32)],
        compiler_params=_params("arbitrary"),
        name="rwkv_scan",
    )(*([ops] * 12))


def _rw_post_kernel(yf_ref, yb_ref, post_ref, ones_ref, gg_ref, gb_ref, o_ref):
    b = pl.program_id(1)
    tt = o_ref.shape[1]
    ones = ones_ref[...]
    out = None
    for d, y_ref in enumerate((yf_ref, yb_ref)):
        y = jnp.concatenate([y_ref[0, pl.ds(b, tt, stride=SUBLANES), :],
                             y_ref[1, pl.ds(b, tt, stride=SUBLANES), :]], axis=1)
        mu = _dot_exact_rhs(y, ones) * (1.0 / HEAD_DIM)
        yc = y - mu
        var = _dot_exact_rhs(yc * yc, ones) * (1.0 / HEAD_DIM)
        yn = yc * lax.rsqrt(var + RW_GN_EPS) * gg_ref[...] + gb_ref[...]
        g = post_ref[0, :, 2 * d * RW_WIDTH:(2 * d + 1) * RW_WIDTH]
        bonus = post_ref[0, :, (2 * d + 1) * RW_WIDTH:(2 * d + 2) * RW_WIDTH]
        term = (yn + bonus) * g
        out = term if out is None else out + term
    o_ref[0] = out.astype(o_ref.dtype)


def _rw_post(y_f, y_b, post, gn_g, gn_b):
    b, t, _ = post.shape
    tt = RW_TILE
    yspec = pl.BlockSpec((2, tt * b, LANES), lambda i, bi: (0, i, 0))
    return pl.pallas_call(
        _rw_post_kernel,
        grid=(t // tt, b),
        in_specs=[yspec, yspec, pl.BlockSpec((1, tt, POST_COLS), lambda i, bi: (bi, i, 0)),
                  _whole((RW_WIDTH, RW_WIDTH)), _whole((1, RW_WIDTH)), _whole((1, RW_WIDTH))],
        out_specs=pl.BlockSpec((1, tt, RW_WIDTH), lambda i, bi: (bi, i, 0)),
        out_shape=jax.ShapeDtypeStruct((b, t, RW_WIDTH), BF16),
        compiler_params=_params("arbitrary", "arbitrary"),
        name="rwkv_post",
    )(y_f, y_b, post, _head_ones(), gn_g.reshape(1, RW_WIDTH), gn_b.reshape(1, RW_WIDTH))


def _bi_rwkv7(u, p, n_ctx):
    ops, post = _rw_prep(u, p, n_ctx)
    y_f, y_b = _rwkv7_scan(ops, u.shape[0], n_ctx)
    return _rw_post(y_f, y_b, post, p["gn_g"], p["gn_b"])


CONV_PAD = 16


def _conv_kernel(u_ref, w_ref, b_ref, lg_ref, lb_ref, o_ref, pad_ref, shift_ref, *, n_ctx):
    t = u_ref.shape[1]
    half = CV_CONV_LEN // 2
    chunk = ROW_TILE
    u = u_ref[0]
    h = u[:, :CV_WIDTH] * _sigmoid(u[:, CV_WIDTH:])
    zeros = jnp.zeros((CONV_PAD, CV_WIDTH), F32)
    lat0 = 2 * CONV_PAD + n_ctx
    pad_ref[0:CONV_PAD, :] = zeros
    pad_ref[CONV_PAD:CONV_PAD + n_ctx, :] = h[:n_ctx]
    pad_ref[CONV_PAD + n_ctx:lat0, :] = zeros
    pad_ref[lat0:lat0 + t - n_ctx, :] = h[n_ctx:]
    pad_ref[lat0 + t - n_ctx:lat0 + t - n_ctx + CONV_PAD, :] = zeros
    w = w_ref[...]

    def body(c, carry):
        out0 = pl.multiple_of(c * chunk, chunk)
        base = pl.multiple_of(out0 + jnp.where(c >= n_ctx // chunk, CONV_PAD, 0), CONV_PAD)
        win = pad_ref[pl.ds(base, chunk + 2 * CONV_PAD), :]
        acc = jnp.zeros((chunk, CV_WIDTH), F32)
        for r in range(SUBLANES):
            offs = [o for o in range(CONV_PAD - half, CONV_PAD + half + 1) if o % SUBLANES == r]
            if not offs:
                continue
            shift_ref[r] = win[r:r + shift_ref.shape[1], :]
            for off in offs:
                j = off - (CONV_PAD - half)
                acc = acc + shift_ref[r, off - r:off - r + chunk, :] * w[j:j + 1, :]
        hh = acc + b_ref[...]
        mu = jnp.mean(hh, axis=-1, keepdims=True)
        hc = hh - mu
        var = jnp.mean(hc * hc, axis=-1, keepdims=True)
        y = hc * lax.rsqrt(var + LN_EPS) * lg_ref[...] + lb_ref[...]
        o_ref[0, pl.ds(out0, chunk), :] = (y * _sigmoid(y)).astype(o_ref.dtype)
        return carry

    lax.fori_loop(0, t // chunk, body, 0)


def _conformer_conv(u, dw_w, dw_b, ln_g, ln_b, n_ctx):
    b, t, _ = u.shape
    assert n_ctx % ROW_TILE == 0 and t % ROW_TILE == 0
    row = lambda vec: vec.reshape(1, CV_WIDTH)
    return pl.pallas_call(
        functools.partial(_conv_kernel, n_ctx=n_ctx),
        grid=(b,),
        in_specs=[pl.BlockSpec((1, t, 2 * CV_WIDTH), lambda bi: (bi, 0, 0)),
                  _whole((CV_CONV_LEN, CV_WIDTH)), _whole((1, CV_WIDTH)), _whole((1, CV_WIDTH)),
                  _whole((1, CV_WIDTH))],
        out_specs=pl.BlockSpec((1, t, CV_WIDTH), lambda bi: (bi, 0, 0)),
        out_shape=jax.ShapeDtypeStruct((b, t, CV_WIDTH), BF16),
        scratch_shapes=[pltpu.VMEM((t + 3 * CONV_PAD, CV_WIDTH), F32),
                        pltpu.VMEM((SUBLANES, ROW_TILE + 2 * CONV_PAD - SUBLANES, CV_WIDTH), F32)],
        compiler_params=_params("arbitrary"),
        name="conformer_conv",
    )(u, dw_w, row(dw_b), row(ln_g), row(ln_b))


def _mixed_residual(x_ref, nal_ref, nac_ref, rw_ref, cv_ref, w_ref, gate_ref, ctx_tiles):
    dot = functools.partial(jnp.dot, preferred_element_type=F32)
    na = nal_ref[...]
    if ctx_tiles:
        na = jnp.where(pl.program_id(1) < ctx_tiles, nac_ref[...], na)
    acc = (dot(_stack(na), w_ref[0:NA_WIDTH, :])
           + dot(_stack(rw_ref[...]), w_ref[NA_WIDTH:NA_WIDTH + RW_WIDTH, :])
           + dot(_stack(cv_ref[...]), w_ref[NA_WIDTH + RW_WIDTH:, :]))
    return x_ref[...] + gate_ref[...] * _unstack(acc)


def _mix_specs(b, t, n_ctx, with_ctx):
    d = D_MODEL
    g = SAMPLES_PER_STEP
    ctx_tiles = n_ctx // ROW_TILE
    assert ctx_tiles == 1 or not with_ctx
    off = 0 if with_ctx else ctx_tiles
    n_rows = t if with_ctx else t - n_ctx
    if with_ctx:
        nal_spec = pl.BlockSpec((g, ROW_TILE, NA_WIDTH), lambda bi, i: (bi, jnp.maximum(i - ctx_tiles, 0), 0))
    else:
        nal_spec = _row_spec(NA_WIDTH)
    specs = [_row_spec(d, off), nal_spec, pl.BlockSpec((g, ROW_TILE, NA_WIDTH), lambda bi, i: (bi, 0, 0)),
             _row_spec(RW_WIDTH, off), _row_spec(CV_WIDTH, off), _whole((d, d)),
             _mod_spec(b, ctx_tiles if with_ctx else 0)]
    return (b // g, n_rows // ROW_TILE), specs, n_rows, (ctx_tiles if with_ctx else 0)


def _mix_ffn_kernel(x_ref, nal_ref, nac_ref, rw_ref, cv_ref, wo_ref, g1_ref, g_ref, sh_ref, sc_ref, g2_ref,
                    wg_ref, wu_ref, wd_ref, o_ref, acc_ref, *, ctx_tiles):
    x = _mixed_residual(x_ref, nal_ref, nac_ref, rw_ref, cv_ref, wo_ref, g1_ref, ctx_tiles)
    hb = _stack(_norm_mod(x, g_ref[...], sh_ref[...], sc_ref[...])).astype(BF16)
    dot = functools.partial(jnp.dot, preferred_element_type=F32)
    for n, (c0, w) in enumerate(_col_chunks(wg_ref.shape[1])):
        g = dot(hb, wg_ref[:, c0:c0 + w])
        u = dot(hb, wu_ref[:, c0:c0 + w])
        a = (g * _sigmoid(g) * u).astype(BF16)
        y = dot(a, wd_ref[c0:c0 + w, :])
        if n == 0:
            acc_ref[...] = y
        else:
            acc_ref[...] += y
    o_ref[...] = x + g2_ref[...] * _unstack(acc_ref[...])


def _mix_dense_ffn(x, na_l, na_c, rw, cv, w_out_bf16, gate1, g, shift, scale, gate2, wg, wu, wd, n_ctx, with_ctx):
    b, t, d = x.shape
    dff = wg.shape[1]
    grid, mix_specs, n_rows, ctx_tiles = _mix_specs(b, t, n_ctx, with_ctx)
    ms = _mod_spec(b, ctx_tiles)
    return pl.pallas_call(
        functools.partial(_mix_ffn_kernel, ctx_tiles=ctx_tiles),
        grid=grid,
        in_specs=mix_specs + [_whole((1, d)), ms, ms, ms, _whole((d, dff)), _whole((d, dff)), _whole((dff, d))],
        out_specs=_row_spec(d),
        out_shape=jax.ShapeDtypeStruct((b, n_rows, d), F32),
        scratch_shapes=[pltpu.VMEM((SAMPLES_PER_STEP * ROW_TILE, d), F32)],
        compiler_params=_params("arbitrary", "arbitrary"),
        name="mix_dense_ffn",
    )(x, na_l, na_l if na_c is None else na_c, rw, cv, w_out_bf16, gate1, g.reshape(1, d), shift, scale, gate2,
      wg, wu, wd)


def _rows_to_tiles(ref2d, val):
    rows = val.shape[0]
    for s in range(val.shape[1] // LANES):
        ref2d[pl.ds(s, rows, stride=SUBLANES), :] = val[:, s * LANES:(s + 1) * LANES]


def _rows_from_tiles(ref, lead, first_row, rows):
    return jnp.concatenate([ref[lead, pl.ds(first_row * SUBLANES + s, rows, stride=SUBLANES), :]
                            for s in range(SUBLANES)], axis=1)


def _mix_router_kernel(x_ref, nal_ref, nac_ref, rw_ref, cv_ref, wo_ref, g1_ref, g_ref, sh_ref, sc_ref, wr_ref,
                       x1_ref, h_ref, logit_ref):
    x = _mixed_residual(x_ref, nal_ref, nac_ref, rw_ref, cv_ref, wo_ref, g1_ref, 0)
    x1_ref[...] = x
    h = _norm_mod(x, g_ref[...], sh_ref[...], sc_ref[...])
    for s in range(h.shape[0]):
        _rows_to_tiles(h_ref.at[s], h[s])
    logit_ref[...] = _unstack(jnp.dot(_stack(h), wr_ref[...], precision=lax.Precision.HIGHEST,
                                      preferred_element_type=F32))


def _mix_moe_router(x, na_l, rw, cv, w_out_bf16, gate1, g, shift, scale, router_pad, n_ctx):
    b, t, d = x.shape
    assert d == SUBLANES * LANES
    grid, mix_specs, n, _ = _mix_specs(b, t, n_ctx, with_ctx=False)
    ms = _mod_spec(b, 0)
    return pl.pallas_call(
        _mix_router_kernel,
        grid=grid,
        in_specs=mix_specs + [_whole((1, d)), ms, ms, _whole((d, ROUTER_PAD))],
        out_specs=[_row_spec(d),
                   pl.BlockSpec((SAMPLES_PER_STEP, ROW_TILE * SUBLANES, LANES), lambda bi, i: (bi, i, 0)),
                   _row_spec(ROUTER_PAD)],
        out_shape=[jax.ShapeDtypeStruct((b, n, d), F32),
                   jax.ShapeDtypeStruct((b, n * SUBLANES, LANES), F32),
                   jax.ShapeDtypeStruct((b, n, ROUTER_PAD), F32)],
        compiler_params=_params("arbitrary", "arbitrary"),
        name="mix_moe_router",
    )(x, na_l, na_l, rw, cv, w_out_bf16, gate1, g.reshape(1, d), shift, scale, router_pad)


def _tile_copy(src_hbm, dst_vmem, sem, src_row, dst_row):
    dst = dst_vmem.at[pl.ds(pl.multiple_of(dst_row * SUBLANES, SUBLANES), SUBLANES)]
    return pltpu.make_async_copy(src_hbm.at[src_row], dst, sem)


GATHER_UNROLL = 8


def _start_row_gather(src_hbm, dst_vmem, sem, index_of_row, rows):
    def body(r, c):
        _tile_copy(src_hbm, dst_vmem, sem, index_of_row(r), r).start()
        return c

    lax.fori_loop(0, rows, body, 0, unroll=GATHER_UNROLL)


def _wait_row_gather(src_hbm, dst_vmem, sem, rows):
    def body(r, c):
        _tile_copy(src_hbm, dst_vmem, sem, 0, r).wait()
        return c

    lax.fori_loop(0, rows, body, 0, unroll=GATHER_UNROLL)


def _expert_kernel(be_ref, nused_ref, tok_ref, tok_next_ref, h_hbm, wg_ref, wu_ref, wd_ref, o_ref,
                   xg_ref, acc_ref, sem):
    i = pl.program_id(0)
    rows = o_ref.shape[0] // SUBLANES
    n_used = nused_ref[0]
    used = i < n_used
    slot = i % 2

    @pl.when(i == 0)
    def _():
        _start_row_gather(h_hbm, xg_ref.at[0], sem.at[0], lambda r: tok_ref[0, 0, r], rows)

    @pl.when(i + 1 < n_used)
    def _():
        _start_row_gather(h_hbm, xg_ref.at[1 - slot], sem.at[1 - slot], lambda r: tok_next_ref[0, 0, r], rows)

    @pl.when(used)
    def _():
        _wait_row_gather(h_hbm, xg_ref.at[slot], sem.at[slot], rows)
        dot = functools.partial(jnp.dot, preferred_element_type=F32)
        xb = _rows_from_tiles(xg_ref, slot, 0, rows).astype(BF16)
        for n, (c0, w) in enumerate(_col_chunks(wg_ref.shape[2], MOE_FF_TILE)):
            g = dot(xb, wg_ref[0, :, c0:c0 + w])
            u = dot(xb, wu_ref[0, :, c0:c0 + w])
            a = (g * _sigmoid(g) * u).astype(BF16)
            y = dot(a, wd_ref[0, c0:c0 + w, :])
            if n == 0:
                acc_ref[...] = y
            else:
                acc_ref[...] += y
        _rows_to_tiles(o_ref, acc_ref[...])

    @pl.when(jnp.logical_not(used))
    def _():
        o_ref[...] = jnp.zeros_like(o_ref)


def _moe_experts(h_flat, slot_tok, block_e, n_used, wg, wu, wd):
    nb = slot_tok.shape[0]
    d = wg.shape[1]
    dff = wg.shape[2]
    once = pl.Buffered(1)
    grid_spec = pltpu.PrefetchScalarGridSpec(
        num_scalar_prefetch=2,
        grid=(nb,),
        in_specs=[
            pl.BlockSpec((1, 1, MOE_ROWS), lambda i, be, nu: (i, 0, 0), memory_space=pltpu.SMEM),
            pl.BlockSpec((1, 1, MOE_ROWS), lambda i, be, nu: (jnp.minimum(i + 1, nb - 1), 0, 0),
                         memory_space=pltpu.SMEM),
            pl.BlockSpec(memory_space=pl.ANY),
            pl.BlockSpec((1, d, dff), lambda i, be, nu: (be[i], 0, 0), pipeline_mode=once),
            pl.BlockSpec((1, d, dff), lambda i, be, nu: (be[i], 0, 0), pipeline_mode=once),
            pl.BlockSpec((1, dff, d), lambda i, be, nu: (be[i], 0, 0), pipeline_mode=once),
        ],
        out_specs=pl.BlockSpec((MOE_ROWS * SUBLANES, LANES), lambda i, be, nu: (i, 0)),
        scratch_shapes=[pltpu.VMEM((2, MOE_ROWS * SUBLANES, LANES), F32), pltpu.VMEM((MOE_ROWS, d), F32),
                        pltpu.SemaphoreType.DMA((2,))],
    )
    yb = pl.pallas_call(
        _expert_kernel,
        grid_spec=grid_spec,
        out_shape=jax.ShapeDtypeStruct((nb * MOE_ROWS * SUBLANES, LANES), F32),
        compiler_params=_params("arbitrary"),
        name="moe_experts",
    )(block_e, n_used, slot_tok, slot_tok, h_flat, wg, wu, wd)
    return yb.reshape(nb * MOE_ROWS, SUBLANES, LANES)


def _combine_kernel(dest_ref, dest_next_ref, x_ref, gates_ref, gate2_ref, fg_ref, yb_hbm, o_ref, y_ref, sem):
    i = pl.program_id(0)
    rows = x_ref.shape[1]
    slot = i % 2

    @pl.when(i == 0)
    def _():
        _start_row_gather(yb_hbm, y_ref.at[0], sem.at[0], lambda r: dest_ref[0, 0, r], TOP_K * rows)

    @pl.when(i + 1 < pl.num_programs(0))
    def _():
        _start_row_gather(yb_hbm, y_ref.at[1 - slot], sem.at[1 - slot], lambda r: dest_next_ref[0, 0, r],
                          TOP_K * rows)

    _wait_row_gather(yb_hbm, y_ref.at[slot], sem.at[slot], TOP_K * rows)
    gates = gates_ref[0]
    y = (_rows_from_tiles(y_ref, slot, 0, rows) * gates[:, 0:1]
         + _rows_from_tiles(y_ref, slot, rows, rows) * gates[:, 1:2])
    x = x_ref[0] + gate2_ref[0] * y
    ms = jnp.mean(x * x, axis=-1, keepdims=True)
    o_ref[0] = x * lax.rsqrt(ms + RMS_EPS) * fg_ref[...]


def _moe_combine_final(x, gates, dest, yb, gate2, final_g):
    nt, tm, d = x.shape
    tiles_per_sample = nt // gate2.shape[0]
    tile = lambda w: pl.BlockSpec((1, tm, w), lambda i: (i, 0, 0))
    return pl.pallas_call(
        _combine_kernel,
        grid=(nt,),
        in_specs=[
            pl.BlockSpec((1, 1, TOP_K * tm), lambda i: (i, 0, 0), memory_space=pltpu.SMEM),
            pl.BlockSpec((1, 1, TOP_K * tm), lambda i: (jnp.minimum(i + 1, nt - 1), 0, 0),
                         memory_space=pltpu.SMEM),
            tile(d), tile(TOP_K),
            pl.BlockSpec((1, 1, d), lambda i: (i // tiles_per_sample, 0, 0)),
            pl.BlockSpec((1, d), lambda i: (0, 0)),
            pl.BlockSpec(memory_space=pl.ANY),
        ],
        out_specs=tile(d),
        out_shape=jax.ShapeDtypeStruct((nt, tm, d), F32),
        scratch_shapes=[pltpu.VMEM((2, TOP_K * tm * SUBLANES, LANES), F32), pltpu.SemaphoreType.DMA((2,))],
        compiler_params=_params("arbitrary"),
        name="moe_combine_final",
    )(dest, dest, x, gates, gate2, final_g.reshape(1, d), yb)


def _moe_layer_final(x, h, logits, gate2, wg, wu, wd, final_g):
    b, t, d = x.shape
    n_tok = b * t
    n_asg = n_tok * TOP_K
    logits = logits.reshape(n_tok, ROUTER_PAD)[:, :N_EXPERTS]
    top_logit, top_e = lax.top_k(logits, TOP_K)
    gates = jax.nn.softmax(top_logit, axis=-1)
    flat_e = top_e.reshape(-1)
    onehot = (flat_e[:, None] == jnp.arange(N_EXPERTS, dtype=flat_e.dtype)[None, :]).astype(jnp.int32)
    rank = jnp.cumsum(onehot, axis=0) - onehot
    counts = jnp.sum(onehot, axis=0)
    padded = (counts + MOE_ROWS - 1) // MOE_ROWS * MOE_ROWS
    pad_end = jnp.cumsum(padded)
    pad_start = pad_end - padded
    dest = (pad_start[flat_e] + jnp.sum(rank * onehot, axis=1)).astype(jnp.int32)
    n_blocks = (n_asg + MOE_ROWS - 1) // MOE_ROWS + N_EXPERTS
    block_e = jnp.minimum(jnp.searchsorted(pad_end, jnp.arange(n_blocks) * MOE_ROWS, side="right"),
                          N_EXPERTS - 1).astype(jnp.int32)
    order = jnp.argsort(flat_e, stable=True).astype(jnp.int32)
    slot = jnp.arange(n_blocks * MOE_ROWS, dtype=jnp.int32)
    slot_e = jnp.repeat(block_e, MOE_ROWS)
    slot_rank = slot - pad_start[slot_e].astype(jnp.int32)
    first = (jnp.cumsum(counts) - counts).astype(jnp.int32)
    src = jnp.clip(first[slot_e] + slot_rank, 0, n_asg - 1)
    slot_tok = jnp.where(slot_rank < counts[slot_e], order[src] // TOP_K, 0).astype(jnp.int32)
    n_used = (pad_end[-1:] // MOE_ROWS).astype(jnp.int32)
    yb = _moe_experts(h.reshape(n_tok, SUBLANES, LANES), slot_tok.reshape(n_blocks, 1, MOE_ROWS), block_e, n_used,
                      wg, wu, wd)
    nt = n_tok // ROW_TILE
    tile_dest = dest.reshape(nt, ROW_TILE, TOP_K).transpose(0, 2, 1).reshape(nt, 1, TOP_K * ROW_TILE)
    out = _moe_combine_final(x.reshape(nt, ROW_TILE, d), gates.reshape(nt, ROW_TILE, TOP_K), tile_dest, yb,
                             gate2[:b], final_g)
    return out.reshape(b, t, d)


def _final_norm_kernel(x_ref, g_ref, o_ref):
    x = x_ref[...]
    ms = jnp.mean(x * x, axis=-1, keepdims=True)
    o_ref[...] = x * lax.rsqrt(ms + RMS_EPS) * g_ref[...]


def _final_norm(x, g, tile_offset):
    b, t, d = x.shape
    n_rows = t - tile_offset * ROW_TILE
    return pl.pallas_call(
        _final_norm_kernel,
        grid=(b // SAMPLES_PER_STEP, n_rows // ROW_TILE),
        in_specs=[_row_spec(d, tile_offset), _whole((1, d))],
        out_specs=_row_spec(d),
        out_shape=jax.ShapeDtypeStruct((b, n_rows, d), F32),
        compiler_params=_params("arbitrary", "arbitrary"),
        name="final_norm",
    )(x, g.reshape(1, d))


def kernel(x, c, ctx, c_ctx, norm1_g, norm2_g, mod_w, mod_b, w_in, w_out, na_rpb, rw_mu_prev, rw_mu_next, rw_w0, rw_w2, rw_a0, rw_a2, rw_g2, rw_k_k, rw_k_a, rw_r_k, rw_gn_g, rw_gn_b, cv_dw_w, cv_dw_b, cv_ln_g, cv_ln_b, ffn_w_gate, ffn_w_up, ffn_w_down, moe_router, moe_w_gate, moe_w_up, moe_w_down, final_g):
    b, n, _ = x.shape
    n_ctx = ctx.shape[1]
    depth = mod_w.shape[0]
    assert n_ctx == ROW_TILE and n % ROW_TILE == 0 and b == SUBLANES
    ctx_tiles = n_ctx // ROW_TILE
    c_rows = 2 * SUBLANES
    n_mod = b + SAMPLES_PER_STEP
    c_all = jnp.concatenate([c, jnp.broadcast_to(c_ctx[None, :], (SAMPLES_PER_STEP, D_MODEL)),
                             jnp.zeros((c_rows - n_mod, D_MODEL), F32)], axis=0)
    mod = _modulation(c_all, mod_w, mod_b)

    xa = jnp.concatenate([ctx, x], axis=1)
    out = None
    for layer in range(depth):
        last = layer == depth - 1
        m = mod[layer, :n_mod].reshape(n_mod, 6, 1, D_MODEL)
        sh1, sc1, g1, sh2, sc2, g2 = (m[:, k] for k in range(6))
        w_in_b = w_in[layer].astype(BF16)
        w_out_b = w_out[layer].astype(BF16)
        qkv, rw_in, cv_in = _in_projection(xa, norm1_g[layer], sh1, sc1, w_in_b, ctx_tiles)
        na_l = _neighbourhood_attention(qkv, _na_bias_table(na_rpb[layer]), n_ctx)
        rw_p = dict(mu_prev=rw_mu_prev[layer], mu_next=rw_mu_next[layer], w0=rw_w0[layer], a0=rw_a0[layer],
                    wa=_rw_lora_weights(rw_w2[layer], rw_a2[layer]), g2=rw_g2[layer].astype(BF16),
                    k_k=rw_k_k[layer], k_a=rw_k_a[layer], r_k=rw_r_k[layer].reshape(-1),
                    gn_g=rw_gn_g[layer], gn_b=rw_gn_b[layer])
        rw_o = _bi_rwkv7(rw_in, rw_p, n_ctx)
        cv_o = _conformer_conv(cv_in, cv_dw_w[layer], cv_dw_b[layer], cv_ln_g[layer], cv_ln_b[layer], n_ctx)
        na_c = None if last else _context_attention(qkv, n_ctx)
        j = layer // 2
        if layer % 2 == 0:
            ffn_w = (ffn_w_gate[j].astype(BF16), ffn_w_up[j].astype(BF16), ffn_w_down[j].astype(BF16))
            xa = _mix_dense_ffn(xa, na_l, na_c, rw_o, cv_o, w_out_b, g1, norm2_g[layer], sh2, sc2, g2, *ffn_w,
                                n_ctx, with_ctx=not last)
            if last:
                out = _final_norm(xa, final_g, 0)
        else:
            if not last:
                raise NotImplementedError("context tokens through a mixture-of-experts layer")
            moe_w = (moe_w_gate[j].astype(BF16), moe_w_up[j].astype(BF16), moe_w_down[j].astype(BF16))
            router_pad = jnp.pad(moe_router[j], ((0, 0), (0, ROUTER_PAD - N_EXPERTS)))
            x1, h, logits = _mix_moe_router(xa, na_l, rw_o, cv_o, w_out_b, g1, norm2_g[layer], sh2, sc2,
                                            router_pad, n_ctx)
            out = _moe_layer_final(x1, h, logits, g2, *moe_w, final_g)
    return out
```

```python
import functools

import jax
import jax.numpy as jnp
import numpy as np
from jax import lax
from jax.experimental import pallas as pl
from jax.experimental.pallas import tpu as pltpu

F32 = jnp.float32
BF16 = jnp.bfloat16

D_MODEL = 1024
GRID_W = 64
HEAD_DIM = 64
NA_WIDTH = 512
NA_HEADS = 8
NA_WIN_ROWS = 8
NA_WIN_COLS = 16
RW_WIDTH = 256
RW_HEADS = 4
RW_DECAY_RANK = 64
RW_LORA = 256
RW_GATE_RANK = 128
RW_IN = 1280
RW_GN_EPS = 64e-5
CV_WIDTH = 256
CV_CONV_LEN = 31
QKV_WIDTH = 3 * NA_WIDTH
IN_WIDTH = QKV_WIDTH + RW_IN + 2 * CV_WIDTH
N_EXPERTS = 8
TOP_K = 2
RMS_EPS = 1e-6
LN_EPS = 1e-5

LANES = 128
SUBLANES = 8
VMEM_LIMIT_BYTES = 56 * 1024 * 1024

ROW_TILE = 256
SAMPLES_PER_STEP = 4
MASK_VALUE = -1e30
NA_ROWS_PER_STEP = 16
MOE_ROWS = 1024
MOE_FF_TILE = 512
ROUTER_PAD = LANES
SCAN_STEPS = 64
RW_TILE = 256


def _params(*sem):
    return pltpu.CompilerParams(dimension_semantics=sem, vmem_limit_bytes=VMEM_LIMIT_BYTES)


def _col_chunks(width, step=512):
    out, c = [], 0
    while c < width:
        w = min(step, width - c)
        out.append((c, w))
        c += w
    return out


def _sigmoid(x):
    return 1.0 / (1.0 + jnp.exp(-x))


def _norm_mod(x, g, shift, scale):
    ms = jnp.mean(x * x, axis=-1, keepdims=True)
    h = x * lax.rsqrt(ms + RMS_EPS) * g
    return h * (1.0 + scale) + shift


def _dot_exact_rhs(x, m):
    hi = x.astype(BF16)
    r1 = x - hi.astype(F32)
    mid = r1.astype(BF16)
    lo = (r1 - mid.astype(F32)).astype(BF16)
    dot = functools.partial(jnp.dot, preferred_element_type=F32)
    return dot(hi, m) + dot(mid, m) + dot(lo, m)


def _mod_spec(n_batch, ctx_tiles):
    g = SAMPLES_PER_STEP
    return pl.BlockSpec((g, 1, D_MODEL), lambda b, i: (jnp.where(i < ctx_tiles, n_batch // g, b), 0, 0))


def _row_spec(width, tile_offset=0):
    return pl.BlockSpec((SAMPLES_PER_STEP, ROW_TILE, width), lambda b, i: (b, i + tile_offset, 0))


def _stack(x):
    return x.reshape(x.shape[0] * x.shape[1], x.shape[2])


def _unstack(x):
    return x.reshape(SAMPLES_PER_STEP, x.shape[0] // SAMPLES_PER_STEP, x.shape[1])


def _whole(shape):
    return pl.BlockSpec(shape, lambda *_: (0,) * len(shape), pipeline_mode=pl.Buffered(1))


def _mod_kernel(c_ref, w_ref, b_ref, o_ref):
    c = c_ref[...]
    cs = c * _sigmoid(c)
    o_ref[0] = jnp.dot(cs, w_ref[0], precision=lax.Precision.HIGHEST,
                       preferred_element_type=F32) + b_ref[0]


def _modulation(c_all, mod_w, mod_b):
    depth, d, n = mod_w.shape
    rows = c_all.shape[0]
    tn = 768
    return pl.pallas_call(
        _mod_kernel,
        grid=(depth, n // tn),
        in_specs=[
            pl.BlockSpec((rows, d), lambda l, j: (0, 0)),
            pl.BlockSpec((1, d, tn), lambda l, j: (l, 0, j)),
            pl.BlockSpec((1, 1, tn), lambda l, j: (l, 0, j)),
        ],
        out_specs=pl.BlockSpec((1, rows, tn), lambda l, j: (l, 0, j)),
        out_shape=jax.ShapeDtypeStruct((depth, rows, n), F32),
        compiler_params=_params("arbitrary", "arbitrary"),
        name="modulation",
    )(c_all, mod_w, mod_b.reshape(depth, 1, n))


def _proj_kernel(x_ref, g_ref, sh_ref, sc_ref, w_ref, qkv_ref, rw_ref, cv_ref):
    hb = _stack(_norm_mod(x_ref[...], g_ref[...], sh_ref[...], sc_ref[...])).astype(BF16)
    for ref, base in ((qkv_ref, 0), (rw_ref, QKV_WIDTH), (cv_ref, QKV_WIDTH + RW_IN)):
        for c0, w in _col_chunks(ref.shape[-1]):
            y = jnp.dot(hb, w_ref[:, base + c0:base + c0 + w], preferred_element_type=F32)
            ref[:, :, c0:c0 + w] = _unstack(y.astype(ref.dtype))


def _in_projection(x, g, shift, scale, w_in_bf16, ctx_tiles):
    b, t, d = x.shape
    ms = _mod_spec(b, ctx_tiles)
    return pl.pallas_call(
        _proj_kernel,
        grid=(b // SAMPLES_PER_STEP, t // ROW_TILE),
        in_specs=[_row_spec(d), _whole((1, d)), ms, ms, _whole((d, IN_WIDTH))],
        out_specs=[_row_spec(QKV_WIDTH), _row_spec(RW_IN), _row_spec(2 * CV_WIDTH)],
        out_shape=[
            jax.ShapeDtypeStruct((b, t, QKV_WIDTH), BF16),
            jax.ShapeDtypeStruct((b, t, RW_IN), F32),
            jax.ShapeDtypeStruct((b, t, 2 * CV_WIDTH), F32),
        ],
        compiler_params=_params("arbitrary", "arbitrary"),
        name="in_projection",
    )(x, g.reshape(1, d), shift, scale, w_in_bf16)


def _head_pair_queries(q):
    lane = lax.broadcasted_iota(jnp.int32, q.shape, 1)
    qs = q * jnp.asarray(HEAD_DIM ** -0.5, q.dtype)
    zero = jnp.zeros_like(qs)
    return jnp.concatenate([jnp.where(lane < HEAD_DIM, qs, zero), jnp.where(lane >= HEAD_DIM, qs, zero)], axis=0)


def _head_pair_merge(o, rows):
    lane = lax.broadcasted_iota(jnp.int32, (rows, LANES), 1)
    return jnp.where(lane < HEAD_DIM, o[:rows], o[rows:])


_NT = (((1,), (1,)), ((), ()))


def _na_kernel(q_ref, k_ref, v_ref, bias_ref, o_ref, *, rows, n_ctx):
    n_win = NA_WIN_ROWS * GRID_W
    nq = 2 * GRID_W
    first = pl.program_id(2) * NA_ROWS_PER_STEP
    q0 = pl.multiple_of(n_ctx + first * GRID_W, GRID_W)
    qb_all = jnp.concatenate(
        [_head_pair_queries(q_ref[0, pl.ds(q0 + j * GRID_W, GRID_W), :]) for j in range(NA_ROWS_PER_STEP)], axis=0)
    s_ctx_all = lax.dot_general(qb_all, k_ref[0, 0:n_ctx, :], _NT, preferred_element_type=F32)
    o_win, p_ctx, denom = [], [], []
    for j in range(NA_ROWS_PER_STEP):
        i = first + j
        rs = jnp.clip(i - NA_WIN_ROWS // 2, 0, rows - NA_WIN_ROWS)
        start = pl.multiple_of(n_ctx + rs * GRID_W, GRID_W)
        bias = bias_ref[0, rs - i + NA_WIN_ROWS - 1]
        s_win = lax.dot_general(qb_all[j * nq:(j + 1) * nq], k_ref[0, pl.ds(start, n_win), :], _NT,
                                preferred_element_type=F32) + bias
        s_ctx = s_ctx_all[j * nq:(j + 1) * nq]
        m = jnp.maximum(jnp.max(s_win, axis=-1, keepdims=True), jnp.max(s_ctx, axis=-1, keepdims=True))
        p_win = jnp.exp(s_win - m)
        p_ctx.append(jnp.exp(s_ctx - m))
        denom.append(jnp.sum(p_win, axis=-1, keepdims=True) + jnp.sum(p_ctx[j], axis=-1, keepdims=True))
        o_win.append(jnp.dot(p_win.astype(BF16), v_ref[0, pl.ds(start, n_win), :], preferred_element_type=F32))
    o_ctx_all = jnp.dot(jnp.concatenate(p_ctx, axis=0).astype(BF16), v_ref[0, 0:n_ctx, :],
                        preferred_element_type=F32)
    for j in range(NA_ROWS_PER_STEP):
        o = (o_win[j] + o_ctx_all[j * nq:(j + 1) * nq]) / denom[j]
        o_ref[0, j * GRID_W:(j + 1) * GRID_W, :] = _head_pair_merge(o, GRID_W).astype(o_ref.dtype)


def _na_bias_table(rpb):
    qc = np.arange(GRID_W)[:, None]
    kc = np.arange(GRID_W)[None, :]
    ws = np.clip(qc - NA_WIN_COLS // 2, 0, GRID_W - NA_WIN_COLS)
    mask = (kc >= ws) & (kc < ws + NA_WIN_COLS)
    rel = np.clip(kc - qc + NA_WIN_COLS - 1, 0, 2 * NA_WIN_COLS - 2)
    full = jnp.where(mask[None, None], rpb[:, :, rel].astype(F32), MASK_VALUE)
    dr = np.arange(NA_WIN_ROWS)[:, None] + np.arange(NA_WIN_ROWS)[None, :]
    t = full[:, dr]
    t = t.transpose(0, 1, 3, 2, 4).reshape(NA_HEADS // 2, 2, NA_WIN_ROWS, GRID_W, NA_WIN_ROWS * GRID_W)
    return t.transpose(0, 2, 1, 3, 4).reshape(NA_HEADS // 2, NA_WIN_ROWS, 2 * GRID_W, NA_WIN_ROWS * GRID_W)


def _neighbourhood_attention(qkv, bias_table, n_ctx):
    b, t, _ = qkv.shape
    n = t - n_ctx
    rows = n // GRID_W
    hp = NA_HEADS // 2
    koff, voff = NA_WIDTH // LANES, 2 * NA_WIDTH // LANES
    qrows = NA_ROWS_PER_STEP * GRID_W
    assert rows % NA_ROWS_PER_STEP == 0
    return pl.pallas_call(
        functools.partial(_na_kernel, rows=rows, n_ctx=n_ctx),
        grid=(b, hp, rows // NA_ROWS_PER_STEP),
        in_specs=[
            pl.BlockSpec((1, t, LANES), lambda bi, h, i: (bi, 0, h)),
            pl.BlockSpec((1, t, LANES), lambda bi, h, i: (bi, 0, koff + h)),
            pl.BlockSpec((1, t, LANES), lambda bi, h, i: (bi, 0, voff + h)),
            pl.BlockSpec((1, NA_WIN_ROWS, 2 * GRID_W, NA_WIN_ROWS * GRID_W), lambda bi, h, i: (h, 0, 0, 0)),
        ],
        out_specs=pl.BlockSpec((1, qrows, LANES), lambda bi, h, i: (bi, i, h)),
        out_shape=jax.ShapeDtypeStruct((b, n, NA_WIDTH), BF16),
        compiler_params=_params("arbitrary", "arbitrary", "arbitrary"),
        name="neighbourhood_attention",
    )(qkv, qkv, qkv, bias_table)


def _ctx_attn_kernel(q_ref, k_ref, v_ref, o_ref):
    l = q_ref.shape[1]
    qb = _head_pair_queries(q_ref[0])
    s = lax.dot_general(qb, k_ref[0], _NT, preferred_element_type=F32)
    p = jnp.exp(s - jnp.max(s, axis=-1, keepdims=True))
    o = jnp.dot(p.astype(BF16), v_ref[0], preferred_element_type=F32) / jnp.sum(p, axis=-1, keepdims=True)
    o_ref[0] = _head_pair_merge(o, l).astype(o_ref.dtype)


def _context_attention(qkv, n_ctx):
    b = qkv.shape[0]
    hp = NA_HEADS // 2
    koff, voff = NA_WIDTH // LANES, 2 * NA_WIDTH // LANES
    return pl.pallas_call(
        _ctx_attn_kernel,
        grid=(b, hp),
        in_specs=[
            pl.BlockSpec((1, n_ctx, LANES), lambda bi, h: (bi, 0, h)),
            pl.BlockSpec((1, n_ctx, LANES), lambda bi, h: (bi, 0, koff + h)),
            pl.BlockSpec((1, n_ctx, LANES), lambda bi, h: (bi, 0, voff + h)),
        ],
        out_specs=pl.BlockSpec((1, n_ctx, LANES), lambda bi, h: (bi, 0, h)),
        out_shape=jax.ShapeDtypeStruct((b, n_ctx, NA_WIDTH), BF16),
        compiler_params=_params("arbitrary", "arbitrary"),
        name="context_attention",
    )(qkv, qkv, qkv)


SCAN_GROUPS = ("r", "v", "z", "w0", "k0", "b0", "w1", "k1", "b1")
POST_COLS = 4 * RW_WIDTH


def _softplus(x):
    return jnp.maximum(x, 0.0) + jnp.log(1.0 + jnp.exp(-jnp.abs(x)))


def _rw_prep_kernel(u_ref, up_ref, un_ref, mup_ref, mun_ref, ones_ref, kk_ref, ka_ref, rk_ref,
                    w0_ref, a0_ref, wa_ref, g2_ref, scan_ref, post_ref, *, n_ctx):
    i = pl.program_id(0)
    b = pl.program_id(1)
    tt = u_ref.shape[1]
    n_tok = pl.num_programs(0) * tt
    u = u_ref[0]
    row = lax.broadcasted_iota(jnp.int32, u.shape, 0)
    starts_segment = jnp.logical_or(i == 0, i * tt == n_ctx)
    ends_segment = jnp.logical_or((i + 1) * tt == n_tok, (i + 1) * tt == n_ctx)
    prev_row = jnp.where(starts_segment, 0.0, up_ref[0, SUBLANES - 1:SUBLANES, :])
    next_row = jnp.where(ends_segment, 0.0, un_ref[0, 0:1, :])
    prev = jnp.where(row == 0, prev_row, pltpu.roll(u, 1, 0))
    nxt = jnp.where(row == tt - 1, next_row, pltpu.roll(u, tt - 1, 0))
    us = u + mup_ref[...] * (prev - u) + mun_ref[...] * (nxt - u)

    def put(group, val):
        g = SCAN_GROUPS.index(group)
        for c in range(2):
            scan_ref[2 * g + c, pl.ds(b, tt, stride=SUBLANES), :] = val[:, c * LANES:(c + 1) * LANES]

    ones = ones_ref[...]
    r = us[:, 0:RW_WIDTH]
    k = us[:, RW_WIDTH:2 * RW_WIDTH]
    v = us[:, 2 * RW_WIDTH:3 * RW_WIDTH]
    kk = k * kk_ref[...]
    kk = kk * lax.rsqrt(_dot_exact_rhs(kk * kk, ones) + 1e-12)
    put("r", r)
    put("v", v)
    put("z", -kk)
    for d in range(2):
        lo = 3 * RW_WIDTH + d * RW_LORA
        wa_in = us[:, lo:lo + LANES]
        lane = lax.broadcasted_iota(jnp.int32, wa_in.shape, 1)
        wa_in = jnp.where(lane < RW_DECAY_RANK, jnp.tanh(wa_in), wa_in)
        wa = jnp.dot(wa_in.astype(BF16), wa_ref[d], preferred_element_type=F32)
        log_w = -_softplus(-(w0_ref[d] + wa[:, 0:RW_WIDTH])) - 0.5
        decay = jnp.exp(-jnp.exp(log_w))
        a = _sigmoid(a0_ref[d] + wa[:, RW_WIDTH:2 * RW_WIDTH])
        gate_in = _sigmoid(us[:, lo + LANES:lo + 2 * LANES])
        g = jnp.dot(gate_in.astype(BF16), g2_ref[d], preferred_element_type=F32)
        kd = k * (1.0 + (a - 1.0) * ka_ref[...])
        bonus = _dot_exact_rhs(r * kd * rk_ref[...], ones) * v
        put("w%d" % d, decay)
        put("k%d" % d, kd)
        put("b%d" % d, kk * a)
        post_ref[0, :, 2 * d * RW_WIDTH:(2 * d + 1) * RW_WIDTH] = g
        post_ref[0, :, (2 * d + 1) * RW_WIDTH:(2 * d + 2) * RW_WIDTH] = bonus


def _head_ones():
    h = np.arange(RW_WIDTH) // HEAD_DIM
    return jnp.asarray(h[:, None] == h[None, :], BF16)


def _rw_prep(u, p, n_ctx):
    b, t, _ = u.shape
    tt = RW_TILE
    assert b == SUBLANES and n_ctx % tt == 0 and t % tt == 0
    nt8 = t // SUBLANES
    row = lambda vec, n: vec.reshape(1, n)
    return pl.pallas_call(
        functools.partial(_rw_prep_kernel, n_ctx=n_ctx),
        grid=(t // tt, b),
        in_specs=[
            pl.BlockSpec((1, tt, RW_IN), lambda i, bi: (bi, i, 0)),
            pl.BlockSpec((1, SUBLANES, RW_IN), lambda i, bi: (bi, jnp.maximum(i * (tt // SUBLANES) - 1, 0), 0)),
            pl.BlockSpec((1, SUBLANES, RW_IN), lambda i, bi: (bi, jnp.minimum((i + 1) * (tt // SUBLANES), nt8 - 1), 0)),
            _whole((1, RW_IN)), _whole((1, RW_IN)),
            _whole((RW_WIDTH, RW_WIDTH)),
            _whole((1, RW_WIDTH)), _whole((1, RW_WIDTH)), _whole((1, RW_WIDTH)),
            _whole((2, 1, RW_WIDTH)), _whole((2, 1, RW_WIDTH)),
            _whole((2, LANES, 2 * RW_WIDTH)), _whole((2, RW_GATE_RANK, RW_WIDTH)),
        ],
        out_specs=[
            pl.BlockSpec((2 * len(SCAN_GROUPS), tt * b, LANES), lambda i, bi: (0, i, 0)),
            pl.BlockSpec((1, tt, POST_COLS), lambda i, bi: (bi, i, 0)),
        ],
        out_shape=[
            jax.ShapeDtypeStruct((2 * len(SCAN_GROUPS), t * b, LANES), F32),
            jax.ShapeDtypeStruct((b, t, POST_COLS), F32),
        ],
        compiler_params=_params("arbitrary", "arbitrary"),
        name="rwkv_prep",
    )(u, u, u, row(p["mu_prev"], RW_IN), row(p["mu_next"], RW_IN), _head_ones(),
      row(p["k_k"], RW_WIDTH), row(p["k_a"], RW_WIDTH), row(p["r_k"], RW_WIDTH),
      p["w0"].reshape(2, 1, RW_WIDTH), p["a0"].reshape(2, 1, RW_WIDTH), p["wa"], p["g2"])


def _rw_lora_weights(w2, a2):
    z = jnp.zeros_like(w2)
    top = jnp.concatenate([w2, z], axis=-1)
    bot = jnp.concatenate([z, a2], axis=-1)
    return jnp.concatenate([top, bot], axis=1).astype(BF16)


_QUARTER = LANES // 4
_VROWS = HEAD_DIM // 4


def _chain_tile(f_ref, b_ref, sf, sb):
    pieces = [f_ref[0, pl.ds(sf, SUBLANES), :], f_ref[1, pl.ds(sf, SUBLANES), :],
              b_ref[0, pl.ds(sb, SUBLANES), :], b_ref[1, pl.ds(sb, SUBLANES), :]]
    return jnp.concatenate(pieces * 4, axis=0).T


def _scan_kernel(rf, vf, zf, wf, kf, bf, rb, vb, zb, wb, kb, bb, yf_ref, yb_ref, s_ref, t_ref, v_ref, y_ref):
    steps = rf.shape[1] // SUBLANES
    zero = jnp.zeros((_VROWS, LANES), F32)

    @pl.when(pl.program_id(0) == 0)
    def _():
        s_ref[...] = jnp.zeros_like(s_ref)
        y_ref[...] = jnp.zeros_like(y_ref)

    lane_q = lax.broadcasted_iota(jnp.int32, (_VROWS, LANES), 1) // _QUARTER
    col_q = (lax.broadcasted_iota(jnp.int32, (_QUARTER, LANES), 1) % HEAD_DIM) // _VROWS

    def offsets(s):
        s = jnp.clip(s, 0, steps - 1)
        return pl.multiple_of(s * SUBLANES, SUBLANES), pl.multiple_of((steps - 1 - s) * SUBLANES, SUBLANES)

    def prepare(s, p):
        sf, sb = offsets(s)
        zf_off, zb_off = offsets(s + 1)
        t_ref[p, 0] = _chain_tile(zf, zb, zf_off, zb_off)
        for n, (f, b) in enumerate(((wf, wb), (bf, bb), (kf, kb), (rf, rb))):
            t_ref[p, n + 1] = _chain_tile(f, b, sf, sb)
        vt = _chain_tile(vf, vb, sf, sb)
        for h in range(2):
            v = zero
            for q in range(4):
                lo = h * HEAD_DIM + q * _VROWS
                v = jnp.where(lane_q == q, vt[lo:lo + _VROWS], v)
            v_ref[p, h] = v

    def flush(s, p):
        sf, sb = offsets(s)
        yt = jnp.concatenate([y_ref[p, 0]] * 4 + [y_ref[p, 1]] * 4, axis=0).T
        nat = jnp.zeros((_QUARTER, LANES), F32)
        for q in range(4):
            nat = jnp.where(col_q == q, yt[q * _QUARTER:(q + 1) * _QUARTER], nat)
        yf_ref[0, pl.ds(sf, SUBLANES), :] = nat[0:8]
        yf_ref[1, pl.ds(sf, SUBLANES), :] = nat[8:16]
        yb_ref[0, pl.ds(sb, SUBLANES), :] = nat[16:24]
        yb_ref[1, pl.ds(sb, SUBLANES), :] = nat[24:32]

    def update(p, sz):
        sz_next = []
        for h in range(2):
            base = h * HEAD_DIM
            v = v_ref[p, h]
            y = [zero, zero]
            zn = [zero, zero]
            for k in range(HEAD_DIM):
                row = pl.ds(base + k, 1)
                st = s_ref[base + k] * t_ref[p, 1, row, :] + sz[h] * t_ref[p, 2, row, :] + v * t_ref[p, 3, row, :]
                s_ref[base + k] = st
                y[k % 2] = y[k % 2] + st * t_ref[p, 4, row, :]
                zn[k % 2] = zn[k % 2] + st * t_ref[p, 0, row, :]
            y_ref[p, h] = y[0] + y[1]
            sz_next.append(zn[0] + zn[1])
        return tuple(sz_next)

    f0, b0 = offsets(0)
    t_ref[1, 0] = _chain_tile(zf, zb, f0, b0)
    sz0 = []
    for h in range(2):
        acc = [zero, zero]
        for k in range(HEAD_DIM):
            r = h * HEAD_DIM + k
            acc[k % 2] = acc[k % 2] + s_ref[r] * t_ref[1, 0, pl.ds(r, 1), :]
        sz0.append(acc[0] + acc[1])
    prepare(0, 0)

    def pair(j, sz):
        s = 2 * j
        for p in range(2):
            prepare(s + p + 1, 1 - p)
            sz = update(p, sz)
            flush(s + p - 1, 1 - p)
        return sz

    lax.fori_loop(0, steps // 2, pair, tuple(sz0))
    flush(steps - 1, 1)


def _rwkv7_scan(ops, n_batch, n_ctx):
    rows = ops.shape[1]
    blk = SCAN_STEPS * n_batch
    nb = rows // blk
    nb_c = n_ctx * n_batch // blk

    def mirror(i):
        return jnp.where(i < nb_c, nb_c - 1 - i, nb_c + nb - 1 - i)

    def spec(group, backward):
        g = SCAN_GROUPS.index(group)
        if backward:
            return pl.BlockSpec((2, blk, LANES), lambda i: (g, mirror(i), 0))
        return pl.BlockSpec((2, blk, LANES), lambda i: (g, i, 0))

    fw = [spec(g, False) for g in ("r", "v", "z", "w0", "k0", "b0")]
    bw = [spec(g, True) for g in ("r", "v", "z", "w1", "k1", "b1")]
    out_sds = jax.ShapeDtypeStruct((2, rows, LANES), F32)
    return pl.pallas_call(
        _scan_kernel,
        grid=(nb,),
        in_specs=fw + bw,
        out_specs=[pl.BlockSpec((2, blk, LANES), lambda i: (0, i, 0)),
                   pl.BlockSpec((2, blk, LANES), lambda i: (0, mirror(i), 0))],
        out_shape=[out_sds, out_sds],
        scratch_shapes=[pltpu.VMEM((2 * HEAD_DIM, _VROWS, LANES), F32), pltpu.VMEM((2, 5, LANES, LANES), F32),
                        pltpu.VMEM((2, 2, _VROWS, LANES), F32), pltpu.VMEM((2, 2, _VROWS, LANES), F32)],
        compiler_params=_params("arbitrary"),
        name="rwkv_scan",
    )(*([ops] * 12))


def _rw_post_kernel(yf_ref, yb_ref, post_ref, ones_ref, gg_ref, gb_ref, o_ref):
    b = pl.program_id(1)
    tt = o_ref.shape[1]
    ones = ones_ref[...]
    out = None
    for d, y_ref in enumerate((yf_ref, yb_ref)):
        y = jnp.concatenate([y_ref[0, pl.ds(b, tt, stride=SUBLANES), :],
                             y_ref[1, pl.ds(b, tt, stride=SUBLANES), :]], axis=1)
        mu = _dot_exact_rhs(y, ones) * (1.0 / HEAD_DIM)
        yc = y - mu
        var = _dot_exact_rhs(yc * yc, ones) * (1.0 / HEAD_DIM)
        yn = yc * lax.rsqrt(var + RW_GN_EPS) * gg_ref[...] + gb_ref[...]
        g = post_ref[0, :, 2 * d * RW_WIDTH:(2 * d + 1) * RW_WIDTH]
        bonus = post_ref[0, :, (2 * d + 1) * RW_WIDTH:(2 * d + 2) * RW_WIDTH]
        term = (yn + bonus) * g
        out = term if out is None else out + term
    o_ref[0] = out.astype(o_ref.dtype)


def _rw_post(y_f, y_b, post, gn_g, gn_b):
    b, t, _ = post.shape
    tt = RW_TILE
    yspec = pl.BlockSpec((2, tt * b, LANES), lambda i, bi: (0, i, 0))
    return pl.pallas_call(
        _rw_post_kernel,
        grid=(t // tt, b),
        in_specs=[yspec, yspec, pl.BlockSpec((1, tt, POST_COLS), lambda i, bi: (bi, i, 0)),
                  _whole((RW_WIDTH, RW_WIDTH)), _whole((1, RW_WIDTH)), _whole((1, RW_WIDTH))],
        out_specs=pl.BlockSpec((1, tt, RW_WIDTH), lambda i, bi: (bi, i, 0)),
        out_shape=jax.ShapeDtypeStruct((b, t, RW_WIDTH), BF16),
        compiler_params=_params("arbitrary", "arbitrary"),
        name="rwkv_post",
    )(y_f, y_b, post, _head_ones(), gn_g.reshape(1, RW_WIDTH), gn_b.reshape(1, RW_WIDTH))


def _bi_rwkv7(u, p, n_ctx):
    ops, post = _rw_prep(u, p, n_ctx)
    y_f, y_b = _rwkv7_scan(ops, u.shape[0], n_ctx)
    return _rw_post(y_f, y_b, post, p["gn_g"], p["gn_b"])


CONV_PAD = 16


def _conv_kernel(u_ref, w_ref, b_ref, lg_ref, lb_ref, o_ref, pad_ref, shift_ref, *, n_ctx):
    t = u_ref.shape[1]
    half = CV_CONV_LEN // 2
    chunk = ROW_TILE
    u = u_ref[0]
    h = u[:, :CV_WIDTH] * _sigmoid(u[:, CV_WIDTH:])
    zeros = jnp.zeros((CONV_PAD, CV_WIDTH), F32)
    lat0 = 2 * CONV_PAD + n_ctx
    pad_ref[0:CONV_PAD, :] = zeros
    pad_ref[CONV_PAD:CONV_PAD + n_ctx, :] = h[:n_ctx]
    pad_ref[CONV_PAD + n_ctx:lat0, :] = zeros
    pad_ref[lat0:lat0 + t - n_ctx, :] = h[n_ctx:]
    pad_ref[lat0 + t - n_ctx:lat0 + t - n_ctx + CONV_PAD, :] = zeros
    w = w_ref[...]

    def body(c, carry):
        out0 = pl.multiple_of(c * chunk, chunk)
        base = pl.multiple_of(out0 + jnp.where(c >= n_ctx // chunk, CONV_PAD, 0), CONV_PAD)
        win = pad_ref[pl.ds(base, chunk + 2 * CONV_PAD), :]
        acc = jnp.zeros((chunk, CV_WIDTH), F32)
        for r in range(SUBLANES):
            offs = [o for o in range(CONV_PAD - half, CONV_PAD + half + 1) if o % SUBLANES == r]
            if not offs:
                continue
            shift_ref[r] = win[r:r + shift_ref.shape[1], :]
            for off in offs:
                j = off - (CONV_PAD - half)
                acc = acc + shift_ref[r, off - r:off - r + chunk, :] * w[j:j + 1, :]
        hh = acc + b_ref[...]
        mu = jnp.mean(hh, axis=-1, keepdims=True)
        hc = hh - mu
        var = jnp.mean(hc * hc, axis=-1, keepdims=True)
        y = hc * lax.rsqrt(var + LN_EPS) * lg_ref[...] + lb_ref[...]
        o_ref[0, pl.ds(out0, chunk), :] = (y * _sigmoid(y)).astype(o_ref.dtype)
        return carry

    lax.fori_loop(0, t // chunk, body, 0)


def _conformer_conv(u, dw_w, dw_b, ln_g, ln_b, n_ctx):
    b, t, _ = u.shape
    assert n_ctx % ROW_TILE == 0 and t % ROW_TILE == 0
    row = lambda vec: vec.reshape(1, CV_WIDTH)
    return pl.pallas_call(
        functools.partial(_conv_kernel, n_ctx=n_ctx),
        grid=(b,),
        in_specs=[pl.BlockSpec((1, t, 2 * CV_WIDTH), lambda bi: (bi, 0, 0)),
                  _whole((CV_CONV_LEN, CV_WIDTH)), _whole((1, CV_WIDTH)), _whole((1, CV_WIDTH)),
                  _whole((1, CV_WIDTH))],
        out_specs=pl.BlockSpec((1, t, CV_WIDTH), lambda bi: (bi, 0, 0)),
        out_shape=jax.ShapeDtypeStruct((b, t, CV_WIDTH), BF16),
        scratch_shapes=[pltpu.VMEM((t + 3 * CONV_PAD, CV_WIDTH), F32),
                        pltpu.VMEM((SUBLANES, ROW_TILE + 2 * CONV_PAD - SUBLANES, CV_WIDTH), F32)],
        compiler_params=_params("arbitrary"),
        name="conformer_conv",
    )(u, dw_w, row(dw_b), row(ln_g), row(ln_b))


def _wout_kernel(x_ref, nal_ref, nac_ref, rw_ref, cv_ref, w_ref, gate_ref, o_ref, *, ctx_tiles):
    dot = functools.partial(jnp.dot, preferred_element_type=F32)
    na = nal_ref[...]
    if ctx_tiles:
        na = jnp.where(pl.program_id(1) < ctx_tiles, nac_ref[...], na)
    acc = (dot(_stack(na), w_ref[0:NA_WIDTH, :])
           + dot(_stack(rw_ref[...]), w_ref[NA_WIDTH:NA_WIDTH + RW_WIDTH, :])
           + dot(_stack(cv_ref[...]), w_ref[NA_WIDTH + RW_WIDTH:, :]))
    o_ref[...] = x_ref[...] + gate_ref[...] * _unstack(acc)


def _out_projection(x, na_l, na_c, rw, cv, w_out_bf16, gate, n_ctx, with_ctx):
    b, t, d = x.shape
    ctx_tiles = n_ctx // ROW_TILE
    assert ctx_tiles == 1 or not with_ctx
    off = 0 if with_ctx else ctx_tiles
    n_rows = t if with_ctx else t - n_ctx
    g = SAMPLES_PER_STEP
    if with_ctx:
        nal_spec = pl.BlockSpec((g, ROW_TILE, NA_WIDTH), lambda bi, i: (bi, jnp.maximum(i - ctx_tiles, 0), 0))
    else:
        nal_spec = _row_spec(NA_WIDTH)
        na_c = na_l
    return pl.pallas_call(
        functools.partial(_wout_kernel, ctx_tiles=ctx_tiles if with_ctx else 0),
        grid=(b // g, n_rows // ROW_TILE),
        in_specs=[_row_spec(d, off), nal_spec,
                  pl.BlockSpec((g, ROW_TILE, NA_WIDTH), lambda bi, i: (bi, 0, 0)),
                  _row_spec(RW_WIDTH, off), _row_spec(CV_WIDTH, off),
                  _whole((d, d)), _mod_spec(b, ctx_tiles if with_ctx else 0)],
        out_specs=_row_spec(d),
        out_shape=jax.ShapeDtypeStruct((b, n_rows, d), F32),
        compiler_params=_params("arbitrary", "arbitrary"),
        name="out_projection",
    )(x, na_l, na_c, rw, cv, w_out_bf16, gate)


def _ffn_kernel(x_ref, g_ref, sh_ref, sc_ref, gate_ref, wg_ref, wu_ref, wd_ref, o_ref, acc_ref):
    x = x_ref[...]
    hb = _stack(_norm_mod(x, g_ref[...], sh_ref[...], sc_ref[...])).astype(BF16)
    dot = functools.partial(jnp.dot, preferred_element_type=F32)
    for n, (c0, w) in enumerate(_col_chunks(wg_ref.shape[1])):
        g = dot(hb, wg_ref[:, c0:c0 + w])
        u = dot(hb, wu_ref[:, c0:c0 + w])
        a = (g * _sigmoid(g) * u).astype(BF16)
        y = dot(a, wd_ref[c0:c0 + w, :])
        if n == 0:
            acc_ref[...] = y
        else:
            acc_ref[...] += y
    o_ref[...] = x + gate_ref[...] * _unstack(acc_ref[...])


def _dense_ffn(x, g, shift, scale, gate, wg, wu, wd, ctx_tiles):
    b, t, d = x.shape
    dff = wg.shape[1]
    ms = _mod_spec(b, ctx_tiles)
    return pl.pallas_call(
        _ffn_kernel,
        grid=(b // SAMPLES_PER_STEP, t // ROW_TILE),
        in_specs=[_row_spec(d), _whole((1, d)), ms, ms, ms,
                  _whole((d, dff)), _whole((d, dff)), _whole((dff, d))],
        out_specs=_row_spec(d),
        out_shape=jax.ShapeDtypeStruct((b, t, d), F32),
        scratch_shapes=[pltpu.VMEM((SAMPLES_PER_STEP * ROW_TILE, d), F32)],
        compiler_params=_params("arbitrary", "arbitrary"),
        name="dense_ffn",
    )(x, g.reshape(1, d), shift, scale, gate, wg, wu, wd)


def _rows_to_tiles(ref2d, val):
    rows = val.shape[0]
    for s in range(val.shape[1] // LANES):
        ref2d[pl.ds(s, rows, stride=SUBLANES), :] = val[:, s * LANES:(s + 1) * LANES]


def _rows_from_tiles(ref, lead, first_row, rows):
    return jnp.concatenate([ref[lead, pl.ds(first_row * SUBLANES + s, rows, stride=SUBLANES), :]
                            for s in range(SUBLANES)], axis=1)


def _router_kernel(x_ref, g_ref, sh_ref, sc_ref, wr_ref, h_ref, logit_ref):
    h = _norm_mod(x_ref[...], g_ref[...], sh_ref[...], sc_ref[...])
    for s in range(h.shape[0]):
        _rows_to_tiles(h_ref.at[s], h[s])
    logit_ref[...] = _unstack(jnp.dot(_stack(h), wr_ref[...], precision=lax.Precision.HIGHEST,
                                      preferred_element_type=F32))


def _moe_router(x, g, shift, scale, router_pad):
    b, t, d = x.shape
    assert d == SUBLANES * LANES
    ms = _mod_spec(b, 0)
    return pl.pallas_call(
        _router_kernel,
        grid=(b // SAMPLES_PER_STEP, t // ROW_TILE),
        in_specs=[_row_spec(d), _whole((1, d)), ms, ms, _whole((d, ROUTER_PAD))],
        out_specs=[pl.BlockSpec((SAMPLES_PER_STEP, ROW_TILE * SUBLANES, LANES), lambda bi, i: (bi, i, 0)),
                   _row_spec(ROUTER_PAD)],
        out_shape=[jax.ShapeDtypeStruct((b, t * SUBLANES, LANES), F32),
                   jax.ShapeDtypeStruct((b, t, ROUTER_PAD), F32)],
        compiler_params=_params("arbitrary", "arbitrary"),
        name="moe_router",
    )(x, g.reshape(1, d), shift, scale, router_pad)


def _tile_copy(src_hbm, dst_vmem, sem, src_row, dst_row):
    dst = dst_vmem.at[pl.ds(pl.multiple_of(dst_row * SUBLANES, SUBLANES), SUBLANES)]
    return pltpu.make_async_copy(src_hbm.at[src_row], dst, sem)


GATHER_UNROLL = 8


def _start_row_gather(src_hbm, dst_vmem, sem, index_of_row, rows):
    def body(r, c):
        _tile_copy(src_hbm, dst_vmem, sem, index_of_row(r), r).start()
        return c

    lax.fori_loop(0, rows, body, 0, unroll=GATHER_UNROLL)


def _wait_row_gather(src_hbm, dst_vmem, sem, rows):
    def body(r, c):
        _tile_copy(src_hbm, dst_vmem, sem, 0, r).wait()
        return c

    lax.fori_loop(0, rows, body, 0, unroll=GATHER_UNROLL)


def _expert_kernel(be_ref, nused_ref, tok_ref, tok_next_ref, h_hbm, wg_ref, wu_ref, wd_ref, o_ref,
                   xg_ref, acc_ref, sem):
    i = pl.program_id(0)
    rows = o_ref.shape[0] // SUBLANES
    n_used = nused_ref[0]
    used = i < n_used
    slot = i % 2

    @pl.when(i == 0)
    def _():
        _start_row_gather(h_hbm, xg_ref.at[0], sem.at[0], lambda r: tok_ref[0, 0, r], rows)

    @pl.when(i + 1 < n_used)
    def _():
        _start_row_gather(h_hbm, xg_ref.at[1 - slot], sem.at[1 - slot], lambda r: tok_next_ref[0, 0, r], rows)

    @pl.when(used)
    def _():
        _wait_row_gather(h_hbm, xg_ref.at[slot], sem.at[slot], rows)
        dot = functools.partial(jnp.dot, preferred_element_type=F32)
        xb = _rows_from_tiles(xg_ref, slot, 0, rows).astype(BF16)
        for n, (c0, w) in enumerate(_col_chunks(wg_ref.shape[2], MOE_FF_TILE)):
            g = dot(xb, wg_ref[0, :, c0:c0 + w])
            u = dot(xb, wu_ref[0, :, c0:c0 + w])
            a = (g * _sigmoid(g) * u).astype(BF16)
            y = dot(a, wd_ref[0, c0:c0 + w, :])
            if n == 0:
                acc_ref[...] = y
            else:
                acc_ref[...] += y
        _rows_to_tiles(o_ref, acc_ref[...])

    @pl.when(jnp.logical_not(used))
    def _():
        o_ref[...] = jnp.zeros_like(o_ref)


def _moe_experts(h_flat, slot_tok, block_e, n_used, wg, wu, wd):
    nb = slot_tok.shape[0]
    d = wg.shape[1]
    dff = wg.shape[2]
    once = pl.Buffered(1)
    grid_spec = pltpu.PrefetchScalarGridSpec(
        num_scalar_prefetch=2,
        grid=(nb,),
        in_specs=[
            pl.BlockSpec((1, 1, MOE_ROWS), lambda i, be, nu: (i, 0, 0), memory_space=pltpu.SMEM),
            pl.BlockSpec((1, 1, MOE_ROWS), lambda i, be, nu: (jnp.minimum(i + 1, nb - 1), 0, 0),
                         memory_space=pltpu.SMEM),
            pl.BlockSpec(memory_space=pl.ANY),
            pl.BlockSpec((1, d, dff), lambda i, be, nu: (be[i], 0, 0), pipeline_mode=once),
            pl.BlockSpec((1, d, dff), lambda i, be, nu: (be[i], 0, 0), pipeline_mode=once),
            pl.BlockSpec((1, dff, d), lambda i, be, nu: (be[i], 0, 0), pipeline_mode=once),
        ],
        out_specs=pl.BlockSpec((MOE_ROWS * SUBLANES, LANES), lambda i, be, nu: (i, 0)),
        scratch_shapes=[pltpu.VMEM((2, MOE_ROWS * SUBLANES, LANES), F32), pltpu.VMEM((MOE_ROWS, d), F32),
                        pltpu.SemaphoreType.DMA((2,))],
    )
    yb = pl.pallas_call(
        _expert_kernel,
        grid_spec=grid_spec,
        out_shape=jax.ShapeDtypeStruct((nb * MOE_ROWS * SUBLANES, LANES), F32),
        compiler_params=_params("arbitrary"),
        name="moe_experts",
    )(block_e, n_used, slot_tok, slot_tok, h_flat, wg, wu, wd)
    return yb.reshape(nb * MOE_ROWS, SUBLANES, LANES)


def _combine_kernel(dest_ref, dest_next_ref, x_ref, gates_ref, gate2_ref, fg_ref, yb_hbm, o_ref, y_ref, sem):
    i = pl.program_id(0)
    rows = x_ref.shape[1]
    slot = i % 2

    @pl.when(i == 0)
    def _():
        _start_row_gather(yb_hbm, y_ref.at[0], sem.at[0], lambda r: dest_ref[0, 0, r], TOP_K * rows)

    @pl.when(i + 1 < pl.num_programs(0))
    def _():
        _start_row_gather(yb_hbm, y_ref.at[1 - slot], sem.at[1 - slot], lambda r: dest_next_ref[0, 0, r],
                          TOP_K * rows)

    _wait_row_gather(yb_hbm, y_ref.at[slot], sem.at[slot], TOP_K * rows)
    gates = gates_ref[0]
    y = (_rows_from_tiles(y_ref, slot, 0, rows) * gates[:, 0:1]
         + _rows_from_tiles(y_ref, slot, rows, rows) * gates[:, 1:2])
    x = x_ref[0] + gate2_ref[0] * y
    ms = jnp.mean(x * x, axis=-1, keepdims=True)
    o_ref[0] = x * lax.rsqrt(ms + RMS_EPS) * fg_ref[...]


def _moe_combine_final(x, gates, dest, yb, gate2, final_g):
    nt, tm, d = x.shape
    tiles_per_sample = nt // gate2.shape[0]
    tile = lambda w: pl.BlockSpec((1, tm, w), lambda i: (i, 0, 0))
    return pl.pallas_call(
        _combine_kernel,
        grid=(nt,),
        in_specs=[
            pl.BlockSpec((1, 1, TOP_K * tm), lambda i: (i, 0, 0), memory_space=pltpu.SMEM),
            pl.BlockSpec((1, 1, TOP_K * tm), lambda i: (jnp.minimum(i + 1, nt - 1), 0, 0),
                         memory_space=pltpu.SMEM),
            tile(d), tile(TOP_K),
            pl.BlockSpec((1, 1, d), lambda i: (i // tiles_per_sample, 0, 0)),
            pl.BlockSpec((1, d), lambda i: (0, 0)),
            pl.BlockSpec(memory_space=pl.ANY),
        ],
        out_specs=tile(d),
        out_shape=jax.ShapeDtypeStruct((nt, tm, d), F32),
        scratch_shapes=[pltpu.VMEM((2, TOP_K * tm * SUBLANES, LANES), F32), pltpu.SemaphoreType.DMA((2,))],
        compiler_params=_params("arbitrary"),
        name="moe_combine_final",
    )(dest, dest, x, gates, gate2, final_g.reshape(1, d), yb)


def _moe_layer_final(x, g, shift, scale, gate2, router, wg, wu, wd, final_g):
    b, t, d = x.shape
    n_tok = b * t
    n_asg = n_tok * TOP_K
    router_pad = jnp.pad(router, ((0, 0), (0, ROUTER_PAD - N_EXPERTS)))
    h, logits = _moe_router(x, g, shift, scale, router_pad)
    logits = logits.reshape(n_tok, ROUTER_PAD)[:, :N_EXPERTS]
    top_logit, top_e = lax.top_k(logits, TOP_K)
    gates = jax.nn.softmax(top_logit, axis=-1)
    flat_e = top_e.reshape(-1)
    onehot = (flat_e[:, None] == jnp.arange(N_EXPERTS, dtype=flat_e.dtype)[None, :]).astype(jnp.int32)
    rank = jnp.cumsum(onehot, axis=0) - onehot
    counts = jnp.sum(onehot, axis=0)
    padded = (counts + MOE_ROWS - 1) // MOE_ROWS * MOE_ROWS
    pad_end = jnp.cumsum(padded)
    pad_start = pad_end - padded
    dest = (pad_start[flat_e] + jnp.sum(rank * onehot, axis=1)).astype(jnp.int32)
    n_blocks = (n_asg + MOE_ROWS - 1) // MOE_ROWS + N_EXPERTS
    block_e = jnp.minimum(jnp.searchsorted(pad_end, jnp.arange(n_blocks) * MOE_ROWS, side="right"),
                          N_EXPERTS - 1).astype(jnp.int32)
    order = jnp.argsort(flat_e, stable=True).astype(jnp.int32)
    slot = jnp.arange(n_blocks * MOE_ROWS, dtype=jnp.int32)
    slot_e = jnp.repeat(block_e, MOE_ROWS)
    slot_rank = slot - pad_start[slot_e].astype(jnp.int32)
    first = (jnp.cumsum(counts) - counts).astype(jnp.int32)
    src = jnp.clip(first[slot_e] + slot_rank, 0, n_asg - 1)
    slot_tok = jnp.where(slot_rank < counts[slot_e], order[src] // TOP_K, 0).astype(jnp.int32)
    n_used = (pad_end[-1:] // MOE_ROWS).astype(jnp.int32)
    yb = _moe_experts(h.reshape(n_tok, SUBLANES, LANES), slot_tok.reshape(n_blocks, 1, MOE_ROWS), block_e, n_used,
                      wg, wu, wd)
    nt = n_tok // ROW_TILE
    tile_dest = dest.reshape(nt, ROW_TILE, TOP_K).transpose(0, 2, 1).reshape(nt, 1, TOP_K * ROW_TILE)
    out = _moe_combine_final(x.reshape(nt, ROW_TILE, d), gates.reshape(nt, ROW_TILE, TOP_K), tile_dest, yb,
                             gate2[:b], final_g)
    return out.reshape(b, t, d)


def _final_norm_kernel(x_ref, g_ref, o_ref):
    x = x_ref[...]
    ms = jnp.mean(x * x, axis=-1, keepdims=True)
    o_ref[...] = x * lax.rsqrt(ms + RMS_EPS) * g_ref[...]


def _final_norm(x, g, tile_offset):
    b, t, d = x.shape
    n_rows = t - tile_offset * ROW_TILE
    return pl.pallas_call(
        _final_norm_kernel,
        grid=(b // SAMPLES_PER_STEP, n_rows // ROW_TILE),
        in_specs=[_row_spec(d, tile_offset), _whole((1, d))],
        out_specs=_row_spec(d),
        out_shape=jax.ShapeDtypeStruct((b, n_rows, d), F32),
        compiler_params=_params("arbitrary", "arbitrary"),
        name="final_norm",
    )(x, g.reshape(1, d))


def kernel(x, c, ctx, c_ctx, norm1_g, norm2_g, mod_w, mod_b, w_in, w_out, na_rpb, rw_mu_prev, rw_mu_next, rw_w0, rw_w2, rw_a0, rw_a2, rw_g2, rw_k_k, rw_k_a, rw_r_k, rw_gn_g, rw_gn_b, cv_dw_w, cv_dw_b, cv_ln_g, cv_ln_b, ffn_w_gate, ffn_w_up, ffn_w_down, moe_router, moe_w_gate, moe_w_up, moe_w_down, final_g):
    b, n, _ = x.shape
    n_ctx = ctx.shape[1]
    depth = mod_w.shape[0]
    assert n_ctx == ROW_TILE and n % ROW_TILE == 0 and b == SUBLANES
    ctx_tiles = n_ctx // ROW_TILE
    c_rows = 2 * SUBLANES
    n_mod = b + SAMPLES_PER_STEP
    c_all = jnp.concatenate([c, jnp.broadcast_to(c_ctx[None, :], (SAMPLES_PER_STEP, D_MODEL)),
                             jnp.zeros((c_rows - n_mod, D_MODEL), F32)], axis=0)
    mod = _modulation(c_all, mod_w, mod_b)

    xa = jnp.concatenate([ctx, x], axis=1)
    out = None
    for layer in range(depth):
        last = layer == depth - 1
        m = mod[layer, :n_mod].reshape(n_mod, 6, 1, D_MODEL)
        sh1, sc1, g1, sh2, sc2, g2 = (m[:, k] for k in range(6))
        w_in_b = w_in[layer].astype(BF16)
        w_out_b = w_out[layer].astype(BF16)
        qkv, rw_in, cv_in = _in_projection(xa, norm1_g[layer], sh1, sc1, w_in_b, ctx_tiles)
        na_l = _neighbourhood_attention(qkv, _na_bias_table(na_rpb[layer]), n_ctx)
        rw_p = dict(mu_prev=rw_mu_prev[layer], mu_next=rw_mu_next[layer], w0=rw_w0[layer], a0=rw_a0[layer],
                    wa=_rw_lora_weights(rw_w2[layer], rw_a2[layer]), g2=rw_g2[layer].astype(BF16),
                    k_k=rw_k_k[layer], k_a=rw_k_a[layer], r_k=rw_r_k[layer].reshape(-1),
                    gn_g=rw_gn_g[layer], gn_b=rw_gn_b[layer])
        rw_o = _bi_rwkv7(rw_in, rw_p, n_ctx)
        cv_o = _conformer_conv(cv_in, cv_dw_w[layer], cv_dw_b[layer], cv_ln_g[layer], cv_ln_b[layer], n_ctx)
        na_c = None if last else _context_attention(qkv, n_ctx)
        xa = _out_projection(xa, na_l, na_c, rw_o, cv_o, w_out_b, g1, n_ctx, with_ctx=not last)
        j = layer // 2
        if layer % 2 == 0:
            ffn_w = (ffn_w_gate[j].astype(BF16), ffn_w_up[j].astype(BF16), ffn_w_down[j].astype(BF16))
            xa = _dense_ffn(xa, norm2_g[layer], sh2, sc2, g2, *ffn_w, 0 if last else ctx_tiles)
            if last:
                out = _final_norm(xa, final_g, 0)
        else:
            if not last:
                raise NotImplementedError("context tokens through a mixture-of-experts layer")
            moe_w = (moe_w_gate[j].astype(BF16), moe_w_up[j].astype(BF16), moe_w_down[j].astype(BF16))
            out = _moe_layer_final(xa, norm2_g[layer], sh2, sc2, g2, moe_router[j], *moe_w, final_g)
    return out
```
